```python
import jax, jax.numpy as jnp
from jax import lax
import numpy as np

D_MODEL = 2048
BATCH = 8
SEQ = 2048
DEPTH = 1
DEC_BATCH = 16
DEC_SEQ = 16
PAST_LEN = 2048

CHUNK = 64
N_HEADS = 16
HEAD_DIM = 64
ATTN_W = N_HEADS * HEAD_DIM
POOL_WINDOWS = (2, 4, 8, 16)
POOL_GROUPS = 4
POOL_W = 1024
POOL_GW = POOL_W // POOL_GROUPS
POOL_OUT_GW = D_MODEL // POOL_GROUPS
POOL_HIST = 15
D_FF = 4 * D_MODEL
Q_BLOCK = 128
EPS = 1e-6
SCALE = HEAD_DIM ** -0.5
OFF_Q = 0
OFF_K = OFF_Q + ATTN_W
OFF_V = OFF_K + ATTN_W
OFF_F = OFF_V + ATTN_W
OFF_U = OFF_F + N_HEADS
OFF_GA = OFF_U + POOL_W
OFF_GB = OFF_GA + D_MODEL
IN_W = OFF_GB + D_MODEL

kernel_name = "fox_pool_gated_streaming_encoder"


def rmsnorm(x, g):
    xf = x.astype(jnp.float32)
    y = xf * lax.rsqrt(jnp.mean(xf * xf, axis=-1, keepdims=True) + EPS)
    return (y * g.astype(jnp.float32)).astype(x.dtype)


def split_projection(h, w_in, b_f):
    B, T, _ = h.shape
    p = h @ w_in
    q = p[..., OFF_Q:OFF_K].reshape(B, T, N_HEADS, HEAD_DIM)
    k = p[..., OFF_K:OFF_V].reshape(B, T, N_HEADS, HEAD_DIM)
    v = p[..., OFF_V:OFF_F].reshape(B, T, N_HEADS, HEAD_DIM)
    logf = jax.nn.log_sigmoid(p[..., OFF_F:OFF_U].astype(jnp.float32) + b_f.astype(jnp.float32))
    u = p[..., OFF_U:OFF_GA]
    ga = p[..., OFF_GA:OFF_GB]
    gb = p[..., OFF_GB:IN_W]
    return q, k, v, logf, u, ga, gb


def forgetting_attend(q, k, v, cq, ck, qpos, kpos):
    s = jnp.einsum('bqhd,bkhd->bhqk', q, k).astype(jnp.float32) * SCALE
    bias = jnp.transpose(cq, (0, 2, 1))[..., :, None] - jnp.transpose(ck, (0, 2, 1))[..., None, :]
    mask = kpos[None, :] <= qpos[:, None]
    s = jnp.where(mask, s + bias, -jnp.inf)
    p = jax.nn.softmax(s, axis=-1)
    return jnp.einsum('bhqk,bkhd->bqhd', p.astype(v.dtype), v)


def fox_prompt(q, k, v, logf):
    B, T, H, dh = q.shape
    nb = T // Q_BLOCK
    c = jnp.cumsum(logf, axis=1)
    qb = jnp.transpose(q.reshape(B, nb, Q_BLOCK, H, dh), (1, 0, 2, 3, 4))
    cqb = jnp.transpose(c.reshape(B, nb, Q_BLOCK, H), (1, 0, 2, 3))
    posb = jnp.arange(T).reshape(nb, Q_BLOCK)
    kpos = jnp.arange(T)
    o = lax.map(lambda a: forgetting_attend(a[0], k, v, a[1], c, a[2], kpos), (qb, cqb, posb))
    return jnp.transpose(o, (1, 0, 2, 3, 4)).reshape(B, T, H * dh)


def fox_sample(q, k_all, v_all, logf_all, past):
    B, n, H, dh = q.shape
    c = jnp.cumsum(logf_all.astype(jnp.float32), axis=1)
    qpos = past + jnp.arange(n)
    kpos = jnp.arange(past + n)
    o = forgetting_attend(q, k_all, v_all, c[:, past:], c, qpos, kpos)
    return o.reshape(B, n, H * dh)


def multi_scale_pool(u_hist, u, pos):
    B, n, P = u.shape
    L = POOL_HIST
    ext = jnp.concatenate([u_hist.astype(u.dtype), u], axis=1).astype(jnp.float32)
    cs = jnp.concatenate([jnp.zeros((B, 1, P), jnp.float32), jnp.cumsum(ext, axis=1)], axis=1)
    outs = []
    for g, w in enumerate(POOL_WINDOWS):
        csg = cs[..., g * POOL_GW:(g + 1) * POOL_GW]
        tot = csg[:, L + 1:L + 1 + n] - csg[:, L + 1 - w:L + 1 - w + n]
        cnt = jnp.minimum(pos + 1, w).astype(jnp.float32)[None, :, None]
        outs.append(tot / cnt)
    pooled = jnp.concatenate(outs, axis=-1) - u.astype(jnp.float32)
    return pooled.astype(u.dtype), ext[:, -POOL_HIST:].astype(u.dtype)


def merge_and_ffn(x, a_out, pooled, ga, gb, w_attn_up, w_pool, pool_scale, w_out,
                  g_mix_post, g_ffn_pre, w_ff1, w_ff2, g_ffn_post):
    B, T, _ = x.shape
    br_a = a_out @ w_attn_up
    br_b = jnp.einsum('btgc,gcd->btgd', pooled.reshape(B, T, POOL_GROUPS, POOL_GW),
                      w_pool).reshape(B, T, D_MODEL) * pool_scale
    m = jax.nn.sigmoid(ga) * br_a + jax.nn.sigmoid(gb) * br_b
    x = x + rmsnorm(m @ w_out, g_mix_post)
    h = rmsnorm(x, g_ffn_pre)
    z = jnp.square(jax.nn.relu(h @ w_ff1))
    return x + rmsnorm(z @ w_ff2, g_ffn_post)


def setup_inputs(seed: int = 0) -> dict:
    key = jax.random.key(seed)
    ks = jax.random.split(key, 24)
    f32 = jnp.float32
    nrm = lambda k, s, sc: jax.random.normal(k, s, f32) * sc
    gain = lambda k: 1.0 + 0.1 * jax.random.normal(k, (DEPTH, D_MODEL), f32)
    return {
        "x_prompt": nrm(ks[0], (BATCH, SEQ, D_MODEL), 1.0),
        "x_sample": nrm(ks[1], (DEC_BATCH, DEC_SEQ, D_MODEL), 1.0),
        "cache_k": nrm(ks[2], (DEPTH, DEC_BATCH, PAST_LEN, N_HEADS, HEAD_DIM), 1.0),
        "cache_v": nrm(ks[3], (DEPTH, DEC_BATCH, PAST_LEN, N_HEADS, HEAD_DIM), 1.0),
        "cache_logf": jax.nn.log_sigmoid(3.0 + nrm(ks[4], (DEPTH, DEC_BATCH, PAST_LEN, N_HEADS), 1.0)),
        "state_pool": nrm(ks[5], (DEPTH, DEC_BATCH, POOL_HIST, POOL_W), 1.0),
        "g_mix_pre": gain(ks[6]),
        "w_in": nrm(ks[7], (DEPTH, D_MODEL, IN_W), D_MODEL ** -0.5),
        "b_f": 3.0 + nrm(ks[8], (DEPTH, N_HEADS), 0.1),
        "w_attn_up": nrm(ks[9], (DEPTH, ATTN_W, D_MODEL), ATTN_W ** -0.5),
        "w_pool": nrm(ks[10], (DEPTH, POOL_GROUPS, POOL_GW, POOL_OUT_GW), POOL_GW ** -0.5),
        "pool_scale": gain(ks[11]),
        "w_out": nrm(ks[12], (DEPTH, D_MODEL, D_MODEL), D_MODEL ** -0.5),
        "g_mix_post": gain(ks[13]),
        "g_ffn_pre": gain(ks[14]),
        "w_ff1": nrm(ks[15], (DEPTH, D_MODEL, D_FF), D_MODEL ** -0.5),
        "w_ff2": nrm(ks[16], (DEPTH, D_FF, D_MODEL), D_FF ** -0.5),
        "g_ffn_post": gain(ks[17]),
    }


def reference(x_prompt, x_sample, cache_k, cache_v, cache_logf, state_pool,
              g_mix_pre, w_in, b_f, w_attn_up, w_pool, pool_scale, w_out,
              g_mix_post, g_ffn_pre, w_ff1, w_ff2, g_ffn_post):
    xp, xs = x_prompt, x_sample
    Bp, Tp, _ = xp.shape
    Bs, Ts, _ = xs.shape
    kp_l, vp_l, fp_l, pp_l, ks_l, vs_l, fs_l, ps_l = [], [], [], [], [], [], [], []
    for l in range(DEPTH):
        h = rmsnorm(xp, g_mix_pre[l])
        q, k, v, logf, u, ga, gb = split_projection(h, w_in[l], b_f[l])
        a_out = fox_prompt(q, k, v, logf)
        pooled, pool_new = multi_scale_pool(jnp.zeros((Bp, POOL_HIST, POOL_W), u.dtype), u,
                                            jnp.arange(Tp))
        xp = merge_and_ffn(xp, a_out, pooled, ga, gb, w_attn_up[l], w_pool[l], pool_scale[l],
                           w_out[l], g_mix_post[l], g_ffn_pre[l], w_ff1[l], w_ff2[l], g_ffn_post[l])
        kp_l.append(k); vp_l.append(v); fp_l.append(logf); pp_l.append(pool_new)

        past = cache_k.shape[2]
        h = rmsnorm(xs, g_mix_pre[l])
        q, k, v, logf, u, ga, gb = split_projection(h, w_in[l], b_f[l])
        k_all = jnp.concatenate([cache_k[l].astype(k.dtype), k], axis=1)
        v_all = jnp.concatenate([cache_v[l].astype(v.dtype), v], axis=1)
        f_all = jnp.concatenate([cache_logf[l].astype(jnp.float32), logf], axis=1)
        a_out = fox_sample(q, k_all, v_all, f_all, past)
        pooled, pool_new = multi_scale_pool(state_pool[l], u, past + jnp.arange(Ts))
        xs = merge_and_ffn(xs, a_out, pooled, ga, gb, w_attn_up[l], w_pool[l], pool_scale[l],
                           w_out[l], g_mix_post[l], g_ffn_pre[l], w_ff1[l], w_ff2[l], g_ffn_post[l])
        ks_l.append(k); vs_l.append(v); fs_l.append(logf); ps_l.append(pool_new)

    k_prompt = jnp.stack(kp_l, 0)
    v_prompt = jnp.stack(vp_l, 0)
    logf_prompt = jnp.stack(fp_l, 0)
    pool_prompt = jnp.stack(pp_l, 0)
    k_sample = jnp.stack(ks_l, 0)
    v_sample = jnp.stack(vs_l, 0)
    logf_sample = jnp.stack(fs_l, 0)
    pool_sample = jnp.stack(ps_l, 0)
    return (xp, xs, k_prompt, v_prompt, logf_prompt, pool_prompt,
            k_sample, v_sample, logf_sample, pool_sample)
```

```python
import functools
import math

import jax
import jax.numpy as jnp
from jax import lax
from jax.experimental import pallas as pl
from jax.experimental.pallas import tpu as pltpu

N_HEADS = 16
HEAD_DIM = 64
ATTN_W = N_HEADS * HEAD_DIM
POOL_WINDOWS = (2, 4, 8, 16)
POOL_GROUPS = len(POOL_WINDOWS)
POOL_HIST = max(POOL_WINDOWS) - 1
EPS = 1e-6
SCALE = HEAD_DIM ** -0.5

LANES = 128
SUBLANES = 8
V7X_VMEM_BYTES = 64 * 1024 * 1024
VMEM_LIMIT_CAP = 60000 * 1024

HEADS_PER_BLOCK = LANES // HEAD_DIM
HIST_ROWS = 16

F32 = jnp.float32
BF16 = jnp.bfloat16


def _vmem_limit(nbytes):
    return int(min(VMEM_LIMIT_CAP, max(32 * 1024 * 1024, nbytes * 5 // 4)))


def _nbytes(shape, dtype):
    return math.prod(shape) * jnp.dtype(dtype).itemsize


def _rms(x):
    return x * lax.rsqrt(jnp.mean(x * x, axis=-1, keepdims=True) + EPS)


def _dot(a, b):
    return jnp.dot(a, b, preferred_element_type=F32)


def _dot_nt(a, b):
    return lax.dot_general(a, b, (((1,), (1,)), ((), ())), preferred_element_type=F32)


def _resident(shape):
    zeros = (0,) * len(shape)
    return pl.BlockSpec(shape, lambda *_: zeros, pipeline_mode=pl.Buffered(1))


def _inproj_kernel(x_ref, g_ref, w_ref, wf_ref, bf_ref,
                   h_ref, q_ref, k_ref, kb_ref, v_ref, vb_ref, u_ref, f_ref):
    h = (_rms(x_ref[...]) * g_ref[...]).astype(BF16)
    h_ref[...] = h
    a = q_ref.shape[1]
    q_ref[...] = (_dot(h, w_ref[:, 0:a]) * SCALE).astype(BF16)
    k = _dot(h, w_ref[:, a:2 * a])
    k_ref[...] = k
    kb_ref[...] = k.astype(BF16)
    v = _dot(h, w_ref[:, 2 * a:3 * a])
    v_ref[...] = v
    vb_ref[...] = v.astype(BF16)
    u_ref[...] = _dot(h, w_ref[:, 3 * a:])
    nh = f_ref.shape[1]
    f = _dot(h, wf_ref[...])[:, 0:nh] + bf_ref[...]
    f_ref[...] = jax.nn.log_sigmoid(f)


def _inproj(x, g, w, wf, bf, *, tm):
    n, d = x.shape
    a = ATTN_W
    p = w.shape[1] - 3 * a
    nh = bf.shape[1]
    row = lambda width: pl.BlockSpec((tm, width), lambda i: (i, 0))
    out_shape = (
        jax.ShapeDtypeStruct((n, d), BF16),
        jax.ShapeDtypeStruct((n, a), BF16),
        jax.ShapeDtypeStruct((n, a), F32),
        jax.ShapeDtypeStruct((n, a), BF16),
        jax.ShapeDtypeStruct((n, a), F32),
        jax.ShapeDtypeStruct((n, a), BF16),
        jax.ShapeDtypeStruct((n, p), F32),
        jax.ShapeDtypeStruct((n, nh), F32),
    )
    est = (2 * _nbytes((tm, d), F32) + _nbytes(w.shape, BF16) + _nbytes(wf.shape, BF16)
           + 2 * (_nbytes((tm, d), BF16) + 3 * _nbytes((tm, a), BF16) + 2 * _nbytes((tm, a), F32)
                  + _nbytes((tm, p), F32) + _nbytes((tm, LANES), F32))
           + 2 * _nbytes((tm, a), F32))
    return pl.pallas_call(
        _inproj_kernel,
        grid=(n // tm,),
        in_specs=[row(d), _resident((1, d)), _resident(w.shape), _resident(wf.shape), _resident((1, nh))],
        out_specs=(row(d), row(a), row(a), row(a), row(a), row(a), row(p), row(nh)),
        out_shape=out_shape,
        compiler_params=pltpu.CompilerParams(
            dimension_semantics=("arbitrary",), vmem_limit_bytes=_vmem_limit(est)),
        name="inproj",
    )(x, g, w, wf, bf)


def _cumsum_kernel(x_ref, o_ref):
    x = x_ref[...]
    length = x.shape[1]
    lane = lax.broadcasted_iota(jnp.int32, x.shape, 1)
    shift = 1
    while shift < length:
        x = x + jnp.where(lane >= shift, pltpu.roll(x, shift, axis=1), 0.0)
        shift *= 2
    o_ref[...] = x


def _cumsum_lanes(x):
    rows, length = x.shape
    spec = pl.BlockSpec((SUBLANES, length), lambda i: (i, 0))
    return pl.pallas_call(
        _cumsum_kernel,
        grid=(rows // SUBLANES,),
        in_specs=[spec],
        out_specs=spec,
        out_shape=jax.ShapeDtypeStruct(x.shape, F32),
        compiler_params=pltpu.CompilerParams(dimension_semantics=("arbitrary",)),
        name="cumsum",
    )(x)


def _attn_kernel(q_ref, k_ref, v_ref, cq_ref, ck_ref, o_ref, *, tq):
    i = pl.program_id(2)
    q2 = q_ref[0]
    lane = lax.broadcasted_iota(jnp.int32, q2.shape, 1)
    zero = jnp.zeros_like(q2)
    qs = [jnp.where((lane // HEAD_DIM) == a, q2, zero) for a in range(HEADS_PER_BLOCK)]
    cq = cq_ref[0, 0]
    cqs = [cq[:, a:a + 1] for a in range(HEADS_PER_BLOCK)]

    def tile(j, carry, masked):
        start = pl.multiple_of(j * tq, tq)
        kt = k_ref[0, pl.ds(start, tq), :]
        vt = v_ref[0, pl.ds(start, tq), :]
        ck = ck_ref[0, 0, :, pl.ds(start, tq)]
        out = []
        for a in range(HEADS_PER_BLOCK):
            m_prev, l_prev, acc_prev = carry[a]
            s = _dot_nt(qs[a], kt) + (cqs[a] - ck[a:a + 1, :])
            if masked:
                r = lax.broadcasted_iota(jnp.int32, s.shape, 0)
                c = lax.broadcasted_iota(jnp.int32, s.shape, 1)
                s = jnp.where(c <= r, s, -jnp.inf)
            m_new = jnp.maximum(m_prev, jnp.max(s, axis=1, keepdims=True))
            alpha = jnp.exp(m_prev - m_new)
            p = jnp.exp(s - m_new)
            l_new = alpha * l_prev + jnp.sum(p, axis=1, keepdims=True)
            acc_new = alpha * acc_prev + _dot(p.astype(BF16), vt)
            out.append((m_new, l_new, acc_new))
        return tuple(out)

    init = tuple((jnp.full((tq, 1), -jnp.inf, F32), jnp.zeros((tq, 1), F32), jnp.zeros((tq, LANES), F32))
                 for _ in range(HEADS_PER_BLOCK))
    carry = lax.fori_loop(0, i, functools.partial(tile, masked=False), init)
    carry = tile(i, carry, True)
    o = jnp.zeros((tq, LANES), F32)
    for a in range(HEADS_PER_BLOCK):
        _, l_a, acc_a = carry[a]
        o = jnp.where((lane // HEAD_DIM) == a, acc_a / l_a, o)
    o_ref[0] = o.astype(o_ref.dtype)


def _attn_prompt(q, kb, vb, cq, ck, *, tq):
    b, t, a = q.shape
    nblk = a // LANES
    qspec = pl.BlockSpec((1, tq, LANES), lambda bi, hi, i: (bi, i, hi))
    kvspec = pl.BlockSpec((1, t, LANES), lambda bi, hi, i: (bi, 0, hi))
    return pl.pallas_call(
        functools.partial(_attn_kernel, tq=tq),
        grid=(b, nblk, t // tq),
        in_specs=[qspec, kvspec, kvspec,
                  pl.BlockSpec((1, 1, tq, HEADS_PER_BLOCK), lambda bi, hi, i: (bi, hi, i, 0)),
                  pl.BlockSpec((1, 1, HEADS_PER_BLOCK, t), lambda bi, hi, i: (bi, hi, 0, 0))],
        out_specs=qspec,
        out_shape=jax.ShapeDtypeStruct((b, t, a), BF16),
        compiler_params=pltpu.CompilerParams(
            dimension_semantics=("arbitrary", "arbitrary", "arbitrary")),
        name="attn_prompt",
    )(q, kb, vb, cq, ck)


def _attn_sample_kernel(q_ref, kc_ref, vc_ref, kn_ref, vn_ref, cc_ref, cnr_ref, cnc_ref, o_ref,
                        kb_ref, vb_ref, s_ref, p_ref, pn_ref):
    n, a = q_ref.shape[1], q_ref.shape[2]
    nh = a // HEAD_DIM
    q = q_ref[0]
    qt = jnp.concatenate([q] * nh, axis=0)
    row_h = lax.broadcasted_iota(jnp.int32, qt.shape, 0) // n
    col_h = lax.broadcasted_iota(jnp.int32, qt.shape, 1) // HEAD_DIM
    qbd = jnp.where(row_h == col_h, qt, jnp.zeros_like(qt))
    kb_ref[...] = kc_ref[0].astype(BF16)
    vb_ref[...] = vc_ref[0].astype(BF16)
    s_ref[...] = _dot_nt(qbd, kb_ref[...])
    s_new = _dot_nt(qbd, kn_ref[0])
    cnc = cnc_ref[0]
    r = lax.broadcasted_iota(jnp.int32, (n, n), 0)
    c = lax.broadcasted_iota(jnp.int32, (n, n), 1)
    inv_l = []
    for h in range(nh):
        rows = slice(h * n, (h + 1) * n)
        cq = cnc[:, h:h + 1]
        sc = s_ref[rows, :] + (cq - cc_ref[0, h:h + 1, :])
        sn = s_new[rows, :] + (cq - cnr_ref[0, h:h + 1, :])
        sn = jnp.where(c <= r, sn, -jnp.inf)
        m = jnp.maximum(jnp.max(sc, axis=1, keepdims=True), jnp.max(sn, axis=1, keepdims=True))
        pc = jnp.exp(sc - m)
        pn = jnp.exp(sn - m)
        inv_l.append(1.0 / (jnp.sum(pc, axis=1, keepdims=True) + jnp.sum(pn, axis=1, keepdims=True)))
        p_ref[rows, :] = pc.astype(BF16)
        pn_ref[rows, :] = pn.astype(BF16)
    o = _dot(p_ref[...], vb_ref[...]) + _dot(pn_ref[...], vn_ref[0])
    for h in range(nh):
        rows = slice(h * n, (h + 1) * n)
        cols = slice(h * HEAD_DIM, (h + 1) * HEAD_DIM)
        o_ref[0, :, cols] = (o[rows, cols] * inv_l[h]).astype(o_ref.dtype)


def _attn_sample(q, kc, vc, kn, vn, cc, cnr, cnc):
    b, n, a = q.shape
    past = kc.shape[1]
    nh = a // HEAD_DIM
    per_b = lambda *tail: pl.BlockSpec((1,) + tail, lambda bi: (bi,) + (0,) * len(tail))
    est = (4 * _nbytes((past, a), F32) + 2 * _nbytes((past, a), BF16)
           + _nbytes((nh * n, past), F32) + _nbytes((nh * n, past), BF16) + 4 * _nbytes((nh * n, a), F32))
    return pl.pallas_call(
        _attn_sample_kernel,
        grid=(b,),
        in_specs=[per_b(n, a), per_b(past, a), per_b(past, a), per_b(n, a), per_b(n, a),
                  per_b(nh, past), per_b(nh, n), per_b(n, nh)],
        out_specs=per_b(n, a),
        out_shape=jax.ShapeDtypeStruct((b, n, a), BF16),
        scratch_shapes=[pltpu.VMEM((past, a), BF16), pltpu.VMEM((past, a), BF16),
                        pltpu.VMEM((nh * n, past), F32), pltpu.VMEM((nh * n, past), BF16),
                        pltpu.VMEM((nh * n, n), BF16)],
        compiler_params=pltpu.CompilerParams(
            dimension_semantics=("arbitrary",), vmem_limit_bytes=_vmem_limit(est)),
        name="attn_sample",
    )(q, kc, vc, kn, vn, cc, cnr, cnc)


def _merge_kernel(a_ref, h_ref, u_ref, hist_ref, x_ref, wup_ref, wga_ref, wgb_ref, wpool_ref, ps_ref,
                  wout_ref, gpost_ref, gpre_ref, x1_ref, h2_ref, ext_ref, m_ref,
                  *, pos0, zero_first):
    nseg, tl, _ = u_ref.shape
    i = pl.program_id(1)
    hist = hist_ref[...]
    if zero_first:
        hist = jnp.where(i == 0, 0.0, hist)
    ext_ref[:, 0:HIST_ROWS, :] = hist
    ext_ref[:, HIST_ROWS:HIST_ROWS + tl, :] = u_ref[...]
    gw = u_ref.shape[2] // POOL_GROUPS
    ogw = wpool_ref.shape[2]
    pos = pos0 + i * tl + lax.broadcasted_iota(jnp.int32, (nseg, tl, gw), 1)
    a = a_ref[...]
    h = h_ref[...]
    for g, w in enumerate(POOL_WINDOWS):
        cs = slice(g * gw, (g + 1) * gw)
        cur = ext_ref[:, HIST_ROWS:HIST_ROWS + tl, cs]
        tot = cur
        for s in range(1, w):
            tot = tot + ext_ref[:, HIST_ROWS - s:HIST_ROWS - s + tl, cs]
        cnt = jnp.minimum(pos + 1, w).astype(F32)
        pooled = (tot / cnt - cur).reshape(nseg * tl, gw).astype(BF16)
        os_ = slice(g * ogw, (g + 1) * ogw)
        br_b = _dot(pooled, wpool_ref[g]) * ps_ref[:, os_]
        br_a = _dot(a, wup_ref[:, os_])
        ga = _dot(h, wga_ref[:, os_])
        gb = _dot(h, wgb_ref[:, os_])
        m_ref[:, os_] = (jax.nn.sigmoid(ga) * br_a + jax.nn.sigmoid(gb) * br_b).astype(BF16)
    x1 = x_ref[...] + _rms(_dot(m_ref[...], wout_ref[...])) * gpost_ref[...]
    x1_ref[...] = x1
    h2_ref[...] = (_rms(x1) * gpre_ref[...]).astype(BF16)


def _merge(a, h, u3, hist, hist_map, x, wup, wga, wgb, wpool, ps, wout, gpost, gpre,
           *, nseg, tl, pos0, zero_first):
    s_total, l_total, p = u3.shape
    n, d = x.shape
    tm = nseg * tl
    steps = l_total // tl
    assert nseg == 1 or steps == 1
    row = lambda width: pl.BlockSpec((tm, width), lambda s, i: (s * steps + i, 0))
    est = (_nbytes(wup.shape, BF16) + 2 * _nbytes(wga.shape, BF16) + _nbytes(wpool.shape, BF16)
           + _nbytes(wout.shape, BF16)
           + 2 * (_nbytes((tm, a.shape[1]), BF16) + 2 * _nbytes((tm, d), BF16) + 2 * _nbytes((tm, d), F32)
                  + _nbytes((tm + HIST_ROWS, p), F32))
           + _nbytes((tm + nseg * HIST_ROWS, p), F32) + _nbytes((tm, d), BF16) + 6 * _nbytes((tm, d), F32))
    return pl.pallas_call(
        functools.partial(_merge_kernel, pos0=pos0, zero_first=zero_first),
        grid=(s_total // nseg, steps),
        in_specs=[row(a.shape[1]), row(d),
                  pl.BlockSpec((nseg, tl, p), lambda s, i: (s, i, 0)),
                  pl.BlockSpec((nseg, HIST_ROWS, p), hist_map),
                  row(d),
                  _resident(wup.shape), _resident(wga.shape), _resident(wgb.shape), _resident(wpool.shape),
                  _resident((1, d)), _resident(wout.shape), _resident((1, d)), _resident((1, d))],
        out_specs=(row(d), row(d)),
        out_shape=(jax.ShapeDtypeStruct((n, d), F32), jax.ShapeDtypeStruct((n, d), BF16)),
        scratch_shapes=[pltpu.VMEM((nseg, HIST_ROWS + tl, p), F32), pltpu.VMEM((tm, d), BF16)],
        compiler_params=pltpu.CompilerParams(
            dimension_semantics=("arbitrary", "arbitrary"), vmem_limit_bytes=_vmem_limit(est)),
        name="merge",
    )(a, h, u3, hist, x, wup, wga, wgb, wpool, ps, wout, gpost, gpre)


def _ffn_kernel(h_ref, x1_ref, w1_ref, w2_ref, g_ref, o_ref):
    f = pl.program_id(1)
    z = jnp.square(jnp.maximum(_dot(h_ref[...], w1_ref[...]), 0.0)).astype(BF16)
    part = _dot(z, w2_ref[...])

    @pl.when(f == 0)
    def _():
        o_ref[...] = part

    @pl.when(f > 0)
    def _():
        o_ref[...] += part

    @pl.when(f == pl.num_programs(1) - 1)
    def _():
        o_ref[...] = x1_ref[...] + _rms(o_ref[...]) * g_ref[...]


def _ffn(h2, x1, w1, w2, g, *, tm, tf):
    n, d = x1.shape
    dff = w1.shape[1]
    est = (2 * _nbytes((tm, d), BF16) + 4 * _nbytes((tm, d), F32)
           + 4 * _nbytes((d, tf), BF16) + 2 * _nbytes((tm, tf), F32) + _nbytes((tm, d), F32))
    return pl.pallas_call(
        _ffn_kernel,
        grid=(n // tm, dff // tf),
        in_specs=[pl.BlockSpec((tm, d), lambda i, f: (i, 0)),
                  pl.BlockSpec((tm, d), lambda i, f: (i, 0)),
                  pl.BlockSpec((d, tf), lambda i, f: (0, f)),
                  pl.BlockSpec((tf, d), lambda i, f: (f, 0)),
                  pl.BlockSpec((1, d), lambda i, f: (0, 0))],
        out_specs=pl.BlockSpec((tm, d), lambda i, f: (i, 0)),
        out_shape=jax.ShapeDtypeStruct((n, d), F32),
        compiler_params=pltpu.CompilerParams(
            dimension_semantics=("arbitrary", "arbitrary"), vmem_limit_bytes=_vmem_limit(est)),
        name="ffn",
    )(h2, x1, w1, w2, g)


def _tile(n, pref):
    t = min(n, pref)
    while n % t:
        t //= 2
    return t


INPROJ_TM = 256
ATTN_TQ = 256
MERGE_TM = 256
FFN_TM = 512
FFN_TF = 512


def _cum_logf_rows(logf_bth):
    b, t, h = logf_bth.shape
    rows = jnp.transpose(logf_bth, (0, 2, 1)).reshape(b * h, t)
    pad = (-t) % LANES
    if pad:
        rows = jnp.pad(rows, ((0, 0), (0, pad)))
    return _cumsum_lanes(rows)


def _layer(xp, xs, cache_k, cache_v, cache_logf, state_pool, g_mix_pre, w_in, b_f, w_attn_up, w_pool,
           pool_scale, w_out, g_mix_post, g_ffn_pre, w_ff1, w_ff2, g_ffn_post):
    bp, tp, d = xp.shape
    bs, ts, _ = xs.shape
    past = cache_k.shape[1]
    a, nh, p = ATTN_W, N_HEADS, w_pool.shape[0] * w_pool.shape[1]
    off_f = 3 * a
    off_u = off_f + nh
    off_ga = off_u + p
    off_gb = off_ga + d

    w_qkvu = jnp.concatenate([w_in[:, :off_f], w_in[:, off_u:off_ga]], axis=1).astype(BF16)
    w_f = jnp.pad(w_in[:, off_f:off_u], ((0, 0), (0, LANES - nh))).astype(BF16)
    w_ga = w_in[:, off_ga:off_gb].astype(BF16)
    w_gb = w_in[:, off_gb:].astype(BF16)
    w_up = w_attn_up.astype(BF16)
    w_pl = w_pool.astype(BF16)
    w_o = w_out.astype(BF16)
    w1 = w_ff1.astype(BF16)
    w2 = w_ff2.astype(BF16)
    row = lambda v: v.reshape(1, -1)
    g_pre, b_f2, ps = row(g_mix_pre), row(b_f), row(pool_scale)
    g_post, g_fpre, g_fpost = row(g_mix_post), row(g_ffn_pre), row(g_ffn_post)

    def project(x3):
        x2 = x3.reshape(-1, d)
        return x2, _inproj(x2, g_pre, w_qkvu, w_f, b_f2, tm=_tile(x2.shape[0], INPROJ_TM))

    def finish(x2, h, a_out, u3, hist, hist_map, nseg, tl, pos0, zero_first):
        x1, h2 = _merge(a_out, h, u3, hist, hist_map, x2, w_up, w_ga, w_gb, w_pl, ps, w_o, g_post, g_fpre,
                        nseg=nseg, tl=tl, pos0=pos0, zero_first=zero_first)
        n = x2.shape[0]
        return _ffn(h2, x1, w1, w2, g_fpost, tm=_tile(n, FFN_TM), tf=_tile(w1.shape[1], FFN_TF))

    x2, (h, q, k, kb, v, vb, u, logf) = project(xp)
    c_rows = _cum_logf_rows(logf.reshape(bp, tp, nh))
    nblk = nh // HEADS_PER_BLOCK
    ck = c_rows.reshape(bp, nblk, HEADS_PER_BLOCK, tp)
    cq = jnp.transpose(ck, (0, 1, 3, 2))
    three = lambda z: z.reshape(bp, tp, a)
    a_out = _attn_prompt(three(q), three(kb), three(vb), cq, ck, tq=_tile(tp, ATTN_TQ))
    u3 = u.reshape(bp, tp, p)
    tl = _tile(tp, MERGE_TM)
    blocks_per_tile = tl // HIST_ROWS
    hist_map = lambda s, i: (s, jnp.maximum(i * blocks_per_tile - 1, 0), 0)
    yp = finish(x2, h, a_out.reshape(bp * tp, a), u3, u3, hist_map, 1, tl, 0, True).reshape(bp, tp, d)
    prompt_out = (yp, k.reshape(bp, tp, nh, HEAD_DIM), v.reshape(bp, tp, nh, HEAD_DIM),
                  logf.reshape(bp, tp, nh),
                  jnp.concatenate([jnp.zeros((bp, POOL_HIST, p), F32), u3], axis=1)[:, -POOL_HIST:])

    x2, (h, q, k, kb, v, vb, u, logf) = project(xs)
    logf3 = logf.reshape(bs, ts, nh)
    f_all = jnp.concatenate([cache_logf.astype(F32), logf3], axis=1)
    c_all = _cum_logf_rows(f_all).reshape(bs, nh, -1)
    c_cache = c_all[:, :, :past]
    c_new = c_all[:, :, past:past + ts]
    three = lambda z: z.reshape(bs, ts, a)
    a_out = _attn_sample(three(q), cache_k.reshape(bs, past, a), cache_v.reshape(bs, past, a),
                         three(kb), three(vb), c_cache, c_new, jnp.transpose(c_new, (0, 2, 1)))
    u3 = u.reshape(bs, ts, p)
    hist = jnp.pad(state_pool.astype(F32), ((0, 0), (HIST_ROWS - POOL_HIST, 0), (0, 0)))
    ys = finish(x2, h, a_out.reshape(bs * ts, a), u3, hist, lambda s, i: (s, 0, 0), bs, ts, past,
                False).reshape(bs, ts, d)
    sample_out = (ys, k.reshape(bs, ts, nh, HEAD_DIM), v.reshape(bs, ts, nh, HEAD_DIM), logf3,
                  jnp.concatenate([state_pool.astype(F32), u3], axis=1)[:, -POOL_HIST:])
    return prompt_out, sample_out


def kernel(x_prompt, x_sample, cache_k, cache_v, cache_logf, state_pool, g_mix_pre, w_in, b_f, w_attn_up,
           w_pool, pool_scale, w_out, g_mix_post, g_ffn_pre, w_ff1, w_ff2, g_ffn_post):
    depth = w_in.shape[0]
    xp, xs = x_prompt, x_sample
    per_layer = []
    for l in range(depth):
        po, so = _layer(xp, xs, cache_k[l], cache_v[l], cache_logf[l], state_pool[l], g_mix_pre[l], w_in[l],
                        b_f[l], w_attn_up[l], w_pool[l], pool_scale[l], w_out[l], g_mix_post[l],
                        g_ffn_pre[l], w_ff1[l], w_ff2[l], g_ffn_post[l])
        xp, xs = po[0], so[0]
        per_layer.append(po[1:] + so[1:])
    stacked = [jnp.stack(leaves, 0) for leaves in zip(*per_layer)]
    return (xp, xs, *stacked)
```

```python
import functools
import math

import jax
import jax.numpy as jnp
from jax import lax
from jax.experimental import pallas as pl
from jax.experimental.pallas import tpu as pltpu

N_HEADS = 16
HEAD_DIM = 64
ATTN_W = N_HEADS * HEAD_DIM
POOL_WINDOWS = (2, 4, 8, 16)
POOL_GROUPS = len(POOL_WINDOWS)
POOL_HIST = max(POOL_WINDOWS) - 1
EPS = 1e-6
SCALE = HEAD_DIM ** -0.5

LANES = 128
SUBLANES = 8
V7X_VMEM_BYTES = 64 * 1024 * 1024
VMEM_LIMIT_CAP = 60000 * 1024

HEADS_PER_BLOCK = LANES // HEAD_DIM
HIST_ROWS = 16

F32 = jnp.float32
BF16 = jnp.bfloat16


def _vmem_limit(nbytes):
    return int(min(VMEM_LIMIT_CAP, max(32 * 1024 * 1024, nbytes * 5 // 4)))


def _nbytes(shape, dtype):
    return math.prod(shape) * jnp.dtype(dtype).itemsize


def _rms(x):
    return x * lax.rsqrt(jnp.mean(x * x, axis=-1, keepdims=True) + EPS)


def _dot(a, b):
    return jnp.dot(a, b, preferred_element_type=F32)


def _dot_nt(a, b):
    return lax.dot_general(a, b, (((1,), (1,)), ((), ())), preferred_element_type=F32)


def _resident(shape):
    zeros = (0,) * len(shape)
    return pl.BlockSpec(shape, lambda *_: zeros, pipeline_mode=pl.Buffered(1))


def _inproj_kernel(x_ref, g_ref, w_ref, wf_ref, bf_ref,
                   h_ref, q_ref, k_ref, kb_ref, v_ref, vb_ref, u_ref, f_ref):
    h = (_rms(x_ref[...]) * g_ref[...]).astype(BF16)
    h_ref[...] = h
    a = q_ref.shape[1]
    q_ref[...] = (_dot(h, w_ref[:, 0:a]) * SCALE).astype(BF16)
    k = _dot(h, w_ref[:, a:2 * a])
    k_ref[...] = k
    kb_ref[...] = k.astype(BF16)
    v = _dot(h, w_ref[:, 2 * a:3 * a])
    v_ref[...] = v
    vb_ref[...] = v.astype(BF16)
    u_ref[...] = _dot(h, w_ref[:, 3 * a:])
    nh = f_ref.shape[1]
    f = _dot(h, wf_ref[...])[:, 0:nh] + bf_ref[...]
    f_ref[...] = jax.nn.log_sigmoid(f)


def _inproj(x, g, w, wf, bf, *, tm):
    n, d = x.shape
    a = ATTN_W
    p = w.shape[1] - 3 * a
    nh = bf.shape[1]
    row = lambda width: pl.BlockSpec((tm, width), lambda i: (i, 0))
    out_shape = (
        jax.ShapeDtypeStruct((n, d), BF16),
        jax.ShapeDtypeStruct((n, a), BF16),
        jax.ShapeDtypeStruct((n, a), F32),
        jax.ShapeDtypeStruct((n, a), BF16),
        jax.ShapeDtypeStruct((n, a), F32),
        jax.ShapeDtypeStruct((n, a), BF16),
        jax.ShapeDtypeStruct((n, p), F32),
        jax.ShapeDtypeStruct((n, nh), F32),
    )
    est = (2 * _nbytes((tm, d), F32) + _nbytes(w.shape, BF16) + _nbytes(wf.shape, BF16)
           + 2 * (_nbytes((tm, d), BF16) + 3 * _nbytes((tm, a), BF16) + 2 * _nbytes((tm, a), F32)
                  + _nbytes((tm, p), F32) + _nbytes((tm, LANES), F32))
           + 2 * _nbytes((tm, a), F32))
    return pl.pallas_call(
        _inproj_kernel,
        grid=(n // tm,),
        in_specs=[row(d), _resident((1, d)), _resident(w.shape), _resident(wf.shape), _resident((1, nh))],
        out_specs=(row(d), row(a), row(a), row(a), row(a), row(a), row(p), row(nh)),
        out_shape=out_shape,
        compiler_params=pltpu.CompilerParams(
            dimension_semantics=("arbitrary",), vmem_limit_bytes=_vmem_limit(est)),
        name="inproj",
    )(x, g, w, wf, bf)


def _cumsum_kernel(x_ref, o_ref):
    x = x_ref[...]
    length = x.shape[1]
    lane = lax.broadcasted_iota(jnp.int32, x.shape, 1)
    shift = 1
    while shift < length:
        x = x + jnp.where(lane >= shift, pltpu.roll(x, shift, axis=1), 0.0)
        shift *= 2
    o_ref[...] = x


def _cumsum_lanes(x):
    rows, length = x.shape
    spec = pl.BlockSpec((SUBLANES, length), lambda i: (i, 0))
    return pl.pallas_call(
        _cumsum_kernel,
        grid=(rows // SUBLANES,),
        in_specs=[spec],
        out_specs=spec,
        out_shape=jax.ShapeDtypeStruct(x.shape, F32),
        compiler_params=pltpu.CompilerParams(dimension_semantics=("arbitrary",)),
        name="cumsum",
    )(x)


VALUE_ROWS = HEAD_DIM + 16


def _attn_kernel(q_ref, k_ref, vf_ref, cq_ref, c_ref, o_ref, vta_ref, ckrep_ref, m_ref, acc_ref, *, tq):
    i = pl.program_id(2)
    t = k_ref.shape[1]

    @pl.when(i == 0)
    def _():
        vt = vf_ref[0].T
        for a in range(HEADS_PER_BLOCK):
            vta_ref[a, 0:HEAD_DIM, :] = vt[a * HEAD_DIM:(a + 1) * HEAD_DIM, :].astype(BF16)
            vta_ref[a, HEAD_DIM:VALUE_ROWS, :] = jnp.ones((VALUE_ROWS - HEAD_DIM, t), BF16)
            ckrep_ref[a] = jnp.broadcast_to(c_ref[0, 0, a:a + 1, :], (LANES, t)).T

    q2 = q_ref[0]
    lane = lax.broadcasted_iota(jnp.int32, q2.shape, 1)
    zero = jnp.zeros_like(q2)
    qs = [jnp.where((lane // HEAD_DIM) == a, q2, zero) for a in range(HEADS_PER_BLOCK)]
    cq = cq_ref[0, 0]
    m_ref[...] = jnp.full(m_ref.shape, -jnp.inf, F32)
    acc_ref[...] = jnp.zeros(acc_ref.shape, F32)

    def tile(j, masked):
        start = pl.multiple_of(j * tq, tq)
        kt = k_ref[0, pl.ds(start, tq), :]
        for a in range(HEADS_PER_BLOCK):
            ckr = ckrep_ref[a, pl.ds(start, tq), :]
            s = _dot_nt(kt, qs[a]) - jnp.concatenate([ckr] * (tq // LANES), axis=1)
            if masked:
                r = lax.broadcasted_iota(jnp.int32, s.shape, 0)
                c = lax.broadcasted_iota(jnp.int32, s.shape, 1)
                s = jnp.where(r <= c, s, -jnp.inf)
            cqa = cq[a:a + 1, :]
            m_prev = m_ref[a]
            m_new = jnp.maximum(m_prev, jnp.max(s, axis=0, keepdims=True) + cqa)
            alpha = jnp.exp(m_prev - m_new)
            p = jnp.exp(s - (m_new - cqa)).astype(BF16)
            pv = _dot(vta_ref[a, :, pl.ds(start, tq)], p)
            acc_ref[a] = alpha * acc_ref[a] + pv
            m_ref[a] = m_new

    def body(j, _):
        tile(j, False)
        return 0

    lax.fori_loop(0, i, body, 0)
    tile(i, True)
    ot = jnp.concatenate(
        [acc_ref[a, 0:HEAD_DIM, :] / acc_ref[a, HEAD_DIM:HEAD_DIM + 1, :] for a in range(HEADS_PER_BLOCK)],
        axis=0)
    o_ref[0] = ot.T.astype(o_ref.dtype)


def _attn_prompt(q, kb, vf, c_rows, *, tq):
    b, t, a = q.shape
    nblk = a // LANES
    qspec = pl.BlockSpec((1, tq, LANES), lambda bi, hi, i: (bi, i, hi))
    kvspec = pl.BlockSpec((1, t, LANES), lambda bi, hi, i: (bi, 0, hi))
    return pl.pallas_call(
        functools.partial(_attn_kernel, tq=tq),
        grid=(b, nblk, t // tq),
        in_specs=[qspec, kvspec, kvspec,
                  pl.BlockSpec((1, 1, HEADS_PER_BLOCK, tq), lambda bi, hi, i: (bi, hi, 0, i)),
                  pl.BlockSpec((1, 1, HEADS_PER_BLOCK, t), lambda bi, hi, i: (bi, hi, 0, 0))],
        out_specs=qspec,
        out_shape=jax.ShapeDtypeStruct((b, t, a), BF16),
        scratch_shapes=[pltpu.VMEM((HEADS_PER_BLOCK, VALUE_ROWS, t), BF16),
                        pltpu.VMEM((HEADS_PER_BLOCK, t, LANES), F32),
                        pltpu.VMEM((HEADS_PER_BLOCK, 1, tq), F32),
                        pltpu.VMEM((HEADS_PER_BLOCK, VALUE_ROWS, tq), F32)],
        compiler_params=pltpu.CompilerParams(
            dimension_semantics=("arbitrary", "arbitrary", "arbitrary")),
        name="attn_prompt",
    )(q, kb, vf, c_rows, c_rows)


def _attn_sample_kernel(q_ref, kc_ref, vc_ref, kn_ref, vn_ref, cc_ref, cnr_ref, cnc_ref, o_ref,
                        kb_ref, vb_ref, s_ref, p_ref, pn_ref):
    n, a = q_ref.shape[1], q_ref.shape[2]
    nh = a // HEAD_DIM
    q = q_ref[0]
    qt = jnp.concatenate([q] * nh, axis=0)
    row_h = lax.broadcasted_iota(jnp.int32, qt.shape, 0) // n
    col_h = lax.broadcasted_iota(jnp.int32, qt.shape, 1) // HEAD_DIM
    qbd = jnp.where(row_h == col_h, qt, jnp.zeros_like(qt))
    kb_ref[...] = kc_ref[0].astype(BF16)
    vb_ref[...] = vc_ref[0].astype(BF16)
    s_ref[...] = _dot_nt(qbd, kb_ref[...])
    s_new = _dot_nt(qbd, kn_ref[0])
    cnc = cnc_ref[0]
    r = lax.broadcasted_iota(jnp.int32, (n, n), 0)
    c = lax.broadcasted_iota(jnp.int32, (n, n), 1)
    inv_l = []
    for h in range(nh):
        rows = slice(h * n, (h + 1) * n)
        cq = cnc[:, h:h + 1]
        sc = s_ref[rows, :] + (cq - cc_ref[0, h:h + 1, :])
        sn = s_new[rows, :] + (cq - cnr_ref[0, h:h + 1, :])
        sn = jnp.where(c <= r, sn, -jnp.inf)
        m = jnp.maximum(jnp.max(sc, axis=1, keepdims=True), jnp.max(sn, axis=1, keepdims=True))
        pc = jnp.exp(sc - m)
        pn = jnp.exp(sn - m)
        inv_l.append(1.0 / (jnp.sum(pc, axis=1, keepdims=True) + jnp.sum(pn, axis=1, keepdims=True)))
        p_ref[rows, :] = pc.astype(BF16)
        pn_ref[rows, :] = pn.astype(BF16)
    o = _dot(p_ref[...], vb_ref[...]) + _dot(pn_ref[...], vn_ref[0])
    for h in range(nh):
        rows = slice(h * n, (h + 1) * n)
        cols = slice(h * HEAD_DIM, (h + 1) * HEAD_DIM)
        o_ref[0, :, cols] = (o[rows, cols] * inv_l[h]).astype(o_ref.dtype)


def _attn_sample(q, kc, vc, kn, vn, cc, cnr, cnc):
    b, n, a = q.shape
    past = kc.shape[1]
    nh = a // HEAD_DIM
    per_b = lambda *tail: pl.BlockSpec((1,) + tail, lambda bi: (bi,) + (0,) * len(tail))
    est = (4 * _nbytes((past, a), F32) + 2 * _nbytes((past, a), BF16)
           + _nbytes((nh * n, past), F32) + _nbytes((nh * n, past), BF16) + 4 * _nbytes((nh * n, a), F32))
    return pl.pallas_call(
        _attn_sample_kernel,
        grid=(b,),
        in_specs=[per_b(n, a), per_b(past, a), per_b(past, a), per_b(n, a), per_b(n, a),
                  per_b(nh, past), per_b(nh, n), per_b(n, nh)],
        out_specs=per_b(n, a),
        out_shape=jax.ShapeDtypeStruct((b, n, a), BF16),
        scratch_shapes=[pltpu.VMEM((past, a), BF16), pltpu.VMEM((past, a), BF16),
                        pltpu.VMEM((nh * n, past), F32), pltpu.VMEM((nh * n, past), BF16),
                        pltpu.VMEM((nh * n, n), BF16)],
        compiler_params=pltpu.CompilerParams(
            dimension_semantics=("arbitrary",), vmem_limit_bytes=_vmem_limit(est)),
        name="attn_sample",
    )(q, kc, vc, kn, vn, cc, cnr, cnc)


def _merge_kernel(a_ref, h_ref, u_ref, hist_ref, x_ref, wup_ref, wga_ref, wgb_ref, wpool_ref, ps_ref,
                  wout_ref, gpost_ref, gpre_ref, x1_ref, h2_ref, ext_ref, m_ref,
                  *, pos0, zero_first):
    nseg, tl, _ = u_ref.shape
    i = pl.program_id(1)
    hist = hist_ref[...]
    if zero_first:
        hist = jnp.where(i == 0, 0.0, hist)
    ext_ref[:, 0:HIST_ROWS, :] = hist
    ext_ref[:, HIST_ROWS:HIST_ROWS + tl, :] = u_ref[...]
    gw = u_ref.shape[2] // POOL_GROUPS
    ogw = wpool_ref.shape[2]
    pos = pos0 + i * tl + lax.broadcasted_iota(jnp.int32, (nseg, tl, gw), 1)
    a = a_ref[...]
    h = h_ref[...]
    for g, w in enumerate(POOL_WINDOWS):
        cs = slice(g * gw, (g + 1) * gw)
        cur = ext_ref[:, HIST_ROWS:HIST_ROWS + tl, cs]
        tot = cur
        for s in range(1, w):
            tot = tot + ext_ref[:, HIST_ROWS - s:HIST_ROWS - s + tl, cs]
        cnt = jnp.minimum(pos + 1, w).astype(F32)
        pooled = (tot / cnt - cur).reshape(nseg * tl, gw).astype(BF16)
        os_ = slice(g * ogw, (g + 1) * ogw)
        br_b = _dot(pooled, wpool_ref[g]) * ps_ref[:, os_]
        br_a = _dot(a, wup_ref[:, os_])
        ga = _dot(h, wga_ref[:, os_])
        gb = _dot(h, wgb_ref[:, os_])
        m_ref[:, os_] = (jax.nn.sigmoid(ga) * br_a + jax.nn.sigmoid(gb) * br_b).astype(BF16)
    x1 = x_ref[...] + _rms(_dot(m_ref[...], wout_ref[...])) * gpost_ref[...]
    x1_ref[...] = x1
    h2_ref[...] = (_rms(x1) * gpre_ref[...]).astype(BF16)


def _merge(a, h, u3, hist, hist_map, x, wup, wga, wgb, wpool, ps, wout, gpost, gpre,
           *, nseg, tl, pos0, zero_first):
    s_total, l_total, p = u3.shape
    n, d = x.shape
    tm = nseg * tl
    steps = l_total // tl
    assert nseg == 1 or steps == 1
    row = lambda width: pl.BlockSpec((tm, width), lambda s, i: (s * steps + i, 0))
    est = (_nbytes(wup.shape, BF16) + 2 * _nbytes(wga.shape, BF16) + _nbytes(wpool.shape, BF16)
           + _nbytes(wout.shape, BF16)
           + 2 * (_nbytes((tm, a.shape[1]), BF16) + 2 * _nbytes((tm, d), BF16) + 2 * _nbytes((tm, d), F32)
                  + _nbytes((tm + HIST_ROWS, p), F32))
           + _nbytes((tm + nseg * HIST_ROWS, p), F32) + _nbytes((tm, d), BF16) + 6 * _nbytes((tm, d), F32))
    return pl.pallas_call(
        functools.partial(_merge_kernel, pos0=pos0, zero_first=zero_first),
        grid=(s_total // nseg, steps),
        in_specs=[row(a.shape[1]), row(d),
                  pl.BlockSpec((nseg, tl, p), lambda s, i: (s, i, 0)),
                  pl.BlockSpec((nseg, HIST_ROWS, p), hist_map),
                  row(d),
                  _resident(wup.shape), _resident(wga.shape), _resident(wgb.shape), _resident(wpool.shape),
                  _resident((1, d)), _resident(wout.shape), _resident((1, d)), _resident((1, d))],
        out_specs=(row(d), row(d)),
        out_shape=(jax.ShapeDtypeStruct((n, d), F32), jax.ShapeDtypeStruct((n, d), BF16)),
        scratch_shapes=[pltpu.VMEM((nseg, HIST_ROWS + tl, p), F32), pltpu.VMEM((tm, d), BF16)],
        compiler_params=pltpu.CompilerParams(
            dimension_semantics=("arbitrary", "arbitrary"), vmem_limit_bytes=_vmem_limit(est)),
        name="merge",
    )(a, h, u3, hist, x, wup, wga, wgb, wpool, ps, wout, gpost, gpre)


def _ffn_kernel(h_ref, x1_ref, w1_ref, w2_ref, g_ref, o_ref):
    f = pl.program_id(1)

    @pl.when(f == 0)
    def _():
        o_ref[...] = jnp.zeros_like(o_ref)

    z = jnp.square(jnp.maximum(_dot(h_ref[...], w1_ref[...]), 0.0)).astype(BF16)
    o_ref[...] += _dot(z, w2_ref[...])

    @pl.when(f == pl.num_programs(1) - 1)
    def _():
        o_ref[...] = x1_ref[...] + _rms(o_ref[...]) * g_ref[...]


def _ffn(h2, x1, w1, w2, g, *, tm, tf):
    n, d = x1.shape
    dff = w1.shape[1]
    est = (2 * _nbytes((tm, d), BF16) + 4 * _nbytes((tm, d), F32)
           + 4 * _nbytes((d, tf), BF16) + 2 * _nbytes((tm, tf), F32) + _nbytes((tm, d), F32))
    return pl.pallas_call(
        _ffn_kernel,
        grid=(n // tm, dff // tf),
        in_specs=[pl.BlockSpec((tm, d), lambda i, f: (i, 0)),
                  pl.BlockSpec((tm, d), lambda i, f: (i, 0)),
                  pl.BlockSpec((d, tf), lambda i, f: (0, f)),
                  pl.BlockSpec((tf, d), lambda i, f: (f, 0)),
                  pl.BlockSpec((1, d), lambda i, f: (0, 0))],
        out_specs=pl.BlockSpec((tm, d), lambda i, f: (i, 0)),
        out_shape=jax.ShapeDtypeStruct((n, d), F32),
        compiler_params=pltpu.CompilerParams(
            dimension_semantics=("arbitrary", "arbitrary"), vmem_limit_bytes=_vmem_limit(est)),
        name="ffn",
    )(h2, x1, w1, w2, g)


def _tile(n, pref):
    t = min(n, pref)
    while n % t:
        t //= 2
    return t


INPROJ_TM = 256
ATTN_TQ = 512
MERGE_TM = 256
FFN_TM = 512
FFN_TF = 1024


def _cum_logf_rows(logf_bth):
    b, t, h = logf_bth.shape
    rows = jnp.transpose(logf_bth, (0, 2, 1)).reshape(b * h, t)
    pad = (-t) % LANES
    if pad:
        rows = jnp.pad(rows, ((0, 0), (0, pad)))
    return _cumsum_lanes(rows)


def _layer(xp, xs, cache_k, cache_v, cache_logf, state_pool, g_mix_pre, w_in, b_f, w_attn_up, w_pool,
           pool_scale, w_out, g_mix_post, g_ffn_pre, w_ff1, w_ff2, g_ffn_post):
    bp, tp, d = xp.shape
    bs, ts, _ = xs.shape
    past = cache_k.shape[1]
    a, nh, p = ATTN_W, N_HEADS, w_pool.shape[0] * w_pool.shape[1]
    off_f = 3 * a
    off_u = off_f + nh
    off_ga = off_u + p
    off_gb = off_ga + d

    w_qkvu = jnp.concatenate([w_in[:, :off_f], w_in[:, off_u:off_ga]], axis=1).astype(BF16)
    w_f = jnp.pad(w_in[:, off_f:off_u], ((0, 0), (0, LANES - nh))).astype(BF16)
    w_ga = w_in[:, off_ga:off_gb].astype(BF16)
    w_gb = w_in[:, off_gb:].astype(BF16)
    w_up = w_attn_up.astype(BF16)
    w_pl = w_pool.astype(BF16)
    w_o = w_out.astype(BF16)
    w1 = w_ff1.astype(BF16)
    w2 = w_ff2.astype(BF16)
    row = lambda v: v.reshape(1, -1)
    g_pre, b_f2, ps = row(g_mix_pre), row(b_f), row(pool_scale)
    g_post, g_fpre, g_fpost = row(g_mix_post), row(g_ffn_pre), row(g_ffn_post)

    def project(x3):
        x2 = x3.reshape(-1, d)
        return x2, _inproj(x2, g_pre, w_qkvu, w_f, b_f2, tm=_tile(x2.shape[0], INPROJ_TM))

    def finish(x2, h, a_out, u3, hist, hist_map, nseg, tl, pos0, zero_first):
        x1, h2 = _merge(a_out, h, u3, hist, hist_map, x2, w_up, w_ga, w_gb, w_pl, ps, w_o, g_post, g_fpre,
                        nseg=nseg, tl=tl, pos0=pos0, zero_first=zero_first)
        n = x2.shape[0]
        return _ffn(h2, x1, w1, w2, g_fpost, tm=_tile(n, FFN_TM), tf=_tile(w1.shape[1], FFN_TF))

    x2, (h, q, k, kb, v, vb, u, logf) = project(xp)
    c_rows = _cum_logf_rows(logf.reshape(bp, tp, nh))
    nblk = nh // HEADS_PER_BLOCK
    three = lambda z: z.reshape(bp, tp, a)
    a_out = _attn_prompt(three(q), three(kb), three(v), c_rows.reshape(bp, nblk, HEADS_PER_BLOCK, tp),
                         tq=_tile(tp, ATTN_TQ))
    u3 = u.reshape(bp, tp, p)
    tl = _tile(tp, MERGE_TM)
    blocks_per_tile = tl // HIST_ROWS
    hist_map = lambda s, i: (s, jnp.maximum(i * blocks_per_tile - 1, 0), 0)
    yp = finish(x2, h, a_out.reshape(bp * tp, a), u3, u3, hist_map, 1, tl, 0, True).reshape(bp, tp, d)
    prompt_out = (yp, k.reshape(bp, tp, nh, HEAD_DIM), v.reshape(bp, tp, nh, HEAD_DIM),
                  logf.reshape(bp, tp, nh),
                  jnp.concatenate([jnp.zeros((bp, POOL_HIST, p), F32), u3], axis=1)[:, -POOL_HIST:])

    x2, (h, q, k, kb, v, vb, u, logf) = project(xs)
    logf3 = logf.reshape(bs, ts, nh)
    f_all = jnp.concatenate([cache_logf.astype(F32), logf3], axis=1)
    c_all = _cum_logf_rows(f_all).reshape(bs, nh, -1)
    c_cache = c_all[:, :, :past]
    c_new = c_all[:, :, past:past + ts]
    three = lambda z: z.reshape(bs, ts, a)
    a_out = _attn_sample(three(q), cache_k.reshape(bs, past, a), cache_v.reshape(bs, past, a),
                         three(kb), three(vb), c_cache, c_new, jnp.transpose(c_new, (0, 2, 1)))
    u3 = u.reshape(bs, ts, p)
    hist = jnp.pad(state_pool.astype(F32), ((0, 0), (HIST_ROWS - POOL_HIST, 0), (0, 0)))
    ys = finish(x2, h, a_out.reshape(bs * ts, a), u3, hist, lambda s, i: (s, 0, 0), bs, ts, past,
                False).reshape(bs, ts, d)
    sample_out = (ys, k.reshape(bs, ts, nh, HEAD_DIM), v.reshape(bs, ts, nh, HEAD_DIM), logf3,
                  jnp.concatenate([state_pool.astype(F32), u3], axis=1)[:, -POOL_HIST:])
    return prompt_out, sample_out


def kernel(x_prompt, x_sample, cache_k, cache_v, cache_logf, state_pool, g_mix_pre, w_in, b_f, w_attn_up,
           w_pool, pool_scale, w_out, g_mix_post, g_ffn_pre, w_ff1, w_ff2, g_ffn_post):
    depth = w_in.shape[0]
    xp, xs = x_prompt, x_sample
    per_layer = []
    for l in range(depth):
        po, so = _layer(xp, xs, cache_k[l], cache_v[l], cache_logf[l], state_pool[l], g_mix_pre[l], w_in[l],
                        b_f[l], w_attn_up[l], w_pool[l], pool_scale[l], w_out[l], g_mix_post[l],
                        g_ffn_pre[l], w_ff1[l], w_ff2[l], g_ffn_post[l])
        xp, xs = po[0], so[0]
        per_layer.append(po[1:] + so[1:])
    stacked = [jnp.stack(leaves, 0) for leaves in zip(*per_layer)]
    return (xp, xs, *stacked)
```

```python
import functools
import math

import jax
import jax.numpy as jnp
from jax import lax
from jax.experimental import pallas as pl
from jax.experimental.pallas import tpu as pltpu

N_HEADS = 16
HEAD_DIM = 64
ATTN_W = N_HEADS * HEAD_DIM
POOL_WINDOWS = (2, 4, 8, 16)
POOL_GROUPS = len(POOL_WINDOWS)
POOL_HIST = max(POOL_WINDOWS) - 1
EPS = 1e-6
SCALE = HEAD_DIM ** -0.5

LANES = 128
SUBLANES = 8
V7X_VMEM_BYTES = 64 * 1024 * 1024
VMEM_LIMIT_CAP = 60000 * 1024

HEADS_PER_BLOCK = LANES // HEAD_DIM
HIST_ROWS = 16

F32 = jnp.float32
BF16 = jnp.bfloat16


def _vmem_limit(nbytes):
    return int(min(VMEM_LIMIT_CAP, max(32 * 1024 * 1024, nbytes * 5 // 4)))


def _nbytes(shape, dtype):
    return math.prod(shape) * jnp.dtype(dtype).itemsize


def _rms(x):
    return x * lax.rsqrt(jnp.mean(x * x, axis=-1, keepdims=True) + EPS)


def _dot(a, b):
    return jnp.dot(a, b, preferred_element_type=F32)


def _dot_nt(a, b):
    return lax.dot_general(a, b, (((1,), (1,)), ((), ())), preferred_element_type=F32)


def _resident(shape):
    zeros = (0,) * len(shape)
    return pl.BlockSpec(shape, lambda *_: zeros, pipeline_mode=pl.Buffered(1))


def _inproj_kernel(x_ref, g_ref, w_ref, wf_ref, bf_ref,
                   h_ref, q_ref, k_ref, kb_ref, v_ref, vb_ref, u_ref, f_ref):
    h = (_rms(x_ref[...]) * g_ref[...]).astype(BF16)
    h_ref[...] = h
    a = q_ref.shape[1]
    q_ref[...] = (_dot(h, w_ref[:, 0:a]) * SCALE).astype(BF16)
    k = _dot(h, w_ref[:, a:2 * a])
    k_ref[...] = k
    kb_ref[...] = k.astype(BF16)
    v = _dot(h, w_ref[:, 2 * a:3 * a])
    v_ref[...] = v
    vb_ref[...] = v.astype(BF16)
    u_ref[...] = _dot(h, w_ref[:, 3 * a:])
    nh = f_ref.shape[1]
    f = _dot(h, wf_ref[...])[:, 0:nh] + bf_ref[...]
    f_ref[...] = jax.nn.log_sigmoid(f)


def _inproj(x, g, w, wf, bf, *, tm):
    n, d = x.shape
    a = ATTN_W
    p = w.shape[1] - 3 * a
    nh = bf.shape[1]
    row = lambda width: pl.BlockSpec((tm, width), lambda i: (i, 0))
    out_shape = (
        jax.ShapeDtypeStruct((n, d), BF16),
        jax.ShapeDtypeStruct((n, a), BF16),
        jax.ShapeDtypeStruct((n, a), F32),
        jax.ShapeDtypeStruct((n, a), BF16),
        jax.ShapeDtypeStruct((n, a), F32),
        jax.ShapeDtypeStruct((n, a), BF16),
        jax.ShapeDtypeStruct((n, p), F32),
        jax.ShapeDtypeStruct((n, nh), F32),
    )
    est = (2 * _nbytes((tm, d), F32) + _nbytes(w.shape, BF16) + _nbytes(wf.shape, BF16)
           + 2 * (_nbytes((tm, d), BF16) + 3 * _nbytes((tm, a), BF16) + 2 * _nbytes((tm, a), F32)
                  + _nbytes((tm, p), F32) + _nbytes((tm, LANES), F32))
           + 2 * _nbytes((tm, a), F32))
    return pl.pallas_call(
        _inproj_kernel,
        grid=(n // tm,),
        in_specs=[row(d), _resident((1, d)), _resident(w.shape), _resident(wf.shape), _resident((1, nh))],
        out_specs=(row(d), row(a), row(a), row(a), row(a), row(a), row(p), row(nh)),
        out_shape=out_shape,
        compiler_params=pltpu.CompilerParams(
            dimension_semantics=("arbitrary",), vmem_limit_bytes=_vmem_limit(est)),
        name="inproj",
    )(x, g, w, wf, bf)


def _cumsum_kernel(x_ref, o_ref):
    x = x_ref[...]
    length = x.shape[1]
    lane = lax.broadcasted_iota(jnp.int32, x.shape, 1)
    shift = 1
    while shift < length:
        x = x + jnp.where(lane >= shift, pltpu.roll(x, shift, axis=1), 0.0)
        shift *= 2
    o_ref[...] = x


def _cumsum_lanes(x):
    rows, length = x.shape
    spec = pl.BlockSpec((SUBLANES, length), lambda i: (i, 0))
    return pl.pallas_call(
        _cumsum_kernel,
        grid=(rows // SUBLANES,),
        in_specs=[spec],
        out_specs=spec,
        out_shape=jax.ShapeDtypeStruct(x.shape, F32),
        compiler_params=pltpu.CompilerParams(dimension_semantics=("arbitrary",)),
        name="cumsum",
    )(x)


VALUE_ROWS = HEAD_DIM + 16


def _attn_kernel(q_ref, k_ref, vf_ref, c_ref, o_ref,
                 vta_ref, ckrep_ref, s_ref, p_ref, alpha_ref, m_ref, acc_ref, *, tq):
    t = k_ref.shape[1]
    nq = t // tq
    vt = vf_ref[0].T
    for a in range(HEADS_PER_BLOCK):
        vta_ref[a, 0:HEAD_DIM, :] = vt[a * HEAD_DIM:(a + 1) * HEAD_DIM, :].astype(BF16)
        vta_ref[a, HEAD_DIM:VALUE_ROWS, :] = jnp.ones((VALUE_ROWS - HEAD_DIM, t), BF16)
        ckrep_ref[a] = jnp.broadcast_to(c_ref[0, 0, a:a + 1, :], (LANES, t)).T

    pairs = [(i, j) for i in range(nq) for j in range(i + 1)]
    rows = lambda j: slice(j * tq, (j + 1) * tq)

    def scores(w):
        i, j = pairs[w]
        q2 = q_ref[0, rows(i), :]
        lane = lax.broadcasted_iota(jnp.int32, q2.shape, 1)
        kt = k_ref[0, rows(j), :]
        for a in range(HEADS_PER_BLOCK):
            qa = jnp.where((lane // HEAD_DIM) == a, q2, jnp.zeros_like(q2))
            ckr = ckrep_ref[a, rows(j), :]
            s_ref[w % 2, a] = _dot_nt(kt, qa) - jnp.concatenate([ckr] * (tq // LANES), axis=1)

    def probs(w):
        i, j = pairs[w]
        for a in range(HEADS_PER_BLOCK):
            s = s_ref[w % 2, a]
            if j == i:
                r = lax.broadcasted_iota(jnp.int32, s.shape, 0)
                c = lax.broadcasted_iota(jnp.int32, s.shape, 1)
                s = jnp.where(r <= c, s, -jnp.inf)
            cqa = c_ref[0, 0, a:a + 1, rows(i)]
            smax = jnp.max(s, axis=0, keepdims=True) + cqa
            if j == 0:
                m_new = smax
            else:
                m_prev = m_ref[a]
                m_new = jnp.maximum(m_prev, smax)
                alpha_ref[w % 2, a] = jnp.exp(m_prev - m_new)
            p_ref[w % 2, a] = jnp.exp(s - (m_new - cqa)).astype(BF16)
            m_ref[a] = m_new

    def values(w):
        i, j = pairs[w]
        for a in range(HEADS_PER_BLOCK):
            pv = _dot(vta_ref[a, :, rows(j)], p_ref[w % 2, a])
            acc_ref[a] = pv if j == 0 else alpha_ref[w % 2, a] * acc_ref[a] + pv
        if j == i:
            ot = jnp.concatenate(
                [acc_ref[a, 0:HEAD_DIM, :] / acc_ref[a, HEAD_DIM:HEAD_DIM + 1, :]
                 for a in range(HEADS_PER_BLOCK)], axis=0)
            o_ref[0, rows(i), :] = ot.T.astype(o_ref.dtype)

    n = len(pairs)
    scores(0)
    for w in range(n):
        if w + 1 < n:
            scores(w + 1)
        probs(w)
        if w >= 1:
            values(w - 1)
    values(n - 1)


def _attn_prompt(q, kb, vf, c_rows, *, tq):
    b, t, a = q.shape
    nblk = a // LANES
    spec = pl.BlockSpec((1, t, LANES), lambda bi, hi: (bi, 0, hi))
    return pl.pallas_call(
        functools.partial(_attn_kernel, tq=tq),
        grid=(b, nblk),
        in_specs=[spec, spec, spec,
                  pl.BlockSpec((1, 1, HEADS_PER_BLOCK, t), lambda bi, hi: (bi, hi, 0, 0))],
        out_specs=spec,
        out_shape=jax.ShapeDtypeStruct((b, t, a), BF16),
        scratch_shapes=[pltpu.VMEM((HEADS_PER_BLOCK, VALUE_ROWS, t), BF16),
                        pltpu.VMEM((HEADS_PER_BLOCK, t, LANES), F32),
                        pltpu.VMEM((2, HEADS_PER_BLOCK, tq, tq), F32),
                        pltpu.VMEM((2, HEADS_PER_BLOCK, tq, tq), BF16),
                        pltpu.VMEM((2, HEADS_PER_BLOCK, 1, tq), F32),
                        pltpu.VMEM((HEADS_PER_BLOCK, 1, tq), F32),
                        pltpu.VMEM((HEADS_PER_BLOCK, VALUE_ROWS, tq), F32)],
        compiler_params=pltpu.CompilerParams(dimension_semantics=("arbitrary", "arbitrary")),
        name="attn_prompt",
    )(q, kb, vf, c_rows)


def _attn_sample_kernel(q_ref, kc_ref, vc_ref, kn_ref, vn_ref, cc_ref, cnr_ref, cnc_ref, o_ref,
                        kb_ref, vb_ref, s_ref, p_ref, pn_ref):
    n, a = q_ref.shape[1], q_ref.shape[2]
    nh = a // HEAD_DIM
    q = q_ref[0]
    qt = jnp.concatenate([q] * nh, axis=0)
    row_h = lax.broadcasted_iota(jnp.int32, qt.shape, 0) // n
    col_h = lax.broadcasted_iota(jnp.int32, qt.shape, 1) // HEAD_DIM
    qbd = jnp.where(row_h == col_h, qt, jnp.zeros_like(qt))
    kb_ref[...] = kc_ref[0].astype(BF16)
    vb_ref[...] = vc_ref[0].astype(BF16)
    s_ref[...] = _dot_nt(qbd, kb_ref[...])
    s_new = _dot_nt(qbd, kn_ref[0])
    cnc = cnc_ref[0]
    r = lax.broadcasted_iota(jnp.int32, (n, n), 0)
    c = lax.broadcasted_iota(jnp.int32, (n, n), 1)
    inv_l = []
    for h in range(nh):
        rows = slice(h * n, (h + 1) * n)
        cq = cnc[:, h:h + 1]
        sc = s_ref[rows, :] + (cq - cc_ref[0, h:h + 1, :])
        sn = s_new[rows, :] + (cq - cnr_ref[0, h:h + 1, :])
        sn = jnp.where(c <= r, sn, -jnp.inf)
        m = jnp.maximum(jnp.max(sc, axis=1, keepdims=True), jnp.max(sn, axis=1, keepdims=True))
        pc = jnp.exp(sc - m)
        pn = jnp.exp(sn - m)
        inv_l.append(1.0 / (jnp.sum(pc, axis=1, keepdims=True) + jnp.sum(pn, axis=1, keepdims=True)))
        p_ref[rows, :] = pc.astype(BF16)
        pn_ref[rows, :] = pn.astype(BF16)
    o = _dot(p_ref[...], vb_ref[...]) + _dot(pn_ref[...], vn_ref[0])
    for h in range(nh):
        rows = slice(h * n, (h + 1) * n)
        cols = slice(h * HEAD_DIM, (h + 1) * HEAD_DIM)
        o_ref[0, :, cols] = (o[rows, cols] * inv_l[h]).astype(o_ref.dtype)


def _attn_sample(q, kc, vc, kn, vn, cc, cnr, cnc):
    b, n, a = q.shape
    past = kc.shape[1]
    nh = a // HEAD_DIM
    per_b = lambda *tail: pl.BlockSpec((1,) + tail, lambda bi: (bi,) + (0,) * len(tail))
    est = (4 * _nbytes((past, a), F32) + 2 * _nbytes((past, a), BF16)
           + _nbytes((nh * n, past), F32) + _nbytes((nh * n, past), BF16) + 4 * _nbytes((nh * n, a), F32))
    return pl.pallas_call(
        _attn_sample_kernel,
        grid=(b,),
        in_specs=[per_b(n, a), per_b(past, a), per_b(past, a), per_b(n, a), per_b(n, a),
                  per_b(nh, past), per_b(nh, n), per_b(n, nh)],
        out_specs=per_b(n, a),
        out_shape=jax.ShapeDtypeStruct((b, n, a), BF16),
        scratch_shapes=[pltpu.VMEM((past, a), BF16), pltpu.VMEM((past, a), BF16),
                        pltpu.VMEM((nh * n, past), F32), pltpu.VMEM((nh * n, past), BF16),
                        pltpu.VMEM((nh * n, n), BF16)],
        compiler_params=pltpu.CompilerParams(
            dimension_semantics=("arbitrary",), vmem_limit_bytes=_vmem_limit(est)),
        name="attn_sample",
    )(q, kc, vc, kn, vn, cc, cnr, cnc)


def _merge_kernel(a_ref, h_ref, u_ref, hist_ref, x_ref, wup_ref, wga_ref, wgb_ref, wpool_ref, ps_ref,
                  wout_ref, gpost_ref, gpre_ref, x1_ref, h2_ref, ext_ref, m_ref,
                  *, pos0, zero_first):
    nseg, tl, _ = u_ref.shape
    i = pl.program_id(1)
    hist = hist_ref[...]
    if zero_first:
        hist = jnp.where(i == 0, 0.0, hist)
    ext_ref[:, 0:HIST_ROWS, :] = hist
    ext_ref[:, HIST_ROWS:HIST_ROWS + tl, :] = u_ref[...]
    gw = u_ref.shape[2] // POOL_GROUPS
    ogw = wpool_ref.shape[2]
    pos = pos0 + i * tl + lax.broadcasted_iota(jnp.int32, (nseg, tl, gw), 1)
    a = a_ref[...]
    h = h_ref[...]
    for g, w in enumerate(POOL_WINDOWS):
        cs = slice(g * gw, (g + 1) * gw)
        cur = ext_ref[:, HIST_ROWS:HIST_ROWS + tl, cs]
        tot = cur
        for s in range(1, w):
            tot = tot + ext_ref[:, HIST_ROWS - s:HIST_ROWS - s + tl, cs]
        cnt = jnp.minimum(pos + 1, w).astype(F32)
        pooled = (tot / cnt - cur).reshape(nseg * tl, gw).astype(BF16)
        os_ = slice(g * ogw, (g + 1) * ogw)
        br_b = _dot(pooled, wpool_ref[g]) * ps_ref[:, os_]
        br_a = _dot(a, wup_ref[:, os_])
        ga = _dot(h, wga_ref[:, os_])
        gb = _dot(h, wgb_ref[:, os_])
        m_ref[:, os_] = (jax.nn.sigmoid(ga) * br_a + jax.nn.sigmoid(gb) * br_b).astype(BF16)
    x1 = x_ref[...] + _rms(_dot(m_ref[...], wout_ref[...])) * gpost_ref[...]
    x1_ref[...] = x1
    h2_ref[...] = (_rms(x1) * gpre_ref[...]).astype(BF16)


def _merge(a, h, u3, hist, hist_map, x, wup, wga, wgb, wpool, ps, wout, gpost, gpre,
           *, nseg, tl, pos0, zero_first):
    s_total, l_total, p = u3.shape
    n, d = x.shape
    tm = nseg * tl
    steps = l_total // tl
    assert nseg == 1 or steps == 1
    row = lambda width: pl.BlockSpec((tm, width), lambda s, i: (s * steps + i, 0))
    est = (_nbytes(wup.shape, BF16) + 2 * _nbytes(wga.shape, BF16) + _nbytes(wpool.shape, BF16)
           + _nbytes(wout.shape, BF16)
           + 2 * (_nbytes((tm, a.shape[1]), BF16) + 2 * _nbytes((tm, d), BF16) + 2 * _nbytes((tm, d), F32)
                  + _nbytes((tm + HIST_ROWS, p), F32))
           + _nbytes((tm + nseg * HIST_ROWS, p), F32) + _nbytes((tm, d), BF16) + 6 * _nbytes((tm, d), F32))
    return pl.pallas_call(
        functools.partial(_merge_kernel, pos0=pos0, zero_first=zero_first),
        grid=(s_total // nseg, steps),
        in_specs=[row(a.shape[1]), row(d),
                  pl.BlockSpec((nseg, tl, p), lambda s, i: (s, i, 0)),
                  pl.BlockSpec((nseg, HIST_ROWS, p), hist_map),
                  row(d),
                  _resident(wup.shape), _resident(wga.shape), _resident(wgb.shape), _resident(wpool.shape),
                  _resident((1, d)), _resident(wout.shape), _resident((1, d)), _resident((1, d))],
        out_specs=(row(d), row(d)),
        out_shape=(jax.ShapeDtypeStruct((n, d), F32), jax.ShapeDtypeStruct((n, d), BF16)),
        scratch_shapes=[pltpu.VMEM((nseg, HIST_ROWS + tl, p), F32), pltpu.VMEM((tm, d), BF16)],
        compiler_params=pltpu.CompilerParams(
            dimension_semantics=("arbitrary", "arbitrary"), vmem_limit_bytes=_vmem_limit(est)),
        name="merge",
    )(a, h, u3, hist, x, wup, wga, wgb, wpool, ps, wout, gpost, gpre)


def _ffn_kernel(h_ref, x1_ref, w1_ref, w2_ref, g_ref, o_ref):
    f = pl.program_id(1)

    @pl.when(f == 0)
    def _():
        o_ref[...] = jnp.zeros_like(o_ref)

    z = jnp.square(jnp.maximum(_dot(h_ref[...], w1_ref[...]), 0.0)).astype(BF16)
    o_ref[...] += _dot(z, w2_ref[...])

    @pl.when(f == pl.num_programs(1) - 1)
    def _():
        o_ref[...] = x1_ref[...] + _rms(o_ref[...]) * g_ref[...]


def _ffn(h2, x1, w1, w2, g, *, tm, tf):
    n, d = x1.shape
    dff = w1.shape[1]
    est = (2 * _nbytes((tm, d), BF16) + 4 * _nbytes((tm, d), F32)
           + 4 * _nbytes((d, tf), BF16) + 2 * _nbytes((tm, tf), F32) + _nbytes((tm, d), F32))
    return pl.pallas_call(
        _ffn_kernel,
        grid=(n // tm, dff // tf),
        in_specs=[pl.BlockSpec((tm, d), lambda i, f: (i, 0)),
                  pl.BlockSpec((tm, d), lambda i, f: (i, 0)),
                  pl.BlockSpec((d, tf), lambda i, f: (0, f)),
                  pl.BlockSpec((tf, d), lambda i, f: (f, 0)),
                  pl.BlockSpec((1, d), lambda i, f: (0, 0))],
        out_specs=pl.BlockSpec((tm, d), lambda i, f: (i, 0)),
        out_shape=jax.ShapeDtypeStruct((n, d), F32),
        compiler_params=pltpu.CompilerParams(
            dimension_semantics=("arbitrary", "arbitrary"), vmem_limit_bytes=_vmem_limit(est)),
        name="ffn",
    )(h2, x1, w1, w2, g)


def _tile(n, pref):
    t = min(n, pref)
    while n % t:
        t //= 2
    return t


INPROJ_TM = 256
ATTN_TQ = 512
MERGE_TM = 256
FFN_TM = 512
FFN_TF = 1024


def _cum_logf_rows(logf_bth):
    b, t, h = logf_bth.shape
    rows = jnp.transpose(logf_bth, (0, 2, 1)).reshape(b * h, t)
    pad = (-t) % LANES
    if pad:
        rows = jnp.pad(rows, ((0, 0), (0, pad)))
    return _cumsum_lanes(rows)


def _layer(xp, xs, cache_k, cache_v, cache_logf, state_pool, g_mix_pre, w_in, b_f, w_attn_up, w_pool,
           pool_scale, w_out, g_mix_post, g_ffn_pre, w_ff1, w_ff2, g_ffn_post):
    bp, tp, d = xp.shape
    bs, ts, _ = xs.shape
    past = cache_k.shape[1]
    a, nh, p = ATTN_W, N_HEADS, w_pool.shape[0] * w_pool.shape[1]
    off_f = 3 * a
    off_u = off_f + nh
    off_ga = off_u + p
    off_gb = off_ga + d

    w_qkvu = jnp.concatenate([w_in[:, :off_f], w_in[:, off_u:off_ga]], axis=1).astype(BF16)
    w_f = jnp.pad(w_in[:, off_f:off_u], ((0, 0), (0, LANES - nh))).astype(BF16)
    w_ga = w_in[:, off_ga:off_gb].astype(BF16)
    w_gb = w_in[:, off_gb:].astype(BF16)
    w_up = w_attn_up.astype(BF16)
    w_pl = w_pool.astype(BF16)
    w_o = w_out.astype(BF16)
    w1 = w_ff1.astype(BF16)
    w2 = w_ff2.astype(BF16)
    row = lambda v: v.reshape(1, -1)
    g_pre, b_f2, ps = row(g_mix_pre), row(b_f), row(pool_scale)
    g_post, g_fpre, g_fpost = row(g_mix_post), row(g_ffn_pre), row(g_ffn_post)

    def project(x3):
        x2 = x3.reshape(-1, d)
        return x2, _inproj(x2, g_pre, w_qkvu, w_f, b_f2, tm=_tile(x2.shape[0], INPROJ_TM))

    def finish(x2, h, a_out, u3, hist, hist_map, nseg, tl, pos0, zero_first):
        x1, h2 = _merge(a_out, h, u3, hist, hist_map, x2, w_up, w_ga, w_gb, w_pl, ps, w_o, g_post, g_fpre,
                        nseg=nseg, tl=tl, pos0=pos0, zero_first=zero_first)
        n = x2.shape[0]
        return _ffn(h2, x1, w1, w2, g_fpost, tm=_tile(n, FFN_TM), tf=_tile(w1.shape[1], FFN_TF))

    x2, (h, q, k, kb, v, vb, u, logf) = project(xp)
    c_rows = _cum_logf_rows(logf.reshape(bp, tp, nh))
    nblk = nh // HEADS_PER_BLOCK
    three = lambda z: z.reshape(bp, tp, a)
    a_out = _attn_prompt(three(q), three(kb), three(v), c_rows.reshape(bp, nblk, HEADS_PER_BLOCK, tp),
                         tq=_tile(tp, ATTN_TQ))
    u3 = u.reshape(bp, tp, p)
    tl = _tile(tp, MERGE_TM)
    blocks_per_tile = tl // HIST_ROWS
    hist_map = lambda s, i: (s, jnp.maximum(i * blocks_per_tile - 1, 0), 0)
    yp = finish(x2, h, a_out.reshape(bp * tp, a), u3, u3, hist_map, 1, tl, 0, True).reshape(bp, tp, d)
    prompt_out = (yp, k.reshape(bp, tp, nh, HEAD_DIM), v.reshape(bp, tp, nh, HEAD_DIM),
                  logf.reshape(bp, tp, nh),
                  jnp.concatenate([jnp.zeros((bp, POOL_HIST, p), F32), u3], axis=1)[:, -POOL_HIST:])

    x2, (h, q, k, kb, v, vb, u, logf) = project(xs)
    logf3 = logf.reshape(bs, ts, nh)
    f_all = jnp.concatenate([cache_logf.astype(F32), logf3], axis=1)
    c_all = _cum_logf_rows(f_all).reshape(bs, nh, -1)
    c_cache = c_all[:, :, :past]
    c_new = c_all[:, :, past:past + ts]
    three = lambda z: z.reshape(bs, ts, a)
    a_out = _attn_sample(three(q), cache_k.reshape(bs, past, a), cache_v.reshape(bs, past, a),
                         three(kb), three(vb), c_cache, c_new, jnp.transpose(c_new, (0, 2, 1)))
    u3 = u.reshape(bs, ts, p)
    hist = jnp.pad(state_pool.astype(F32), ((0, 0), (HIST_ROWS - POOL_HIST, 0), (0, 0)))
    ys = finish(x2, h, a_out.reshape(bs * ts, a), u3, hist, lambda s, i: (s, 0, 0), bs, ts, past,
                False).reshape(bs, ts, d)
    sample_out = (ys, k.reshape(bs, ts, nh, HEAD_DIM), v.reshape(bs, ts, nh, HEAD_DIM), logf3,
                  jnp.concatenate([state_pool.astype(F32), u3], axis=1)[:, -POOL_HIST:])
    return prompt_out, sample_out


def kernel(x_prompt, x_sample, cache_k, cache_v, cache_logf, state_pool, g_mix_pre, w_in, b_f, w_attn_up,
           w_pool, pool_scale, w_out, g_mix_post, g_ffn_pre, w_ff1, w_ff2, g_ffn_post):
    depth = w_in.shape[0]
    xp, xs = x_prompt, x_sample
    per_layer = []
    for l in range(depth):
        po, so = _layer(xp, xs, cache_k[l], cache_v[l], cache_logf[l], state_pool[l], g_mix_pre[l], w_in[l],
                        b_f[l], w_attn_up[l], w_pool[l], pool_scale[l], w_out[l], g_mix_post[l],
                        g_ffn_pre[l], w_ff1[l], w_ff2[l], g_ffn_post[l])
        xp, xs = po[0], so[0]
        per_layer.append(po[1:] + so[1:])
    stacked = [jnp.stack(leaves, 0) for leaves in zip(*per_layer)]
    return (xp, xs, *stacked)
```

```python
import functools
import math

import jax
import jax.numpy as jnp
from jax import lax
from jax.experimental import pallas as pl
from jax.experimental.pallas import tpu as pltpu

N_HEADS = 16
HEAD_DIM = 64
ATTN_W = N_HEADS * HEAD_DIM
POOL_WINDOWS = (2, 4, 8, 16)
POOL_GROUPS = len(POOL_WINDOWS)
POOL_HIST = max(POOL_WINDOWS) - 1
EPS = 1e-6
SCALE = HEAD_DIM ** -0.5

LANES = 128
SUBLANES = 8
V7X_VMEM_BYTES = 64 * 1024 * 1024
VMEM_LIMIT_CAP = 60000 * 1024

HEADS_PER_BLOCK = LANES // HEAD_DIM
HIST_ROWS = 16

F32 = jnp.float32
BF16 = jnp.bfloat16


def _vmem_limit(nbytes):
    return int(min(VMEM_LIMIT_CAP, max(32 * 1024 * 1024, nbytes * 5 // 4)))


def _nbytes(shape, dtype):
    return math.prod(shape) * jnp.dtype(dtype).itemsize


def _rms(x):
    return x * lax.rsqrt(jnp.mean(x * x, axis=-1, keepdims=True) + EPS)


def _dot(a, b):
    return jnp.dot(a, b, preferred_element_type=F32)


def _dot_nt(a, b):
    return lax.dot_general(a, b, (((1,), (1,)), ((), ())), preferred_element_type=F32)


def _resident(shape):
    zeros = (0,) * len(shape)
    return pl.BlockSpec(shape, lambda *_: zeros, pipeline_mode=pl.Buffered(1))


def _store_head_major(x, scr_ref, out_ref):
    tm = x.shape[0]
    nh = x.shape[1] // HEAD_DIM
    for c in range(x.shape[1] // LANES):
        blk = x[:, c * LANES:(c + 1) * LANES]
        scr_ref[pl.ds(HEADS_PER_BLOCK * c, tm, stride=nh), :] = blk
        scr_ref[pl.ds(HEADS_PER_BLOCK * c + 1, tm, stride=nh), :] = pltpu.roll(blk, HEAD_DIM, axis=1)
    out_ref[...] = scr_ref[:, 0:HEAD_DIM]


def _inproj_kernel(x_ref, g_ref, w_ref, wf_ref, bf_ref,
                   h_ref, q_ref, kb_ref, k4_ref, v4_ref, vx_ref, u_ref, f_ref, kscr_ref, vscr_ref,
                   *, transposed_v):
    h = (_rms(x_ref[...]) * g_ref[...]).astype(BF16)
    h_ref[...] = h
    a = q_ref.shape[1]
    q_ref[...] = (_dot(h, w_ref[:, 0:a]) * SCALE).astype(BF16)
    k = _dot(h, w_ref[:, a:2 * a])
    kb_ref[...] = k.astype(BF16)
    _store_head_major(k, kscr_ref, k4_ref)
    v = _dot(h, w_ref[:, 2 * a:3 * a])
    _store_head_major(v, vscr_ref, v4_ref)
    if transposed_v:
        vx_ref[0] = v.T.astype(BF16)
    else:
        vx_ref[...] = v.astype(BF16)
    u_ref[...] = _dot(h, w_ref[:, 3 * a:])
    nh = f_ref.shape[1]
    f = _dot(h, wf_ref[...])[:, 0:nh] + bf_ref[...]
    f_ref[...] = jax.nn.log_sigmoid(f)


def _inproj(x, g, w, wf, bf, *, tm, seq, transposed_v):
    n, d = x.shape
    a = ATTN_W
    p = w.shape[1] - 3 * a
    nh = bf.shape[1]
    row = lambda width: pl.BlockSpec((tm, width), lambda i: (i, 0))
    head_major = pl.BlockSpec((tm * N_HEADS, HEAD_DIM), lambda i: (i, 0))
    if transposed_v:
        steps = seq // tm
        vx_shape = jax.ShapeDtypeStruct((n // seq, a, seq), BF16)
        vx_spec = pl.BlockSpec((1, a, tm), lambda i: (i // steps, 0, i % steps))
    else:
        vx_shape = jax.ShapeDtypeStruct((n, a), BF16)
        vx_spec = row(a)
    out_shape = (
        jax.ShapeDtypeStruct((n, d), BF16),
        jax.ShapeDtypeStruct((n, a), BF16),
        jax.ShapeDtypeStruct((n, a), BF16),
        jax.ShapeDtypeStruct((n * N_HEADS, HEAD_DIM), F32),
        jax.ShapeDtypeStruct((n * N_HEADS, HEAD_DIM), F32),
        vx_shape,
        jax.ShapeDtypeStruct((n, p), F32),
        jax.ShapeDtypeStruct((n, nh), F32),
    )
    est = (2 * _nbytes((tm, d), F32) + _nbytes(w.shape, BF16) + _nbytes(wf.shape, BF16)
           + 2 * (_nbytes((tm, d), BF16) + 3 * _nbytes((tm, a), BF16) + 2 * _nbytes((tm * N_HEADS, LANES), F32)
                  + _nbytes((tm, p), F32) + _nbytes((tm, LANES), F32))
           + 2 * _nbytes((tm * N_HEADS, LANES), F32) + 2 * _nbytes((tm, a), F32))
    return pl.pallas_call(
        functools.partial(_inproj_kernel, transposed_v=transposed_v),
        grid=(n // tm,),
        in_specs=[row(d), _resident((1, d)), _resident(w.shape), _resident(wf.shape), _resident((1, nh))],
        out_specs=(row(d), row(a), row(a), head_major, head_major, vx_spec, row(p), row(nh)),
        out_shape=out_shape,
        scratch_shapes=[pltpu.VMEM((tm * N_HEADS, LANES), F32), pltpu.VMEM((tm * N_HEADS, LANES), F32)],
        compiler_params=pltpu.CompilerParams(
            dimension_semantics=("arbitrary",), vmem_limit_bytes=_vmem_limit(est)),
        name="inproj",
    )(x, g, w, wf, bf)


def _cumsum_kernel(x_ref, o_ref):
    x = x_ref[...]
    length = x.shape[1]
    lane = lax.broadcasted_iota(jnp.int32, x.shape, 1)
    shift = 1
    while shift < length:
        x = x + jnp.where(lane >= shift, pltpu.roll(x, shift, axis=1), 0.0)
        shift *= 2
    o_ref[...] = x


def _cumsum_lanes(x):
    rows, length = x.shape
    spec = pl.BlockSpec((SUBLANES, length), lambda i: (i, 0))
    return pl.pallas_call(
        _cumsum_kernel,
        grid=(rows // SUBLANES,),
        in_specs=[spec],
        out_specs=spec,
        out_shape=jax.ShapeDtypeStruct(x.shape, F32),
        compiler_params=pltpu.CompilerParams(dimension_semantics=("arbitrary",)),
        name="cumsum",
    )(x)


VALUE_ROWS = HEAD_DIM + 16


def _attn_kernel(q_ref, k_ref, vt_ref, c_ref, o_ref,
                 vta_ref, ckrep_ref, s_ref, p_ref, alpha_ref, m_ref, acc_ref, *, tq):
    t = k_ref.shape[1]
    nq = t // tq
    for a in range(HEADS_PER_BLOCK):
        vta_ref[a, 0:HEAD_DIM, :] = vt_ref[0, a * HEAD_DIM:(a + 1) * HEAD_DIM, :]
        vta_ref[a, HEAD_DIM:VALUE_ROWS, :] = jnp.ones((VALUE_ROWS - HEAD_DIM, t), BF16)
        ckrep_ref[a] = jnp.broadcast_to(c_ref[0, 0, a:a + 1, :], (LANES, t)).T

    pairs = [(i, j) for i in range(nq) for j in range(i + 1)]
    rows = lambda j: slice(j * tq, (j + 1) * tq)

    def scores(w):
        i, j = pairs[w]
        q2 = q_ref[0, rows(i), :]
        lane = lax.broadcasted_iota(jnp.int32, q2.shape, 1)
        kt = k_ref[0, rows(j), :]
        for a in range(HEADS_PER_BLOCK):
            qa = jnp.where((lane // HEAD_DIM) == a, q2, jnp.zeros_like(q2))
            ckr = ckrep_ref[a, rows(j), :]
            s_ref[w % 2, a] = _dot_nt(kt, qa) - jnp.concatenate([ckr] * (tq // LANES), axis=1)

    def probs(w):
        i, j = pairs[w]
        for a in range(HEADS_PER_BLOCK):
            s = s_ref[w % 2, a]
            if j == i:
                r = lax.broadcasted_iota(jnp.int32, s.shape, 0)
                c = lax.broadcasted_iota(jnp.int32, s.shape, 1)
                s = jnp.where(r <= c, s, -jnp.inf)
            cqa = c_ref[0, 0, a:a + 1, rows(i)]
            smax = jnp.max(s, axis=0, keepdims=True) + cqa
            if j == 0:
                m_new = smax
            else:
                m_prev = m_ref[a]
                m_new = jnp.maximum(m_prev, smax)
                alpha_ref[w % 2, a] = jnp.exp(m_prev - m_new)
            p_ref[w % 2, a] = jnp.exp(s - (m_new - cqa)).astype(BF16)
            m_ref[a] = m_new

    def values(w):
        i, j = pairs[w]
        for a in range(HEADS_PER_BLOCK):
            pv = _dot(vta_ref[a, :, rows(j)], p_ref[w % 2, a])
            acc_ref[a] = pv if j == 0 else alpha_ref[w % 2, a] * acc_ref[a] + pv
        if j == i:
            ot = jnp.concatenate(
                [acc_ref[a, 0:HEAD_DIM, :] / acc_ref[a, HEAD_DIM:HEAD_DIM + 1, :]
                 for a in range(HEADS_PER_BLOCK)], axis=0)
            o_ref[0, rows(i), :] = ot.T.astype(o_ref.dtype)

    n = len(pairs)
    scores(0)
    for w in range(n):
        if w + 1 < n:
            scores(w + 1)
        probs(w)
        if w >= 1:
            values(w - 1)
    values(n - 1)


def _attn_prompt(q, kb, vt, c_rows, *, tq):
    b, t, a = q.shape
    nblk = a // LANES
    spec = pl.BlockSpec((1, t, LANES), lambda bi, hi: (bi, 0, hi))
    return pl.pallas_call(
        functools.partial(_attn_kernel, tq=tq),
        grid=(b, nblk),
        in_specs=[spec, spec, pl.BlockSpec((1, LANES, t), lambda bi, hi: (bi, hi, 0)),
                  pl.BlockSpec((1, 1, HEADS_PER_BLOCK, t), lambda bi, hi: (bi, hi, 0, 0))],
        out_specs=spec,
        out_shape=jax.ShapeDtypeStruct((b, t, a), BF16),
        scratch_shapes=[pltpu.VMEM((HEADS_PER_BLOCK, VALUE_ROWS, t), BF16),
                        pltpu.VMEM((HEADS_PER_BLOCK, t, LANES), F32),
                        pltpu.VMEM((2, HEADS_PER_BLOCK, tq, tq), F32),
                        pltpu.VMEM((2, HEADS_PER_BLOCK, tq, tq), BF16),
                        pltpu.VMEM((2, HEADS_PER_BLOCK, 1, tq), F32),
                        pltpu.VMEM((HEADS_PER_BLOCK, 1, tq), F32),
                        pltpu.VMEM((HEADS_PER_BLOCK, VALUE_ROWS, tq), F32)],
        compiler_params=pltpu.CompilerParams(dimension_semantics=("arbitrary", "arbitrary")),
        name="attn_prompt",
    )(q, kb, vt, c_rows)


def _attn_sample_kernel(q_ref, kc_ref, vc_ref, kn_ref, vn_ref, cc_ref, cnr_ref, cnc_ref, o_ref,
                        kb_ref, vb_ref, s_ref, p_ref, pn_ref):
    n, a = q_ref.shape[1], q_ref.shape[2]
    nh = a // HEAD_DIM
    q = q_ref[0]
    qt = jnp.concatenate([q] * nh, axis=0)
    row_h = lax.broadcasted_iota(jnp.int32, qt.shape, 0) // n
    col_h = lax.broadcasted_iota(jnp.int32, qt.shape, 1) // HEAD_DIM
    qbd = jnp.where(row_h == col_h, qt, jnp.zeros_like(qt))
    kb_ref[...] = kc_ref[0].astype(BF16)
    vb_ref[...] = vc_ref[0].astype(BF16)
    s_ref[...] = _dot_nt(qbd, kb_ref[...])
    s_new = _dot_nt(qbd, kn_ref[0])
    cnc = cnc_ref[0]
    r = lax.broadcasted_iota(jnp.int32, (n, n), 0)
    c = lax.broadcasted_iota(jnp.int32, (n, n), 1)
    inv_l = []
    for h in range(nh):
        rows = slice(h * n, (h + 1) * n)
        cq = cnc[:, h:h + 1]
        sc = s_ref[rows, :] + (cq - cc_ref[0, h:h + 1, :])
        sn = s_new[rows, :] + (cq - cnr_ref[0, h:h + 1, :])
        sn = jnp.where(c <= r, sn, -jnp.inf)
        m = jnp.maximum(jnp.max(sc, axis=1, keepdims=True), jnp.max(sn, axis=1, keepdims=True))
        pc = jnp.exp(sc - m)
        pn = jnp.exp(sn - m)
        inv_l.append(1.0 / (jnp.sum(pc, axis=1, keepdims=True) + jnp.sum(pn, axis=1, keepdims=True)))
        p_ref[rows, :] = pc.astype(BF16)
        pn_ref[rows, :] = pn.astype(BF16)
    o = _dot(p_ref[...], vb_ref[...]) + _dot(pn_ref[...], vn_ref[0])
    for h in range(nh):
        rows = slice(h * n, (h + 1) * n)
        cols = slice(h * HEAD_DIM, (h + 1) * HEAD_DIM)
        o_ref[0, :, cols] = (o[rows, cols] * inv_l[h]).astype(o_ref.dtype)


def _attn_sample(q, kc, vc, kn, vn, cc, cnr, cnc):
    b, n, a = q.shape
    past = kc.shape[1]
    nh = a // HEAD_DIM
    per_b = lambda *tail: pl.BlockSpec((1,) + tail, lambda bi: (bi,) + (0,) * len(tail))
    est = (4 * _nbytes((past, a), F32) + 2 * _nbytes((past, a), BF16)
           + _nbytes((nh * n, past), F32) + _nbytes((nh * n, past), BF16) + 4 * _nbytes((nh * n, a), F32))
    return pl.pallas_call(
        _attn_sample_kernel,
        grid=(b,),
        in_specs=[per_b(n, a), per_b(past, a), per_b(past, a), per_b(n, a), per_b(n, a),
                  per_b(nh, past), per_b(nh, n), per_b(n, nh)],
        out_specs=per_b(n, a),
        out_shape=jax.ShapeDtypeStruct((b, n, a), BF16),
        scratch_shapes=[pltpu.VMEM((past, a), BF16), pltpu.VMEM((past, a), BF16),
                        pltpu.VMEM((nh * n, past), F32), pltpu.VMEM((nh * n, past), BF16),
                        pltpu.VMEM((nh * n, n), BF16)],
        compiler_params=pltpu.CompilerParams(
            dimension_semantics=("arbitrary",), vmem_limit_bytes=_vmem_limit(est)),
        name="attn_sample",
    )(q, kc, vc, kn, vn, cc, cnr, cnc)


def _merge_kernel(a_ref, h_ref, u_ref, hist_ref, x_ref, wup_ref, wga_ref, wgb_ref, wpool_ref, ps_ref,
                  wout_ref, gpost_ref, gpre_ref, x1_ref, h2_ref, ext_ref, m_ref,
                  *, pos0, zero_first):
    nseg, tl, _ = u_ref.shape
    i = pl.program_id(1)
    hist = hist_ref[...]
    if zero_first:
        hist = jnp.where(i == 0, 0.0, hist)
    ext_ref[:, 0:HIST_ROWS, :] = hist
    ext_ref[:, HIST_ROWS:HIST_ROWS + tl, :] = u_ref[...]
    gw = u_ref.shape[2] // POOL_GROUPS
    ogw = wpool_ref.shape[2]
    pos = pos0 + i * tl + lax.broadcasted_iota(jnp.int32, (nseg, tl, gw), 1)
    a = a_ref[...]
    h = h_ref[...]
    for g, w in enumerate(POOL_WINDOWS):
        cs = slice(g * gw, (g + 1) * gw)
        cur = ext_ref[:, HIST_ROWS:HIST_ROWS + tl, cs]
        tot = cur
        for s in range(1, w):
            tot = tot + ext_ref[:, HIST_ROWS - s:HIST_ROWS - s + tl, cs]
        cnt = jnp.minimum(pos + 1, w).astype(F32)
        pooled = (tot / cnt - cur).reshape(nseg * tl, gw).astype(BF16)
        os_ = slice(g * ogw, (g + 1) * ogw)
        br_b = _dot(pooled, wpool_ref[g]) * ps_ref[:, os_]
        br_a = _dot(a, wup_ref[:, os_])
        ga = _dot(h, wga_ref[:, os_])
        gb = _dot(h, wgb_ref[:, os_])
        m_ref[:, os_] = (jax.nn.sigmoid(ga) * br_a + jax.nn.sigmoid(gb) * br_b).astype(BF16)
    x1 = x_ref[...] + _rms(_dot(m_ref[...], wout_ref[...])) * gpost_ref[...]
    x1_ref[...] = x1
    h2_ref[...] = (_rms(x1) * gpre_ref[...]).astype(BF16)


def _merge(a, h, u3, hist, hist_map, x, wup, wga, wgb, wpool, ps, wout, gpost, gpre,
           *, nseg, tl, pos0, zero_first):
    s_total, l_total, p = u3.shape
    n, d = x.shape
    tm = nseg * tl
    steps = l_total // tl
    assert nseg == 1 or steps == 1
    row = lambda width: pl.BlockSpec((tm, width), lambda s, i: (s * steps + i, 0))
    est = (_nbytes(wup.shape, BF16) + 2 * _nbytes(wga.shape, BF16) + _nbytes(wpool.shape, BF16)
           + _nbytes(wout.shape, BF16)
           + 2 * (_nbytes((tm, a.shape[1]), BF16) + 2 * _nbytes((tm, d), BF16) + 2 * _nbytes((tm, d), F32)
                  + _nbytes((tm + HIST_ROWS, p), F32))
           + _nbytes((tm + nseg * HIST_ROWS, p), F32) + _nbytes((tm, d), BF16) + 6 * _nbytes((tm, d), F32))
    return pl.pallas_call(
        functools.partial(_merge_kernel, pos0=pos0, zero_first=zero_first),
        grid=(s_total // nseg, steps),
        in_specs=[row(a.shape[1]), row(d),
                  pl.BlockSpec((nseg, tl, p), lambda s, i: (s, i, 0)),
                  pl.BlockSpec((nseg, HIST_ROWS, p), hist_map),
                  row(d),
                  _resident(wup.shape), _resident(wga.shape), _resident(wgb.shape), _resident(wpool.shape),
                  _resident((1, d)), _resident(wout.shape), _resident((1, d)), _resident((1, d))],
        out_specs=(row(d), row(d)),
        out_shape=(jax.ShapeDtypeStruct((n, d), F32), jax.ShapeDtypeStruct((n, d), BF16)),
        scratch_shapes=[pltpu.VMEM((nseg, HIST_ROWS + tl, p), F32), pltpu.VMEM((tm, d), BF16)],
        compiler_params=pltpu.CompilerParams(
            dimension_semantics=("arbitrary", "arbitrary"), vmem_limit_bytes=_vmem_limit(est)),
        name="merge",
    )(a, h, u3, hist, x, wup, wga, wgb, wpool, ps, wout, gpost, gpre)


def _ffn_kernel(h_ref, x1_ref, w1_ref, w2_ref, g_ref, o_ref):
    f = pl.program_id(1)

    @pl.when(f == 0)
    def _():
        o_ref[...] = jnp.zeros_like(o_ref)

    z = jnp.square(jnp.maximum(_dot(h_ref[...], w1_ref[...]), 0.0)).astype(BF16)
    o_ref[...] += _dot(z, w2_ref[...])

    @pl.when(f == pl.num_programs(1) - 1)
    def _():
        o_ref[...] = x1_ref[...] + _rms(o_ref[...]) * g_ref[...]


def _ffn(h2, x1, w1, w2, g, *, tm, tf):
    n, d = x1.shape
    dff = w1.shape[1]
    est = (2 * _nbytes((tm, d), BF16) + 4 * _nbytes((tm, d), F32)
           + 4 * _nbytes((d, tf), BF16) + 2 * _nbytes((tm, tf), F32) + _nbytes((tm, d), F32))
    return pl.pallas_call(
        _ffn_kernel,
        grid=(n // tm, dff // tf),
        in_specs=[pl.BlockSpec((tm, d), lambda i, f: (i, 0)),
                  pl.BlockSpec((tm, d), lambda i, f: (i, 0)),
                  pl.BlockSpec((d, tf), lambda i, f: (0, f)),
                  pl.BlockSpec((tf, d), lambda i, f: (f, 0)),
                  pl.BlockSpec((1, d), lambda i, f: (0, 0))],
        out_specs=pl.BlockSpec((tm, d), lambda i, f: (i, 0)),
        out_shape=jax.ShapeDtypeStruct((n, d), F32),
        compiler_params=pltpu.CompilerParams(
            dimension_semantics=("arbitrary", "arbitrary"), vmem_limit_bytes=_vmem_limit(est)),
        name="ffn",
    )(h2, x1, w1, w2, g)


def _tile(n, pref):
    t = min(n, pref)
    while n % t:
        t //= 2
    return t


INPROJ_TM = 256
ATTN_TQ = 512
MERGE_TM = 256
FFN_TM = 512
FFN_TF = 1024


def _cum_logf_rows(logf_bth):
    b, t, h = logf_bth.shape
    rows = jnp.transpose(logf_bth, (0, 2, 1)).reshape(b * h, t)
    pad = (-t) % LANES
    if pad:
        rows = jnp.pad(rows, ((0, 0), (0, pad)))
    return _cumsum_lanes(rows)


def _layer(xp, xs, cache_k, cache_v, cache_logf, state_pool, g_mix_pre, w_in, b_f, w_attn_up, w_pool,
           pool_scale, w_out, g_mix_post, g_ffn_pre, w_ff1, w_ff2, g_ffn_post):
    bp, tp, d = xp.shape
    bs, ts, _ = xs.shape
    past = cache_k.shape[1]
    a, nh, p = ATTN_W, N_HEADS, w_pool.shape[0] * w_pool.shape[1]
    off_f = 3 * a
    off_u = off_f + nh
    off_ga = off_u + p
    off_gb = off_ga + d

    w_qkvu = jnp.concatenate([w_in[:, :off_f], w_in[:, off_u:off_ga]], axis=1).astype(BF16)
    w_f = jnp.pad(w_in[:, off_f:off_u], ((0, 0), (0, LANES - nh))).astype(BF16)
    w_ga = w_in[:, off_ga:off_gb].astype(BF16)
    w_gb = w_in[:, off_gb:].astype(BF16)
    w_up = w_attn_up.astype(BF16)
    w_pl = w_pool.astype(BF16)
    w_o = w_out.astype(BF16)
    w1 = w_ff1.astype(BF16)
    w2 = w_ff2.astype(BF16)
    row = lambda v: v.reshape(1, -1)
    g_pre, b_f2, ps = row(g_mix_pre), row(b_f), row(pool_scale)
    g_post, g_fpre, g_fpost = row(g_mix_post), row(g_ffn_pre), row(g_ffn_post)

    def project(x3, transposed_v):
        x2 = x3.reshape(-1, d)
        return x2, _inproj(x2, g_pre, w_qkvu, w_f, b_f2, tm=_tile(x3.shape[1], INPROJ_TM) if transposed_v
                           else _tile(x2.shape[0], INPROJ_TM), seq=x3.shape[1], transposed_v=transposed_v)

    def finish(x2, h, a_out, u3, hist, hist_map, nseg, tl, pos0, zero_first):
        x1, h2 = _merge(a_out, h, u3, hist, hist_map, x2, w_up, w_ga, w_gb, w_pl, ps, w_o, g_post, g_fpre,
                        nseg=nseg, tl=tl, pos0=pos0, zero_first=zero_first)
        n = x2.shape[0]
        return _ffn(h2, x1, w1, w2, g_fpost, tm=_tile(n, FFN_TM), tf=_tile(w1.shape[1], FFN_TF))

    x2, (h, q, kb, k, v, vt, u, logf) = project(xp, True)
    c_rows = _cum_logf_rows(logf.reshape(bp, tp, nh))
    nblk = nh // HEADS_PER_BLOCK
    three = lambda z: z.reshape(bp, tp, a)
    a_out = _attn_prompt(three(q), three(kb), vt, c_rows.reshape(bp, nblk, HEADS_PER_BLOCK, tp),
                         tq=_tile(tp, ATTN_TQ))
    u3 = u.reshape(bp, tp, p)
    tl = _tile(tp, MERGE_TM)
    blocks_per_tile = tl // HIST_ROWS
    hist_map = lambda s, i: (s, jnp.maximum(i * blocks_per_tile - 1, 0), 0)
    yp = finish(x2, h, a_out.reshape(bp * tp, a), u3, u3, hist_map, 1, tl, 0, True).reshape(bp, tp, d)
    prompt_out = (yp, k.reshape(bp, tp, nh, HEAD_DIM), v.reshape(bp, tp, nh, HEAD_DIM),
                  logf.reshape(bp, tp, nh),
                  jnp.concatenate([jnp.zeros((bp, POOL_HIST, p), F32), u3], axis=1)[:, -POOL_HIST:])

    x2, (h, q, kb, k, v, vb, u, logf) = project(xs, False)
    logf3 = logf.reshape(bs, ts, nh)
    f_all = jnp.concatenate([cache_logf.astype(F32), logf3], axis=1)
    c_all = _cum_logf_rows(f_all).reshape(bs, nh, -1)
    c_cache = c_all[:, :, :past]
    c_new = c_all[:, :, past:past + ts]
    three = lambda z: z.reshape(bs, ts, a)
    a_out = _attn_sample(three(q), cache_k.reshape(bs, past, a), cache_v.reshape(bs, past, a),
                         three(kb), three(vb), c_cache, c_new, jnp.transpose(c_new, (0, 2, 1)))
    u3 = u.reshape(bs, ts, p)
    hist = jnp.pad(state_pool.astype(F32), ((0, 0), (HIST_ROWS - POOL_HIST, 0), (0, 0)))
    ys = finish(x2, h, a_out.reshape(bs * ts, a), u3, hist, lambda s, i: (s, 0, 0), bs, ts, past,
                False).reshape(bs, ts, d)
    sample_out = (ys, k.reshape(bs, ts, nh, HEAD_DIM), v.reshape(bs, ts, nh, HEAD_DIM), logf3,
                  jnp.concatenate([state_pool.astype(F32), u3], axis=1)[:, -POOL_HIST:])
    return prompt_out, sample_out


def kernel(x_prompt, x_sample, cache_k, cache_v, cache_logf, state_pool, g_mix_pre, w_in, b_f, w_attn_up,
           w_pool, pool_scale, w_out, g_mix_post, g_ffn_pre, w_ff1, w_ff2, g_ffn_post):
    depth = w_in.shape[0]
    xp, xs = x_prompt, x_sample
    per_layer = []
    for l in range(depth):
        po, so = _layer(xp, xs, cache_k[l], cache_v[l], cache_logf[l], state_pool[l], g_mix_pre[l], w_in[l],
                        b_f[l], w_attn_up[l], w_pool[l], pool_scale[l], w_out[l], g_mix_post[l],
                        g_ffn_pre[l], w_ff1[l], w_ff2[l], g_ffn_post[l])
        xp, xs = po[0], so[0]
        per_layer.append(po[1:] + so[1:])
    stacked = [jnp.stack(leaves, 0) for leaves in zip(*per_layer)]
    return (xp, xs, *stacked)
```

```python
import functools
import math

import jax
import jax.numpy as jnp
from jax import lax
from jax.experimental import pallas as pl
from jax.experimental.pallas import tpu as pltpu

N_HEADS = 16
HEAD_DIM = 64
ATTN_W = N_HEADS * HEAD_DIM
POOL_WINDOWS = (2, 4, 8, 16)
POOL_GROUPS = len(POOL_WINDOWS)
POOL_HIST = max(POOL_WINDOWS) - 1
EPS = 1e-6
SCALE = HEAD_DIM ** -0.5

LANES = 128
SUBLANES = 8
V7X_VMEM_BYTES = 64 * 1024 * 1024
VMEM_LIMIT_CAP = 60000 * 1024

HEADS_PER_BLOCK = LANES // HEAD_DIM
HIST_ROWS = 16

F32 = jnp.float32
BF16 = jnp.bfloat16


def _vmem_limit(nbytes):
    return int(min(VMEM_LIMIT_CAP, max(32 * 1024 * 1024, nbytes * 5 // 4)))


def _nbytes(shape, dtype):
    return math.prod(shape) * jnp.dtype(dtype).itemsize


def _rms(x):
    return x * lax.rsqrt(jnp.mean(x * x, axis=-1, keepdims=True) + EPS)


def _dot(a, b):
    return jnp.dot(a, b, preferred_element_type=F32)


def _dot_nt(a, b):
    return lax.dot_general(a, b, (((1,), (1,)), ((), ())), preferred_element_type=F32)


def _resident(shape):
    zeros = (0,) * len(shape)
    return pl.BlockSpec(shape, lambda *_: zeros, pipeline_mode=pl.Buffered(1))


def _store_head_major(x, scr_ref, out_ref):
    tm = x.shape[0]
    nh = x.shape[1] // HEAD_DIM
    for c in range(x.shape[1] // LANES):
        blk = x[:, c * LANES:(c + 1) * LANES]
        scr_ref[pl.ds(HEADS_PER_BLOCK * c, tm, stride=nh), :] = blk
        scr_ref[pl.ds(HEADS_PER_BLOCK * c + 1, tm, stride=nh), :] = pltpu.roll(blk, HEAD_DIM, axis=1)
    out_ref[...] = scr_ref[:, 0:HEAD_DIM]


def _inproj_kernel(x_ref, g_ref, w_ref, wf_ref, bf_ref,
                   h_ref, q_ref, kb_ref, k4_ref, v4_ref, vx_ref, u_ref, f_ref, kscr_ref, vscr_ref,
                   *, transposed_v):
    h = (_rms(x_ref[...]) * g_ref[...]).astype(BF16)
    h_ref[...] = h
    a = q_ref.shape[1]
    q_ref[...] = (_dot(h, w_ref[:, 0:a]) * SCALE).astype(BF16)
    k = _dot(h, w_ref[:, a:2 * a])
    kb_ref[...] = k.astype(BF16)
    _store_head_major(k, kscr_ref, k4_ref)
    v = _dot(h, w_ref[:, 2 * a:3 * a])
    _store_head_major(v, vscr_ref, v4_ref)
    if transposed_v:
        vx_ref[0] = v.T.astype(BF16)
    else:
        vx_ref[...] = v.astype(BF16)
    u_ref[...] = _dot(h, w_ref[:, 3 * a:])
    nh = f_ref.shape[1]
    f = _dot(h, wf_ref[...])[:, 0:nh] + bf_ref[...]
    f_ref[...] = jax.nn.log_sigmoid(f)


def _inproj(x, g, w, wf, bf, *, tm, seq, transposed_v):
    n, d = x.shape
    a = ATTN_W
    p = w.shape[1] - 3 * a
    nh = bf.shape[1]
    row = lambda width: pl.BlockSpec((tm, width), lambda i: (i, 0))
    head_major = pl.BlockSpec((tm * N_HEADS, HEAD_DIM), lambda i: (i, 0))
    if transposed_v:
        steps = seq // tm
        vx_shape = jax.ShapeDtypeStruct((n // seq, a, seq), BF16)
        vx_spec = pl.BlockSpec((1, a, tm), lambda i: (i // steps, 0, i % steps))
    else:
        vx_shape = jax.ShapeDtypeStruct((n, a), BF16)
        vx_spec = row(a)
    out_shape = (
        jax.ShapeDtypeStruct((n, d), BF16),
        jax.ShapeDtypeStruct((n, a), BF16),
        jax.ShapeDtypeStruct((n, a), BF16),
        jax.ShapeDtypeStruct((n * N_HEADS, HEAD_DIM), F32),
        jax.ShapeDtypeStruct((n * N_HEADS, HEAD_DIM), F32),
        vx_shape,
        jax.ShapeDtypeStruct((n, p), F32),
        jax.ShapeDtypeStruct((n, nh), F32),
    )
    est = (2 * _nbytes((tm, d), F32) + _nbytes(w.shape, BF16) + _nbytes(wf.shape, BF16)
           + 2 * (_nbytes((tm, d), BF16) + 3 * _nbytes((tm, a), BF16) + 2 * _nbytes((tm * N_HEADS, LANES), F32)
                  + _nbytes((tm, p), F32) + _nbytes((tm, LANES), F32))
           + 2 * _nbytes((tm * N_HEADS, LANES), F32) + 2 * _nbytes((tm, a), F32))
    return pl.pallas_call(
        functools.partial(_inproj_kernel, transposed_v=transposed_v),
        grid=(n // tm,),
        in_specs=[row(d), _resident((1, d)), _resident(w.shape), _resident(wf.shape), _resident((1, nh))],
        out_specs=(row(d), row(a), row(a), head_major, head_major, vx_spec, row(p), row(nh)),
        out_shape=out_shape,
        scratch_shapes=[pltpu.VMEM((tm * N_HEADS, LANES), F32), pltpu.VMEM((tm * N_HEADS, LANES), F32)],
        compiler_params=pltpu.CompilerParams(
            dimension_semantics=("arbitrary",), vmem_limit_bytes=_vmem_limit(est)),
        name="inproj",
    )(x, g, w, wf, bf)


def _cumsum_kernel(x_ref, o_ref):
    x = x_ref[...]
    length = x.shape[1]
    lane = lax.broadcasted_iota(jnp.int32, x.shape, 1)
    shift = 1
    while shift < length:
        x = x + jnp.where(lane >= shift, pltpu.roll(x, shift, axis=1), 0.0)
        shift *= 2
    o_ref[...] = x


def _cumsum_lanes(x):
    rows, length = x.shape
    tr = _tile(rows, CUMSUM_ROWS)
    spec = pl.BlockSpec((tr, length), lambda i: (i, 0))
    return pl.pallas_call(
        _cumsum_kernel,
        grid=(rows // tr,),
        in_specs=[spec],
        out_specs=spec,
        out_shape=jax.ShapeDtypeStruct(x.shape, F32),
        compiler_params=pltpu.CompilerParams(dimension_semantics=("arbitrary",)),
        name="cumsum",
    )(x)


VALUE_ROWS = HEAD_DIM + 16


def _attn_kernel(q_ref, k_ref, vt_ref, c_ref, o_ref,
                 vta_ref, ckrep_ref, s_ref, p_ref, alpha_ref, m_ref, acc_ref, *, tq):
    t = k_ref.shape[1]
    nq = t // tq
    for a in range(HEADS_PER_BLOCK):
        vta_ref[a, 0:HEAD_DIM, :] = vt_ref[0, a * HEAD_DIM:(a + 1) * HEAD_DIM, :]
        vta_ref[a, HEAD_DIM:VALUE_ROWS, :] = jnp.ones((VALUE_ROWS - HEAD_DIM, t), BF16)
        ckrep_ref[a] = jnp.broadcast_to(c_ref[0, 0, a:a + 1, :], (LANES, t)).T

    pairs = [(i, j) for i in range(nq) for j in range(i + 1)]
    rows = lambda j: slice(j * tq, (j + 1) * tq)

    def scores(w):
        i, j = pairs[w]
        q2 = q_ref[0, rows(i), :]
        lane = lax.broadcasted_iota(jnp.int32, q2.shape, 1)
        kt = k_ref[0, rows(j), :]
        for a in range(HEADS_PER_BLOCK):
            qa = jnp.where((lane // HEAD_DIM) == a, q2, jnp.zeros_like(q2))
            ckr = ckrep_ref[a, rows(j), :]
            s_ref[w % 2, a] = _dot_nt(kt, qa) - jnp.concatenate([ckr] * (tq // LANES), axis=1)

    def probs(w):
        i, j = pairs[w]
        for a in range(HEADS_PER_BLOCK):
            s = s_ref[w % 2, a]
            if j == i:
                r = lax.broadcasted_iota(jnp.int32, s.shape, 0)
                c = lax.broadcasted_iota(jnp.int32, s.shape, 1)
                s = jnp.where(r <= c, s, -jnp.inf)
            cqa = c_ref[0, 0, a:a + 1, rows(i)]
            smax = jnp.max(s, axis=0, keepdims=True) + cqa
            if j == 0:
                m_new = smax
            else:
                m_prev = m_ref[a]
                m_new = jnp.maximum(m_prev, smax)
                alpha_ref[w % 2, a] = jnp.exp(m_prev - m_new)
            p_ref[w % 2, a] = jnp.exp(s - (m_new - cqa)).astype(BF16)
            m_ref[a] = m_new

    def values(w):
        i, j = pairs[w]
        for a in range(HEADS_PER_BLOCK):
            pv = _dot(vta_ref[a, :, rows(j)], p_ref[w % 2, a])
            acc_ref[a] = pv if j == 0 else alpha_ref[w % 2, a] * acc_ref[a] + pv
        if j == i:
            ot = jnp.concatenate(
                [acc_ref[a, 0:HEAD_DIM, :] / acc_ref[a, HEAD_DIM:HEAD_DIM + 1, :]
                 for a in range(HEADS_PER_BLOCK)], axis=0)
            o_ref[0, rows(i), :] = ot.T.astype(o_ref.dtype)

    n = len(pairs)
    scores(0)
    for w in range(n):
        if w + 1 < n:
            scores(w + 1)
        probs(w)
        if w >= 1:
            values(w - 1)
    values(n - 1)


def _attn_prompt(q, kb, vt, c_rows, *, tq):
    b, t, a = q.shape
    nblk = a // LANES
    spec = pl.BlockSpec((1, t, LANES), lambda bi, hi: (bi, 0, hi))
    return pl.pallas_call(
        functools.partial(_attn_kernel, tq=tq),
        grid=(b, nblk),
        in_specs=[spec, spec, pl.BlockSpec((1, LANES, t), lambda bi, hi: (bi, hi, 0)),
                  pl.BlockSpec((1, 1, HEADS_PER_BLOCK, t), lambda bi, hi: (bi, hi, 0, 0))],
        out_specs=spec,
        out_shape=jax.ShapeDtypeStruct((b, t, a), BF16),
        scratch_shapes=[pltpu.VMEM((HEADS_PER_BLOCK, VALUE_ROWS, t), BF16),
                        pltpu.VMEM((HEADS_PER_BLOCK, t, LANES), F32),
                        pltpu.VMEM((2, HEADS_PER_BLOCK, tq, tq), F32),
                        pltpu.VMEM((2, HEADS_PER_BLOCK, tq, tq), BF16),
                        pltpu.VMEM((2, HEADS_PER_BLOCK, 1, tq), F32),
                        pltpu.VMEM((HEADS_PER_BLOCK, 1, tq), F32),
                        pltpu.VMEM((HEADS_PER_BLOCK, VALUE_ROWS, tq), F32)],
        compiler_params=pltpu.CompilerParams(dimension_semantics=("arbitrary", "arbitrary")),
        name="attn_prompt",
    )(q, kb, vt, c_rows)


def _cache_chunk_copies(kc_hbm, vc_hbm, kstage_ref, vstage_ref, sem, b, c, slot, chunk):
    rows = pl.ds(c * chunk, chunk)
    copies = []
    for h in range(kstage_ref.shape[1]):
        copies.append(pltpu.make_async_copy(kc_hbm.at[b, rows, h, :], kstage_ref.at[slot, h], sem.at[0, slot]))
        copies.append(pltpu.make_async_copy(vc_hbm.at[b, rows, h, :], vstage_ref.at[slot, h], sem.at[1, slot]))
    return copies


def _stage_to_token_major(stage_ref, slot, dst_ref, c, chunk):
    for blk in range(stage_ref.shape[1] // HEADS_PER_BLOCK):
        pair = jnp.concatenate(
            [stage_ref[slot, HEADS_PER_BLOCK * blk + e] for e in range(HEADS_PER_BLOCK)], axis=1)
        dst_ref[c * chunk:(c + 1) * chunk, blk * LANES:(blk + 1) * LANES] = pair.astype(BF16)


def _attn_sample_kernel(q_ref, kc_hbm, vc_hbm, kn_ref, vn_ref, cc_ref, cnr_ref, cnc_ref, o_ref,
                        kstage_ref, vstage_ref, sem, kb_ref, vb_ref, s_ref, p_ref, pn_ref, *, chunk):
    b = pl.program_id(0)
    n, a = q_ref.shape[1], q_ref.shape[2]
    nh = a // HEAD_DIM
    nchunks = kb_ref.shape[0] // chunk
    copies = functools.partial(_cache_chunk_copies, kc_hbm, vc_hbm, kstage_ref, vstage_ref, sem, b)

    for cp in copies(0, 0, chunk):
        cp.start()
    for c in range(nchunks):
        slot = c % 2
        if c + 1 < nchunks:
            for cp in copies(c + 1, 1 - slot, chunk):
                cp.start()
        for cp in copies(c, slot, chunk):
            cp.wait()
        _stage_to_token_major(kstage_ref, slot, kb_ref, c, chunk)
        _stage_to_token_major(vstage_ref, slot, vb_ref, c, chunk)

    q = q_ref[0]
    qt = jnp.concatenate([q] * nh, axis=0)
    row_h = lax.broadcasted_iota(jnp.int32, qt.shape, 0) // n
    col_h = lax.broadcasted_iota(jnp.int32, qt.shape, 1) // HEAD_DIM
    qbd = jnp.where(row_h == col_h, qt, jnp.zeros_like(qt))
    s_ref[...] = _dot_nt(qbd, kb_ref[...])
    s_new = _dot_nt(qbd, kn_ref[0])
    cnc = cnc_ref[0]
    r = lax.broadcasted_iota(jnp.int32, (n, n), 0)
    c = lax.broadcasted_iota(jnp.int32, (n, n), 1)
    inv_l = []
    for h in range(nh):
        rows = slice(h * n, (h + 1) * n)
        cq = cnc[:, h:h + 1]
        sc = s_ref[rows, :] + (cq - cc_ref[0, h:h + 1, :])
        sn = s_new[rows, :] + (cq - cnr_ref[0, h:h + 1, :])
        sn = jnp.where(c <= r, sn, -jnp.inf)
        m = jnp.maximum(jnp.max(sc, axis=1, keepdims=True), jnp.max(sn, axis=1, keepdims=True))
        pc = jnp.exp(sc - m)
        pn = jnp.exp(sn - m)
        inv_l.append(1.0 / (jnp.sum(pc, axis=1, keepdims=True) + jnp.sum(pn, axis=1, keepdims=True)))
        p_ref[rows, :] = pc.astype(BF16)
        pn_ref[rows, :] = pn.astype(BF16)
    o = _dot(p_ref[...], vb_ref[...]) + _dot(pn_ref[...], vn_ref[0])
    for h in range(nh):
        rows = slice(h * n, (h + 1) * n)
        cols = slice(h * HEAD_DIM, (h + 1) * HEAD_DIM)
        o_ref[0, :, cols] = (o[rows, cols] * inv_l[h]).astype(o_ref.dtype)


def _attn_sample(q, kc, vc, kn, vn, cc, cnr, cnc, *, chunk):
    b, n, a = q.shape
    past, nh = kc.shape[1], kc.shape[2]
    per_b = lambda *tail: pl.BlockSpec((1,) + tail, lambda bi: (bi,) + (0,) * len(tail))
    in_hbm = pl.BlockSpec(memory_space=pl.ANY)
    stage = (2, nh, chunk, HEAD_DIM)
    est = (2 * _nbytes((2, nh, chunk, LANES), F32) + 2 * _nbytes((past, a), BF16)
           + _nbytes((nh * n, past), F32) + _nbytes((nh * n, past), BF16) + 4 * _nbytes((nh * n, a), F32))
    return pl.pallas_call(
        functools.partial(_attn_sample_kernel, chunk=chunk),
        grid=(b,),
        in_specs=[per_b(n, a), in_hbm, in_hbm, per_b(n, a), per_b(n, a),
                  per_b(nh, past), per_b(nh, n), per_b(n, nh)],
        out_specs=per_b(n, a),
        out_shape=jax.ShapeDtypeStruct((b, n, a), BF16),
        scratch_shapes=[pltpu.VMEM(stage, F32), pltpu.VMEM(stage, F32), pltpu.SemaphoreType.DMA((2, 2)),
                        pltpu.VMEM((past, a), BF16), pltpu.VMEM((past, a), BF16),
                        pltpu.VMEM((nh * n, past), F32), pltpu.VMEM((nh * n, past), BF16),
                        pltpu.VMEM((nh * n, n), BF16)],
        compiler_params=pltpu.CompilerParams(
            dimension_semantics=("arbitrary",), vmem_limit_bytes=_vmem_limit(est)),
        name="attn_sample",
    )(q, kc, vc, kn, vn, cc, cnr, cnc)


def _merge_kernel(a_ref, h_ref, u_ref, hist_ref, x_ref, wup_ref, wga_ref, wgb_ref, wpool_ref, ps_ref,
                  wout_ref, gpost_ref, gpre_ref, x1_ref, h2_ref, ext_ref, m_ref,
                  *, pos0, zero_first):
    nseg, tl, _ = u_ref.shape
    i = pl.program_id(1)
    hist = hist_ref[...]
    if zero_first:
        hist = jnp.where(i == 0, 0.0, hist)
    ext_ref[:, 0:HIST_ROWS, :] = hist
    ext_ref[:, HIST_ROWS:HIST_ROWS + tl, :] = u_ref[...]
    gw = u_ref.shape[2] // POOL_GROUPS
    ogw = wpool_ref.shape[2]
    pos = pos0 + i * tl + lax.broadcasted_iota(jnp.int32, (nseg, tl, gw), 1)
    a = a_ref[...]
    h = h_ref[...]
    for g, w in enumerate(POOL_WINDOWS):
        cs = slice(g * gw, (g + 1) * gw)
        cur = ext_ref[:, HIST_ROWS:HIST_ROWS + tl, cs]
        tot = cur
        for s in range(1, w):
            tot = tot + ext_ref[:, HIST_ROWS - s:HIST_ROWS - s + tl, cs]
        cnt = jnp.minimum(pos + 1, w).astype(F32)
        pooled = (tot / cnt - cur).reshape(nseg * tl, gw).astype(BF16)
        os_ = slice(g * ogw, (g + 1) * ogw)
        br_b = _dot(pooled, wpool_ref[g]) * ps_ref[:, os_]
        br_a = _dot(a, wup_ref[:, os_])
        ga = _dot(h, wga_ref[:, os_])
        gb = _dot(h, wgb_ref[:, os_])
        m_ref[:, os_] = (jax.nn.sigmoid(ga) * br_a + jax.nn.sigmoid(gb) * br_b).astype(BF16)
    x1 = x_ref[...] + _rms(_dot(m_ref[...], wout_ref[...])) * gpost_ref[...]
    x1_ref[...] = x1
    h2_ref[...] = (_rms(x1) * gpre_ref[...]).astype(BF16)


def _merge(a, h, u3, hist, hist_map, x, wup, wga, wgb, wpool, ps, wout, gpost, gpre,
           *, nseg, tl, pos0, zero_first):
    s_total, l_total, p = u3.shape
    n, d = x.shape
    tm = nseg * tl
    steps = l_total // tl
    assert nseg == 1 or steps == 1
    row = lambda width: pl.BlockSpec((tm, width), lambda s, i: (s * steps + i, 0))
    est = (_nbytes(wup.shape, BF16) + 2 * _nbytes(wga.shape, BF16) + _nbytes(wpool.shape, BF16)
           + _nbytes(wout.shape, BF16)
           + 2 * (_nbytes((tm, a.shape[1]), BF16) + 2 * _nbytes((tm, d), BF16) + 2 * _nbytes((tm, d), F32)
                  + _nbytes((tm + HIST_ROWS, p), F32))
           + _nbytes((tm + nseg * HIST_ROWS, p), F32) + _nbytes((tm, d), BF16) + 6 * _nbytes((tm, d), F32))
    return pl.pallas_call(
        functools.partial(_merge_kernel, pos0=pos0, zero_first=zero_first),
        grid=(s_total // nseg, steps),
        in_specs=[row(a.shape[1]), row(d),
                  pl.BlockSpec((nseg, tl, p), lambda s, i: (s, i, 0)),
                  pl.BlockSpec((nseg, HIST_ROWS, p), hist_map),
                  row(d),
                  _resident(wup.shape), _resident(wga.shape), _resident(wgb.shape), _resident(wpool.shape),
                  _resident((1, d)), _resident(wout.shape), _resident((1, d)), _resident((1, d))],
        out_specs=(row(d), row(d)),
        out_shape=(jax.ShapeDtypeStruct((n, d), F32), jax.ShapeDtypeStruct((n, d), BF16)),
        scratch_shapes=[pltpu.VMEM((nseg, HIST_ROWS + tl, p), F32), pltpu.VMEM((tm, d), BF16)],
        compiler_params=pltpu.CompilerParams(
            dimension_semantics=("arbitrary", "arbitrary"), vmem_limit_bytes=_vmem_limit(est)),
        name="merge",
    )(a, h, u3, hist, x, wup, wga, wgb, wpool, ps, wout, gpost, gpre)


def _ffn_kernel(h_ref, x1_ref, w1_ref, w2_ref, g_ref, o_ref):
    f = pl.program_id(1)

    @pl.when(f == 0)
    def _():
        o_ref[...] = jnp.zeros_like(o_ref)

    z = jnp.square(jnp.maximum(_dot(h_ref[...], w1_ref[...]), 0.0)).astype(BF16)
    o_ref[...] += _dot(z, w2_ref[...])

    @pl.when(f == pl.num_programs(1) - 1)
    def _():
        o_ref[...] = x1_ref[...] + _rms(o_ref[...]) * g_ref[...]


def _ffn(h2, x1, w1, w2, g, *, tm, tf):
    n, d = x1.shape
    dff = w1.shape[1]
    est = (2 * _nbytes((tm, d), BF16) + 4 * _nbytes((tm, d), F32)
           + 4 * _nbytes((d, tf), BF16) + 2 * _nbytes((tm, tf), F32) + _nbytes((tm, d), F32))
    return pl.pallas_call(
        _ffn_kernel,
        grid=(n // tm, dff // tf),
        in_specs=[pl.BlockSpec((tm, d), lambda i, f: (i, 0)),
                  pl.BlockSpec((tm, d), lambda i, f: (i, 0)),
                  pl.BlockSpec((d, tf), lambda i, f: (0, f)),
                  pl.BlockSpec((tf, d), lambda i, f: (f, 0)),
                  pl.BlockSpec((1, d), lambda i, f: (0, 0))],
        out_specs=pl.BlockSpec((tm, d), lambda i, f: (i, 0)),
        out_shape=jax.ShapeDtypeStruct((n, d), F32),
        compiler_params=pltpu.CompilerParams(
            dimension_semantics=("arbitrary", "arbitrary"), vmem_limit_bytes=_vmem_limit(est)),
        name="ffn",
    )(h2, x1, w1, w2, g)


def _tile(n, pref):
    t = min(n, pref)
    while n % t:
        t //= 2
    return t


INPROJ_TM = 256
ATTN_TQ = 512
MERGE_TM = 256
FFN_TM = 512
FFN_TF = 1024
CACHE_CHUNK = 512
CUMSUM_ROWS = 32


def _cum_logf_rows(logf_bth):
    b, t, h = logf_bth.shape
    rows = jnp.transpose(logf_bth, (0, 2, 1)).reshape(b * h, t)
    pad = (-t) % LANES
    if pad:
        rows = jnp.pad(rows, ((0, 0), (0, pad)))
    return _cumsum_lanes(rows)


def _layer(xp, xs, cache_k, cache_v, cache_logf, state_pool, g_mix_pre, w_in, b_f, w_attn_up, w_pool,
           pool_scale, w_out, g_mix_post, g_ffn_pre, w_ff1, w_ff2, g_ffn_post):
    bp, tp, d = xp.shape
    bs, ts, _ = xs.shape
    past = cache_k.shape[1]
    a, nh, p = ATTN_W, N_HEADS, w_pool.shape[0] * w_pool.shape[1]
    off_f = 3 * a
    off_u = off_f + nh
    off_ga = off_u + p
    off_gb = off_ga + d

    w_qkvu = jnp.concatenate([w_in[:, :off_f], w_in[:, off_u:off_ga]], axis=1).astype(BF16)
    w_f = jnp.pad(w_in[:, off_f:off_u], ((0, 0), (0, LANES - nh))).astype(BF16)
    w_ga = w_in[:, off_ga:off_gb].astype(BF16)
    w_gb = w_in[:, off_gb:].astype(BF16)
    w_up = w_attn_up.astype(BF16)
    w_pl = w_pool.astype(BF16)
    w_o = w_out.astype(BF16)
    w1 = w_ff1.astype(BF16)
    w2 = w_ff2.astype(BF16)
    row = lambda v: v.reshape(1, -1)
    g_pre, b_f2, ps = row(g_mix_pre), row(b_f), row(pool_scale)
    g_post, g_fpre, g_fpost = row(g_mix_post), row(g_ffn_pre), row(g_ffn_post)

    def project(x3, transposed_v):
        x2 = x3.reshape(-1, d)
        return x2, _inproj(x2, g_pre, w_qkvu, w_f, b_f2, tm=_tile(x3.shape[1], INPROJ_TM) if transposed_v
                           else _tile(x2.shape[0], INPROJ_TM), seq=x3.shape[1], transposed_v=transposed_v)

    def finish(x2, h, a_out, u3, hist, hist_map, nseg, tl, pos0, zero_first):
        x1, h2 = _merge(a_out, h, u3, hist, hist_map, x2, w_up, w_ga, w_gb, w_pl, ps, w_o, g_post, g_fpre,
                        nseg=nseg, tl=tl, pos0=pos0, zero_first=zero_first)
        n = x2.shape[0]
        return _ffn(h2, x1, w1, w2, g_fpost, tm=_tile(n, FFN_TM), tf=_tile(w1.shape[1], FFN_TF))

    x2, (h, q, kb, k, v, vt, u, logf) = project(xp, True)
    c_rows = _cum_logf_rows(logf.reshape(bp, tp, nh))
    nblk = nh // HEADS_PER_BLOCK
    three = lambda z: z.reshape(bp, tp, a)
    a_out = _attn_prompt(three(q), three(kb), vt, c_rows.reshape(bp, nblk, HEADS_PER_BLOCK, tp),
                         tq=_tile(tp, ATTN_TQ))
    u3 = u.reshape(bp, tp, p)
    tl = _tile(tp, MERGE_TM)
    blocks_per_tile = tl // HIST_ROWS
    hist_map = lambda s, i: (s, jnp.maximum(i * blocks_per_tile - 1, 0), 0)
    yp = finish(x2, h, a_out.reshape(bp * tp, a), u3, u3, hist_map, 1, tl, 0, True).reshape(bp, tp, d)
    prompt_out = (yp, k.reshape(bp, tp, nh, HEAD_DIM), v.reshape(bp, tp, nh, HEAD_DIM),
                  logf.reshape(bp, tp, nh),
                  jnp.concatenate([jnp.zeros((bp, POOL_HIST, p), F32), u3], axis=1)[:, -POOL_HIST:])

    x2, (h, q, kb, k, v, vb, u, logf) = project(xs, False)
    logf3 = logf.reshape(bs, ts, nh)
    f_all = jnp.concatenate([cache_logf.astype(F32), logf3], axis=1)
    c_all = _cum_logf_rows(f_all).reshape(bs, nh, -1)
    c_cache = c_all[:, :, :past]
    c_new = c_all[:, :, past:past + ts]
    three = lambda z: z.reshape(bs, ts, a)
    a_out = _attn_sample(three(q), cache_k.astype(F32), cache_v.astype(F32), three(kb), three(vb),
                         c_cache, c_new, jnp.transpose(c_new, (0, 2, 1)), chunk=_tile(past, CACHE_CHUNK))
    u3 = u.reshape(bs, ts, p)
    hist = jnp.pad(state_pool.astype(F32), ((0, 0), (HIST_ROWS - POOL_HIST, 0), (0, 0)))
    ys = finish(x2, h, a_out.reshape(bs * ts, a), u3, hist, lambda s, i: (s, 0, 0), bs, ts, past,
                False).reshape(bs, ts, d)
    sample_out = (ys, k.reshape(bs, ts, nh, HEAD_DIM), v.reshape(bs, ts, nh, HEAD_DIM), logf3,
                  jnp.concatenate([state_pool.astype(F32), u3], axis=1)[:, -POOL_HIST:])
    return prompt_out, sample_out


def kernel(x_prompt, x_sample, cache_k, cache_v, cache_logf, state_pool, g_mix_pre, w_in, b_f, w_attn_up,
           w_pool, pool_scale, w_out, g_mix_post, g_ffn_pre, w_ff1, w_ff2, g_ffn_post):
    depth = w_in.shape[0]
    xp, xs = x_prompt, x_sample
    per_layer = []
    for l in range(depth):
        po, so = _layer(xp, xs, cache_k[l], cache_v[l], cache_logf[l], state_pool[l], g_mix_pre[l], w_in[l],
                        b_f[l], w_attn_up[l], w_pool[l], pool_scale[l], w_out[l], g_mix_post[l],
                        g_ffn_pre[l], w_ff1[l], w_ff2[l], g_ffn_post[l])
        xp, xs = po[0], so[0]
        per_layer.append(po[1:] + so[1:])
    stacked = [jnp.stack(leaves, 0) for leaves in zip(*per_layer)]
    return (xp, xs, *stacked)
```

```python
import functools
import math

import jax
import jax.numpy as jnp
from jax import lax
from jax.experimental import pallas as pl
from jax.experimental.pallas import tpu as pltpu

N_HEADS = 16
HEAD_DIM = 64
ATTN_W = N_HEADS * HEAD_DIM
POOL_WINDOWS = (2, 4, 8, 16)
POOL_GROUPS = len(POOL_WINDOWS)
POOL_HIST = max(POOL_WINDOWS) - 1
EPS = 1e-6
SCALE = HEAD_DIM ** -0.5

LANES = 128
SUBLANES = 8
V7X_VMEM_BYTES = 64 * 1024 * 1024
VMEM_LIMIT_CAP = 60000 * 1024

HEADS_PER_BLOCK = LANES // HEAD_DIM
HIST_ROWS = 16

F32 = jnp.float32
BF16 = jnp.bfloat16


def _vmem_limit(nbytes):
    return int(min(VMEM_LIMIT_CAP, max(32 * 1024 * 1024, nbytes * 5 // 4)))


def _nbytes(shape, dtype):
    return math.prod(shape) * jnp.dtype(dtype).itemsize


def _rms(x):
    return x * lax.rsqrt(jnp.mean(x * x, axis=-1, keepdims=True) + EPS)


def _dot(a, b):
    return jnp.dot(a, b, preferred_element_type=F32)


def _dot_nt(a, b):
    return lax.dot_general(a, b, (((1,), (1,)), ((), ())), preferred_element_type=F32)


def _resident(shape):
    zeros = (0,) * len(shape)
    return pl.BlockSpec(shape, lambda *_: zeros, pipeline_mode=pl.Buffered(1))


def _inproj_kernel(x_ref, g_ref, w_ref, wf_ref, bf_ref,
                   h_ref, q_ref, kb_ref, k_ref, v_ref, vb_ref, u_ref, f_ref, *, time_minor):
    h = (_rms(x_ref[...]) * g_ref[...]).astype(BF16)
    h_ref[...] = h
    a = q_ref.shape[1]
    q_ref[...] = (_dot(h, w_ref[:, 0:a]) * SCALE).astype(BF16)
    k = _dot(h, w_ref[:, a:2 * a])
    kb_ref[...] = k.astype(BF16)
    v = _dot(h, w_ref[:, 2 * a:3 * a])
    vb_ref[...] = v.astype(BF16)
    u_ref[...] = _dot(h, w_ref[:, 3 * a:])
    f = _dot(h, wf_ref[...]) + bf_ref[...]
    if time_minor:
        k_ref[0] = k.T
        v_ref[0] = v.T
        f_ref[0] = jax.nn.log_sigmoid(f.T[0:f_ref.shape[1], :])
    else:
        k_ref[...] = k
        v_ref[...] = v
        f_ref[...] = jax.nn.log_sigmoid(f[:, 0:f_ref.shape[1]])


def _inproj(x, g, w, wf, bf, *, tm, seq, time_minor):
    n, d = x.shape
    a = ATTN_W
    p = w.shape[1] - 3 * a
    nh = N_HEADS
    row = lambda width: pl.BlockSpec((tm, width), lambda i: (i, 0))
    if time_minor:
        steps = seq // tm
        feat = lambda rows: (jax.ShapeDtypeStruct((n // seq, rows, seq), F32),
                             pl.BlockSpec((1, rows, tm), lambda i: (i // steps, 0, i % steps)))
    else:
        feat = lambda rows: (jax.ShapeDtypeStruct((n, rows), F32), row(rows))
    (k_shape, k_spec), (f_shape, f_spec) = feat(a), feat(nh)
    out_shape = (
        jax.ShapeDtypeStruct((n, d), BF16),
        jax.ShapeDtypeStruct((n, a), BF16),
        jax.ShapeDtypeStruct((n, a), BF16),
        k_shape,
        k_shape,
        jax.ShapeDtypeStruct((n, a), BF16),
        jax.ShapeDtypeStruct((n, p), F32),
        f_shape,
    )
    est = (2 * _nbytes((tm, d), F32) + _nbytes(w.shape, BF16) + _nbytes(wf.shape, BF16)
           + 2 * (_nbytes((tm, d), BF16) + 3 * _nbytes((tm, a), BF16) + 2 * _nbytes((tm, a), F32)
                  + _nbytes((tm, p), F32) + _nbytes((tm, LANES), F32))
           + 4 * _nbytes((tm, a), F32))
    return pl.pallas_call(
        functools.partial(_inproj_kernel, time_minor=time_minor),
        grid=(n // tm,),
        in_specs=[row(d), _resident((1, d)), _resident(w.shape), _resident(wf.shape), _resident((1, LANES))],
        out_specs=(row(d), row(a), row(a), k_spec, k_spec, row(a), row(p), f_spec),
        out_shape=out_shape,
        compiler_params=pltpu.CompilerParams(
            dimension_semantics=("arbitrary",), vmem_limit_bytes=_vmem_limit(est)),
        name="inproj",
    )(x, g, w, wf, bf)


def _cumsum_kernel(x_ref, o_ref):
    x = x_ref[...]
    length = x.shape[1]
    lane = lax.broadcasted_iota(jnp.int32, x.shape, 1)
    shift = 1
    while shift < length:
        x = x + jnp.where(lane >= shift, pltpu.roll(x, shift, axis=1), 0.0)
        shift *= 2
    o_ref[...] = x


def _cumsum_lanes(x):
    rows, length = x.shape
    tr = _tile(rows, CUMSUM_ROWS)
    spec = pl.BlockSpec((tr, length), lambda i: (i, 0))
    return pl.pallas_call(
        _cumsum_kernel,
        grid=(rows // tr,),
        in_specs=[spec],
        out_specs=spec,
        out_shape=jax.ShapeDtypeStruct(x.shape, F32),
        compiler_params=pltpu.CompilerParams(dimension_semantics=("arbitrary",)),
        name="cumsum",
    )(x)


VALUE_ROWS = HEAD_DIM + 16


def _attn_kernel(q_ref, k_ref, vt_ref, c_ref, o_ref,
                 vta_ref, ckrep_ref, s_ref, p_ref, alpha_ref, m_ref, acc_ref, *, tq):
    t = k_ref.shape[1]
    nq = t // tq
    for a in range(HEADS_PER_BLOCK):
        vta_ref[a, 0:HEAD_DIM, :] = vt_ref[0, a * HEAD_DIM:(a + 1) * HEAD_DIM, :].astype(BF16)
        vta_ref[a, HEAD_DIM:VALUE_ROWS, :] = jnp.ones((VALUE_ROWS - HEAD_DIM, t), BF16)
        ckrep_ref[a] = jnp.broadcast_to(c_ref[0, 0, a:a + 1, :], (LANES, t)).T

    pairs = [(i, j) for i in range(nq) for j in range(i + 1)]
    rows = lambda j: slice(j * tq, (j + 1) * tq)

    def scores(w):
        i, j = pairs[w]
        q2 = q_ref[0, rows(i), :]
        lane = lax.broadcasted_iota(jnp.int32, q2.shape, 1)
        kt = k_ref[0, rows(j), :]
        for a in range(HEADS_PER_BLOCK):
            qa = jnp.where((lane // HEAD_DIM) == a, q2, jnp.zeros_like(q2))
            ckr = ckrep_ref[a, rows(j), :]
            s_ref[w % 2, a] = _dot_nt(kt, qa) - jnp.concatenate([ckr] * (tq // LANES), axis=1)

    def probs(w):
        i, j = pairs[w]
        for a in range(HEADS_PER_BLOCK):
            s = s_ref[w % 2, a]
            if j == i:
                r = lax.broadcasted_iota(jnp.int32, s.shape, 0)
                c = lax.broadcasted_iota(jnp.int32, s.shape, 1)
                s = jnp.where(r <= c, s, -jnp.inf)
            cqa = c_ref[0, 0, a:a + 1, rows(i)]
            smax = jnp.max(s, axis=0, keepdims=True) + cqa
            if j == 0:
                m_new = smax
            else:
                m_prev = m_ref[a]
                m_new = jnp.maximum(m_prev, smax)
                alpha_ref[w % 2, a] = jnp.exp(m_prev - m_new)
            p_ref[w % 2, a] = jnp.exp(s - (m_new - cqa)).astype(BF16)
            m_ref[a] = m_new

    def values(w):
        i, j = pairs[w]
        for a in range(HEADS_PER_BLOCK):
            pv = _dot(vta_ref[a, :, rows(j)], p_ref[w % 2, a])
            acc_ref[a] = pv if j == 0 else alpha_ref[w % 2, a] * acc_ref[a] + pv
        if j == i:
            ot = jnp.concatenate(
                [acc_ref[a, 0:HEAD_DIM, :] / acc_ref[a, HEAD_DIM:HEAD_DIM + 1, :]
                 for a in range(HEADS_PER_BLOCK)], axis=0)
            o_ref[0, rows(i), :] = ot.T.astype(o_ref.dtype)

    n = len(pairs)
    scores(0)
    for w in range(n):
        if w + 1 < n:
            scores(w + 1)
        probs(w)
        if w >= 1:
            values(w - 1)
    values(n - 1)


def _attn_prompt(q, kb, vt, c_rows, *, tq):
    b, t, a = q.shape
    nblk = a // LANES
    spec = pl.BlockSpec((1, t, LANES), lambda bi, hi: (bi, 0, hi))
    return pl.pallas_call(
        functools.partial(_attn_kernel, tq=tq),
        grid=(b, nblk),
        in_specs=[spec, spec, pl.BlockSpec((1, LANES, t), lambda bi, hi: (bi, hi, 0)),
                  pl.BlockSpec((1, 1, HEADS_PER_BLOCK, t), lambda bi, hi: (bi, hi, 0, 0))],
        out_specs=spec,
        out_shape=jax.ShapeDtypeStruct((b, t, a), BF16),
        scratch_shapes=[pltpu.VMEM((HEADS_PER_BLOCK, VALUE_ROWS, t), BF16),
                        pltpu.VMEM((HEADS_PER_BLOCK, t, LANES), F32),
                        pltpu.VMEM((2, HEADS_PER_BLOCK, tq, tq), F32),
                        pltpu.VMEM((2, HEADS_PER_BLOCK, tq, tq), BF16),
                        pltpu.VMEM((2, HEADS_PER_BLOCK, 1, tq), F32),
                        pltpu.VMEM((HEADS_PER_BLOCK, 1, tq), F32),
                        pltpu.VMEM((HEADS_PER_BLOCK, VALUE_ROWS, tq), F32)],
        compiler_params=pltpu.CompilerParams(dimension_semantics=("arbitrary", "arbitrary")),
        name="attn_prompt",
    )(q, kb, vt, c_rows)


def _attn_sample_kernel(q_ref, kct_ref, vct_ref, kn_ref, vn_ref, cc_ref, cnr_ref, cnc_ref, o_ref,
                        kb_ref, vb_ref, s_ref, p_ref, pn_ref):
    n, a = q_ref.shape[1], q_ref.shape[2]
    nh = a // HEAD_DIM
    q = q_ref[0]
    qt = jnp.concatenate([q] * nh, axis=0)
    row_h = lax.broadcasted_iota(jnp.int32, qt.shape, 0) // n
    col_h = lax.broadcasted_iota(jnp.int32, qt.shape, 1) // HEAD_DIM
    qbd = jnp.where(row_h == col_h, qt, jnp.zeros_like(qt))
    kb_ref[...] = kct_ref[0].astype(BF16)
    vb_ref[...] = vct_ref[0].astype(BF16)
    s_ref[...] = _dot(qbd, kb_ref[...])
    s_new = _dot_nt(qbd, kn_ref[0])
    cnc = cnc_ref[0]
    r = lax.broadcasted_iota(jnp.int32, (n, n), 0)
    c = lax.broadcasted_iota(jnp.int32, (n, n), 1)
    inv_l = []
    for h in range(nh):
        rows = slice(h * n, (h + 1) * n)
        cq = cnc[:, h:h + 1]
        sc = s_ref[rows, :] + (cq - cc_ref[0, h:h + 1, :])
        sn = s_new[rows, :] + (cq - cnr_ref[0, h:h + 1, :])
        sn = jnp.where(c <= r, sn, -jnp.inf)
        m = jnp.maximum(jnp.max(sc, axis=1, keepdims=True), jnp.max(sn, axis=1, keepdims=True))
        pc = jnp.exp(sc - m)
        pn = jnp.exp(sn - m)
        inv_l.append(1.0 / (jnp.sum(pc, axis=1, keepdims=True) + jnp.sum(pn, axis=1, keepdims=True)))
        p_ref[rows, :] = pc.astype(BF16)
        pn_ref[rows, :] = pn.astype(BF16)
    o = _dot_nt(p_ref[...], vb_ref[...]) + _dot(pn_ref[...], vn_ref[0])
    for h in range(nh):
        rows = slice(h * n, (h + 1) * n)
        cols = slice(h * HEAD_DIM, (h + 1) * HEAD_DIM)
        o_ref[0, :, cols] = (o[rows, cols] * inv_l[h]).astype(o_ref.dtype)


def _attn_sample(q, kct, vct, kn, vn, cc, cnr, cnc):
    b, n, a = q.shape
    past = kct.shape[2]
    nh = a // HEAD_DIM
    per_b = lambda *tail: pl.BlockSpec((1,) + tail, lambda bi: (bi,) + (0,) * len(tail))
    est = (4 * _nbytes((a, past), F32) + 2 * _nbytes((a, past), BF16)
           + _nbytes((nh * n, past), F32) + _nbytes((nh * n, past), BF16) + 4 * _nbytes((nh * n, a), F32))
    return pl.pallas_call(
        _attn_sample_kernel,
        grid=(b,),
        in_specs=[per_b(n, a), per_b(a, past), per_b(a, past), per_b(n, a), per_b(n, a),
                  per_b(nh, past), per_b(nh, n), per_b(n, nh)],
        out_specs=per_b(n, a),
        out_shape=jax.ShapeDtypeStruct((b, n, a), BF16),
        scratch_shapes=[pltpu.VMEM((a, past), BF16), pltpu.VMEM((a, past), BF16),
                        pltpu.VMEM((nh * n, past), F32), pltpu.VMEM((nh * n, past), BF16),
                        pltpu.VMEM((nh * n, n), BF16)],
        compiler_params=pltpu.CompilerParams(
            dimension_semantics=("arbitrary",), vmem_limit_bytes=_vmem_limit(est)),
        name="attn_sample",
    )(q, kct, vct, kn, vn, cc, cnr, cnc)


def _merge_kernel(a_ref, h_ref, u_ref, hist_ref, x_ref, wup_ref, wga_ref, wgb_ref, wpool_ref, ps_ref,
                  wout_ref, gpost_ref, gpre_ref, x1_ref, h2_ref, ext_ref, m_ref,
                  *, pos0, zero_first):
    nseg, tl, _ = u_ref.shape
    i = pl.program_id(1)
    hist = hist_ref[...]
    if zero_first:
        hist = jnp.where(i == 0, 0.0, hist)
    ext_ref[:, 0:HIST_ROWS, :] = hist
    ext_ref[:, HIST_ROWS:HIST_ROWS + tl, :] = u_ref[...]
    gw = u_ref.shape[2] // POOL_GROUPS
    ogw = wpool_ref.shape[2]
    pos = pos0 + i * tl + lax.broadcasted_iota(jnp.int32, (nseg, tl, gw), 1)
    a = a_ref[...]
    h = h_ref[...]
    for g, w in enumerate(POOL_WINDOWS):
        cs = slice(g * gw, (g + 1) * gw)
        cur = ext_ref[:, HIST_ROWS:HIST_ROWS + tl, cs]
        tot = cur
        for s in range(1, w):
            tot = tot + ext_ref[:, HIST_ROWS - s:HIST_ROWS - s + tl, cs]
        cnt = jnp.minimum(pos + 1, w).astype(F32)
        pooled = (tot / cnt - cur).reshape(nseg * tl, gw).astype(BF16)
        os_ = slice(g * ogw, (g + 1) * ogw)
        br_b = _dot(pooled, wpool_ref[g]) * ps_ref[:, os_]
        br_a = _dot(a, wup_ref[:, os_])
        ga = _dot(h, wga_ref[:, os_])
        gb = _dot(h, wgb_ref[:, os_])
        m_ref[:, os_] = (jax.nn.sigmoid(ga) * br_a + jax.nn.sigmoid(gb) * br_b).astype(BF16)
    x1 = x_ref[...] + _rms(_dot(m_ref[...], wout_ref[...])) * gpost_ref[...]
    x1_ref[...] = x1
    h2_ref[...] = (_rms(x1) * gpre_ref[...]).astype(BF16)


def _merge(a, h, u3, hist, hist_map, x, wup, wga, wgb, wpool, ps, wout, gpost, gpre,
           *, nseg, tl, pos0, zero_first):
    s_total, l_total, p = u3.shape
    n, d = x.shape
    tm = nseg * tl
    steps = l_total // tl
    assert nseg == 1 or steps == 1
    row = lambda width: pl.BlockSpec((tm, width), lambda s, i: (s * steps + i, 0))
    est = (_nbytes(wup.shape, BF16) + 2 * _nbytes(wga.shape, BF16) + _nbytes(wpool.shape, BF16)
           + _nbytes(wout.shape, BF16)
           + 2 * (_nbytes((tm, a.shape[1]), BF16) + 2 * _nbytes((tm, d), BF16) + 2 * _nbytes((tm, d), F32)
                  + _nbytes((tm + HIST_ROWS, p), F32))
           + _nbytes((tm + nseg * HIST_ROWS, p), F32) + _nbytes((tm, d), BF16) + 6 * _nbytes((tm, d), F32))
    return pl.pallas_call(
        functools.partial(_merge_kernel, pos0=pos0, zero_first=zero_first),
        grid=(s_total // nseg, steps),
        in_specs=[row(a.shape[1]), row(d),
                  pl.BlockSpec((nseg, tl, p), lambda s, i: (s, i, 0)),
                  pl.BlockSpec((nseg, HIST_ROWS, p), hist_map),
                  row(d),
                  _resident(wup.shape), _resident(wga.shape), _resident(wgb.shape), _resident(wpool.shape),
                  _resident((1, d)), _resident(wout.shape), _resident((1, d)), _resident((1, d))],
        out_specs=(row(d), row(d)),
        out_shape=(jax.ShapeDtypeStruct((n, d), F32), jax.ShapeDtypeStruct((n, d), BF16)),
        scratch_shapes=[pltpu.VMEM((nseg, HIST_ROWS + tl, p), F32), pltpu.VMEM((tm, d), BF16)],
        compiler_params=pltpu.CompilerParams(
            dimension_semantics=("arbitrary", "arbitrary"), vmem_limit_bytes=_vmem_limit(est)),
        name="merge",
    )(a, h, u3, hist, x, wup, wga, wgb, wpool, ps, wout, gpost, gpre)


def _ffn_kernel(h_ref, x1_ref, w1_ref, w2_ref, g_ref, o_ref):
    f = pl.program_id(1)

    @pl.when(f == 0)
    def _():
        o_ref[...] = jnp.zeros_like(o_ref)

    z = jnp.square(jnp.maximum(_dot(h_ref[...], w1_ref[...]), 0.0)).astype(BF16)
    o_ref[...] += _dot(z, w2_ref[...])

    @pl.when(f == pl.num_programs(1) - 1)
    def _():
        o_ref[...] = x1_ref[...] + _rms(o_ref[...]) * g_ref[...]


def _ffn(h2, x1, w1, w2, g, *, tm, tf):
    n, d = x1.shape
    dff = w1.shape[1]
    est = (2 * _nbytes((tm, d), BF16) + 4 * _nbytes((tm, d), F32)
           + 4 * _nbytes((d, tf), BF16) + 2 * _nbytes((tm, tf), F32) + _nbytes((tm, d), F32))
    return pl.pallas_call(
        _ffn_kernel,
        grid=(n // tm, dff // tf),
        in_specs=[pl.BlockSpec((tm, d), lambda i, f: (i, 0)),
                  pl.BlockSpec((tm, d), lambda i, f: (i, 0)),
                  pl.BlockSpec((d, tf), lambda i, f: (0, f)),
                  pl.BlockSpec((tf, d), lambda i, f: (f, 0)),
                  pl.BlockSpec((1, d), lambda i, f: (0, 0))],
        out_specs=pl.BlockSpec((tm, d), lambda i, f: (i, 0)),
        out_shape=jax.ShapeDtypeStruct((n, d), F32),
        compiler_params=pltpu.CompilerParams(
            dimension_semantics=("arbitrary", "arbitrary"), vmem_limit_bytes=_vmem_limit(est)),
        name="ffn",
    )(h2, x1, w1, w2, g)


def _tile(n, pref):
    t = min(n, pref)
    while n % t:
        t //= 2
    return t


INPROJ_TM = 256
ATTN_TQ = 512
MERGE_TM = 256
FFN_TM = 512
FFN_TF = 1024
CUMSUM_ROWS = 32


def _cum_logf_rows(logf_bht):
    b, h, t = logf_bht.shape
    rows = logf_bht.reshape(b * h, t)
    pad = (-t) % LANES
    if pad:
        rows = jnp.pad(rows, ((0, 0), (0, pad)))
    return _cumsum_lanes(rows)


def _time_minor(x, nbatch):
    return jnp.transpose(x, (0, 2, 3, 1)).reshape(nbatch, -1, x.shape[1])


def _layer(xp, xs, cache_k, cache_v, cache_logf, state_pool, g_mix_pre, w_in, b_f, w_attn_up, w_pool,
           pool_scale, w_out, g_mix_post, g_ffn_pre, w_ff1, w_ff2, g_ffn_post):
    bp, tp, d = xp.shape
    bs, ts, _ = xs.shape
    past = cache_k.shape[1]
    a, nh, p = ATTN_W, N_HEADS, w_pool.shape[0] * w_pool.shape[1]
    off_f = 3 * a
    off_u = off_f + nh
    off_ga = off_u + p
    off_gb = off_ga + d

    w_qkvu = jnp.concatenate([w_in[:, :off_f], w_in[:, off_u:off_ga]], axis=1).astype(BF16)
    w_f = jnp.pad(w_in[:, off_f:off_u], ((0, 0), (0, LANES - nh))).astype(BF16)
    b_f2 = jnp.pad(b_f.reshape(1, -1), ((0, 0), (0, LANES - nh)))
    w_ga = w_in[:, off_ga:off_gb].astype(BF16)
    w_gb = w_in[:, off_gb:].astype(BF16)
    w_up = w_attn_up.astype(BF16)
    w_pl = w_pool.astype(BF16)
    w_o = w_out.astype(BF16)
    w1 = w_ff1.astype(BF16)
    w2 = w_ff2.astype(BF16)
    row = lambda v: v.reshape(1, -1)
    g_pre, ps = row(g_mix_pre), row(pool_scale)
    g_post, g_fpre, g_fpost = row(g_mix_post), row(g_ffn_pre), row(g_ffn_post)

    def project(x3, time_minor):
        x2 = x3.reshape(-1, d)
        tm = _tile(x3.shape[1] if time_minor else x2.shape[0], INPROJ_TM)
        return x2, _inproj(x2, g_pre, w_qkvu, w_f, b_f2, tm=tm, seq=x3.shape[1], time_minor=time_minor)

    def finish(x2, h, a_out, u3, hist, hist_map, nseg, tl, pos0, zero_first):
        x1, h2 = _merge(a_out, h, u3, hist, hist_map, x2, w_up, w_ga, w_gb, w_pl, ps, w_o, g_post, g_fpre,
                        nseg=nseg, tl=tl, pos0=pos0, zero_first=zero_first)
        n = x2.shape[0]
        return _ffn(h2, x1, w1, w2, g_fpost, tm=_tile(n, FFN_TM), tf=_tile(w1.shape[1], FFN_TF))

    x2, (h, q, kb, kt, vt, _, u, logft) = project(xp, True)
    c_rows = _cum_logf_rows(logft)
    nblk = nh // HEADS_PER_BLOCK
    three = lambda z: z.reshape(bp, tp, a)
    a_out = _attn_prompt(three(q), three(kb), vt, c_rows.reshape(bp, nblk, HEADS_PER_BLOCK, tp),
                         tq=_tile(tp, ATTN_TQ))
    to_bthd = lambda zt: jnp.transpose(zt.reshape(bp, nh, HEAD_DIM, tp), (0, 3, 1, 2))
    u3 = u.reshape(bp, tp, p)
    tl = _tile(tp, MERGE_TM)
    blocks_per_tile = tl // HIST_ROWS
    hist_map = lambda s, i: (s, jnp.maximum(i * blocks_per_tile - 1, 0), 0)
    yp = finish(x2, h, a_out.reshape(bp * tp, a), u3, u3, hist_map, 1, tl, 0, True).reshape(bp, tp, d)
    prompt_out = (yp, to_bthd(kt), to_bthd(vt), jnp.transpose(logft, (0, 2, 1)),
                  jnp.concatenate([jnp.zeros((bp, POOL_HIST, p), F32), u3], axis=1)[:, -POOL_HIST:])

    x2, (h, q, kb, k, v, vb, u, logf) = project(xs, False)
    logf3 = logf.reshape(bs, ts, nh)
    f_all = jnp.concatenate([cache_logf.astype(F32), logf3], axis=1)
    c_all = _cum_logf_rows(jnp.transpose(f_all, (0, 2, 1))).reshape(bs, nh, -1)
    c_cache = c_all[:, :, :past]
    c_new = c_all[:, :, past:past + ts]
    three = lambda z: z.reshape(bs, ts, a)
    a_out = _attn_sample(three(q), _time_minor(cache_k.astype(F32), bs), _time_minor(cache_v.astype(F32), bs),
                         three(kb), three(vb), c_cache, c_new, jnp.transpose(c_new, (0, 2, 1)))
    u3 = u.reshape(bs, ts, p)
    hist = jnp.pad(state_pool.astype(F32), ((0, 0), (HIST_ROWS - POOL_HIST, 0), (0, 0)))
    ys = finish(x2, h, a_out.reshape(bs * ts, a), u3, hist, lambda s, i: (s, 0, 0), bs, ts, past,
                False).reshape(bs, ts, d)
    sample_out = (ys, k.reshape(bs, ts, nh, HEAD_DIM), v.reshape(bs, ts, nh, HEAD_DIM), logf3,
                  jnp.concatenate([state_pool.astype(F32), u3], axis=1)[:, -POOL_HIST:])
    return prompt_out, sample_out


def kernel(x_prompt, x_sample, cache_k, cache_v, cache_logf, state_pool, g_mix_pre, w_in, b_f, w_attn_up,
           w_pool, pool_scale, w_out, g_mix_post, g_ffn_pre, w_ff1, w_ff2, g_ffn_post):
    depth = w_in.shape[0]
    xp, xs = x_prompt, x_sample
    per_layer = []
    for l in range(depth):
        po, so = _layer(xp, xs, cache_k[l], cache_v[l], cache_logf[l], state_pool[l], g_mix_pre[l], w_in[l],
                        b_f[l], w_attn_up[l], w_pool[l], pool_scale[l], w_out[l], g_mix_post[l],
                        g_ffn_pre[l], w_ff1[l], w_ff2[l], g_ffn_post[l])
        xp, xs = po[0], so[0]
        per_layer.append(po[1:] + so[1:])
    stacked = [jnp.stack(leaves, 0) for leaves in zip(*per_layer)]
    return (xp, xs, *stacked)
```

```python
import functools
import math

import jax
import jax.numpy as jnp
from jax import lax
from jax.experimental import pallas as pl
from jax.experimental.pallas import tpu as pltpu

N_HEADS = 16
HEAD_DIM = 64
ATTN_W = N_HEADS * HEAD_DIM
POOL_WINDOWS = (2, 4, 8, 16)
POOL_GROUPS = len(POOL_WINDOWS)
POOL_HIST = max(POOL_WINDOWS) - 1
EPS = 1e-6
SCALE = HEAD_DIM ** -0.5

LANES = 128
SUBLANES = 8
V7X_VMEM_BYTES = 64 * 1024 * 1024
VMEM_LIMIT_CAP = 60000 * 1024

HEADS_PER_BLOCK = LANES // HEAD_DIM
HIST_ROWS = 16

F32 = jnp.float32
BF16 = jnp.bfloat16


def _vmem_limit(nbytes):
    return int(min(VMEM_LIMIT_CAP, max(32 * 1024 * 1024, nbytes * 5 // 4)))


def _nbytes(shape, dtype):
    return math.prod(shape) * jnp.dtype(dtype).itemsize


def _rms(x):
    return x * lax.rsqrt(jnp.mean(x * x, axis=-1, keepdims=True) + EPS)


def _dot(a, b):
    return jnp.dot(a, b, preferred_element_type=F32)


def _dot_nt(a, b):
    return lax.dot_general(a, b, (((1,), (1,)), ((), ())), preferred_element_type=F32)


def _resident(shape):
    zeros = (0,) * len(shape)
    return pl.BlockSpec(shape, lambda *_: zeros, pipeline_mode=pl.Buffered(1))


def _repack_kernel(w_ref, qkvu_ref, f_ref, ga_ref, gb_ref, *, off_f, off_u, off_ga, off_gb):
    nh = off_u - off_f
    p = off_ga - off_u
    d = off_gb - off_ga
    qkvu_ref[:, 0:off_f] = w_ref[:, 0:off_f].astype(BF16)
    qkvu_ref[:, off_f:off_f + p] = w_ref[:, off_u:off_ga].astype(BF16)
    fwin = w_ref[:, off_f:off_f + LANES]
    lane = lax.broadcasted_iota(jnp.int32, fwin.shape, 1)
    f_ref[...] = jnp.where(lane < nh, fwin, 0.0).astype(BF16)
    ga_ref[...] = w_ref[:, off_ga:off_gb].astype(BF16)
    gb_ref[...] = w_ref[:, off_gb:off_gb + d].astype(BF16)


def _repack_w_in(w, *, off_f, off_u, off_ga, off_gb, tr):
    rows, width = w.shape
    p = off_ga - off_u
    d = off_gb - off_ga
    blk = lambda cols: pl.BlockSpec((tr, cols), lambda i: (i, 0))
    est = 2 * _nbytes((tr, width), F32) + 2 * _nbytes((tr, off_f + p + LANES + 2 * d), BF16)
    return pl.pallas_call(
        functools.partial(_repack_kernel, off_f=off_f, off_u=off_u, off_ga=off_ga, off_gb=off_gb),
        grid=(rows // tr,),
        in_specs=[blk(width)],
        out_specs=(blk(off_f + p), blk(LANES), blk(d), blk(d)),
        out_shape=(jax.ShapeDtypeStruct((rows, off_f + p), BF16), jax.ShapeDtypeStruct((rows, LANES), BF16),
                   jax.ShapeDtypeStruct((rows, d), BF16), jax.ShapeDtypeStruct((rows, d), BF16)),
        compiler_params=pltpu.CompilerParams(
            dimension_semantics=("arbitrary",), vmem_limit_bytes=_vmem_limit(est)),
        name="repack_w_in",
    )(w)


def _inproj_kernel(x_ref, g_ref, w_ref, wf_ref, bf_ref,
                   h_ref, q_ref, kb_ref, k_ref, v_ref, vb_ref, u_ref, f_ref, *, time_minor):
    h = (_rms(x_ref[...]) * g_ref[...]).astype(BF16)
    h_ref[...] = h
    a = q_ref.shape[1]
    q_ref[...] = (_dot(h, w_ref[:, 0:a]) * SCALE).astype(BF16)
    k = _dot(h, w_ref[:, a:2 * a])
    kb_ref[...] = k.astype(BF16)
    v = _dot(h, w_ref[:, 2 * a:3 * a])
    vb_ref[...] = v.astype(BF16)
    u_ref[...] = _dot(h, w_ref[:, 3 * a:])
    f = _dot(h, wf_ref[...]) + bf_ref[...]
    if time_minor:
        k_ref[0] = k.T
        v_ref[0] = v.T
        f_ref[0] = jax.nn.log_sigmoid(f.T[0:f_ref.shape[1], :])
    else:
        k_ref[...] = k
        v_ref[...] = v
        f_ref[...] = jax.nn.log_sigmoid(f[:, 0:f_ref.shape[1]])


def _inproj(x, g, w, wf, bf, *, tm, seq, time_minor):
    n, d = x.shape
    a = ATTN_W
    p = w.shape[1] - 3 * a
    nh = N_HEADS
    row = lambda width: pl.BlockSpec((tm, width), lambda i: (i, 0))
    if time_minor:
        steps = seq // tm
        feat = lambda rows: (jax.ShapeDtypeStruct((n // seq, rows, seq), F32),
                             pl.BlockSpec((1, rows, tm), lambda i: (i // steps, 0, i % steps)))
    else:
        feat = lambda rows: (jax.ShapeDtypeStruct((n, rows), F32), row(rows))
    (k_shape, k_spec), (f_shape, f_spec) = feat(a), feat(nh)
    out_shape = (
        jax.ShapeDtypeStruct((n, d), BF16),
        jax.ShapeDtypeStruct((n, a), BF16),
        jax.ShapeDtypeStruct((n, a), BF16),
        k_shape,
        k_shape,
        jax.ShapeDtypeStruct((n, a), BF16),
        jax.ShapeDtypeStruct((n, p), F32),
        f_shape,
    )
    est = (2 * _nbytes((tm, d), F32) + _nbytes(w.shape, BF16) + _nbytes(wf.shape, BF16)
           + 2 * (_nbytes((tm, d), BF16) + 3 * _nbytes((tm, a), BF16) + 2 * _nbytes((tm, a), F32)
                  + _nbytes((tm, p), F32) + _nbytes((tm, LANES), F32))
           + 4 * _nbytes((tm, a), F32))
    return pl.pallas_call(
        functools.partial(_inproj_kernel, time_minor=time_minor),
        grid=(n // tm,),
        in_specs=[row(d), _resident((1, d)), _resident(w.shape), _resident(wf.shape), _resident((1, LANES))],
        out_specs=(row(d), row(a), row(a), k_spec, k_spec, row(a), row(p), f_spec),
        out_shape=out_shape,
        compiler_params=pltpu.CompilerParams(
            dimension_semantics=("arbitrary",), vmem_limit_bytes=_vmem_limit(est)),
        name="inproj",
    )(x, g, w, wf, bf)


def _cumsum_kernel(x_ref, o_ref):
    x = x_ref[...]
    length = x.shape[1]
    lane = lax.broadcasted_iota(jnp.int32, x.shape, 1)
    shift = 1
    while shift < length:
        x = x + jnp.where(lane >= shift, pltpu.roll(x, shift, axis=1), 0.0)
        shift *= 2
    o_ref[...] = x


def _cumsum_lanes(x):
    rows, length = x.shape
    tr = _tile(rows, CUMSUM_ROWS)
    spec = pl.BlockSpec((tr, length), lambda i: (i, 0))
    return pl.pallas_call(
        _cumsum_kernel,
        grid=(rows // tr,),
        in_specs=[spec],
        out_specs=spec,
        out_shape=jax.ShapeDtypeStruct(x.shape, F32),
        compiler_params=pltpu.CompilerParams(dimension_semantics=("arbitrary",)),
        name="cumsum",
    )(x)


VALUE_ROWS = HEAD_DIM + 16


def _attn_kernel(q_ref, k_ref, vt_ref, c_ref, o_ref,
                 vta_ref, ckrep_ref, s_ref, p_ref, alpha_ref, m_ref, acc_ref, *, tq):
    t = k_ref.shape[1]
    nq = t // tq
    for a in range(HEADS_PER_BLOCK):
        vta_ref[a, 0:HEAD_DIM, :] = vt_ref[0, a * HEAD_DIM:(a + 1) * HEAD_DIM, :].astype(BF16)
        vta_ref[a, HEAD_DIM:VALUE_ROWS, :] = jnp.ones((VALUE_ROWS - HEAD_DIM, t), BF16)
        ckrep_ref[a] = jnp.broadcast_to(c_ref[0, 0, a:a + 1, :], (LANES, t)).T

    pairs = [(i, j) for i in range(nq) for j in range(i + 1)]
    rows = lambda j: slice(j * tq, (j + 1) * tq)

    def scores(w):
        i, j = pairs[w]
        q2 = q_ref[0, rows(i), :]
        lane = lax.broadcasted_iota(jnp.int32, q2.shape, 1)
        kt = k_ref[0, rows(j), :]
        for a in range(HEADS_PER_BLOCK):
            qa = jnp.where((lane // HEAD_DIM) == a, q2, jnp.zeros_like(q2))
            ckr = ckrep_ref[a, rows(j), :]
            s_ref[w % 2, a] = _dot_nt(kt, qa) - jnp.concatenate([ckr] * (tq // LANES), axis=1)

    def probs(w):
        i, j = pairs[w]
        for a in range(HEADS_PER_BLOCK):
            s = s_ref[w % 2, a]
            if j == i:
                r = lax.broadcasted_iota(jnp.int32, s.shape, 0)
                c = lax.broadcasted_iota(jnp.int32, s.shape, 1)
                s = jnp.where(r <= c, s, -jnp.inf)
            cqa = c_ref[0, 0, a:a + 1, rows(i)]
            smax = jnp.max(s, axis=0, keepdims=True) + cqa
            if j == 0:
                m_new = smax
            else:
                m_prev = m_ref[a]
                m_new = jnp.maximum(m_prev, smax)
                alpha_ref[w % 2, a] = jnp.exp(m_prev - m_new)
            p_ref[w % 2, a] = jnp.exp(s - (m_new - cqa)).astype(BF16)
            m_ref[a] = m_new

    def values(w):
        i, j = pairs[w]
        for a in range(HEADS_PER_BLOCK):
            pv = _dot(vta_ref[a, :, rows(j)], p_ref[w % 2, a])
            acc_ref[a] = pv if j == 0 else alpha_ref[w % 2, a] * acc_ref[a] + pv
        if j == i:
            ot = jnp.concatenate(
                [acc_ref[a, 0:HEAD_DIM, :] / acc_ref[a, HEAD_DIM:HEAD_DIM + 1, :]
                 for a in range(HEADS_PER_BLOCK)], axis=0)
            o_ref[0, rows(i), :] = ot.T.astype(o_ref.dtype)

    n = len(pairs)
    scores(0)
    for w in range(n):
        if w + 1 < n:
            scores(w + 1)
        probs(w)
        if w >= 1:
            values(w - 1)
    values(n - 1)


def _attn_prompt(q, kb, vt, c_rows, *, tq):
    b, t, a = q.shape
    nblk = a // LANES
    spec = pl.BlockSpec((1, t, LANES), lambda bi, hi: (bi, 0, hi))
    return pl.pallas_call(
        functools.partial(_attn_kernel, tq=tq),
        grid=(b, nblk),
        in_specs=[spec, spec, pl.BlockSpec((1, LANES, t), lambda bi, hi: (bi, hi, 0)),
                  pl.BlockSpec((1, 1, HEADS_PER_BLOCK, t), lambda bi, hi: (bi, hi, 0, 0))],
        out_specs=spec,
        out_shape=jax.ShapeDtypeStruct((b, t, a), BF16),
        scratch_shapes=[pltpu.VMEM((HEADS_PER_BLOCK, VALUE_ROWS, t), BF16),
                        pltpu.VMEM((HEADS_PER_BLOCK, t, LANES), F32),
                        pltpu.VMEM((2, HEADS_PER_BLOCK, tq, tq), F32),
                        pltpu.VMEM((2, HEADS_PER_BLOCK, tq, tq), BF16),
                        pltpu.VMEM((2, HEADS_PER_BLOCK, 1, tq), F32),
                        pltpu.VMEM((HEADS_PER_BLOCK, 1, tq), F32),
                        pltpu.VMEM((HEADS_PER_BLOCK, VALUE_ROWS, tq), F32)],
        compiler_params=pltpu.CompilerParams(dimension_semantics=("arbitrary", "arbitrary")),
        name="attn_prompt",
    )(q, kb, vt, c_rows)


def _attn_sample_kernel(q_ref, kct_ref, vct_ref, kn_ref, vn_ref, cc_ref, cnr_ref, cnc_ref, o_ref,
                        kb_ref, vb_ref, s_ref, p_ref, pn_ref):
    n, a = q_ref.shape[1], q_ref.shape[2]
    nh = a // HEAD_DIM
    q = q_ref[0]
    qt = jnp.concatenate([q] * nh, axis=0)
    row_h = lax.broadcasted_iota(jnp.int32, qt.shape, 0) // n
    col_h = lax.broadcasted_iota(jnp.int32, qt.shape, 1) // HEAD_DIM
    qbd = jnp.where(row_h == col_h, qt, jnp.zeros_like(qt))
    kb_ref[...] = kct_ref[0].astype(BF16)
    vb_ref[...] = vct_ref[0].astype(BF16)
    s_ref[...] = _dot(qbd, kb_ref[...])
    s_new = _dot_nt(qbd, kn_ref[0])
    cnc = cnc_ref[0]
    r = lax.broadcasted_iota(jnp.int32, (n, n), 0)
    c = lax.broadcasted_iota(jnp.int32, (n, n), 1)
    inv_l = []
    for h in range(nh):
        rows = slice(h * n, (h + 1) * n)
        cq = cnc[:, h:h + 1]
        sc = s_ref[rows, :] + (cq - cc_ref[0, h:h + 1, :])
        sn = s_new[rows, :] + (cq - cnr_ref[0, h:h + 1, :])
        sn = jnp.where(c <= r, sn, -jnp.inf)
        m = jnp.maximum(jnp.max(sc, axis=1, keepdims=True), jnp.max(sn, axis=1, keepdims=True))
        pc = jnp.exp(sc - m)
        pn = jnp.exp(sn - m)
        inv_l.append(1.0 / (jnp.sum(pc, axis=1, keepdims=True) + jnp.sum(pn, axis=1, keepdims=True)))
        p_ref[rows, :] = pc.astype(BF16)
        pn_ref[rows, :] = pn.astype(BF16)
    o = _dot_nt(p_ref[...], vb_ref[...]) + _dot(pn_ref[...], vn_ref[0])
    for h in range(nh):
        rows = slice(h * n, (h + 1) * n)
        cols = slice(h * HEAD_DIM, (h + 1) * HEAD_DIM)
        o_ref[0, :, cols] = (o[rows, cols] * inv_l[h]).astype(o_ref.dtype)


def _attn_sample(q, kct, vct, kn, vn, cc, cnr, cnc):
    b, n, a = q.shape
    past = kct.shape[2]
    nh = a // HEAD_DIM
    per_b = lambda *tail: pl.BlockSpec((1,) + tail, lambda bi: (bi,) + (0,) * len(tail))
    est = (4 * _nbytes((a, past), F32) + 2 * _nbytes((a, past), BF16)
           + _nbytes((nh * n, past), F32) + _nbytes((nh * n, past), BF16) + 4 * _nbytes((nh * n, a), F32))
    return pl.pallas_call(
        _attn_sample_kernel,
        grid=(b,),
        in_specs=[per_b(n, a), per_b(a, past), per_b(a, past), per_b(n, a), per_b(n, a),
                  per_b(nh, past), per_b(nh, n), per_b(n, nh)],
        out_specs=per_b(n, a),
        out_shape=jax.ShapeDtypeStruct((b, n, a), BF16),
        scratch_shapes=[pltpu.VMEM((a, past), BF16), pltpu.VMEM((a, past), BF16),
                        pltpu.VMEM((nh * n, past), F32), pltpu.VMEM((nh * n, past), BF16),
                        pltpu.VMEM((nh * n, n), BF16)],
        compiler_params=pltpu.CompilerParams(
            dimension_semantics=("arbitrary",), vmem_limit_bytes=_vmem_limit(est)),
        name="attn_sample",
    )(q, kct, vct, kn, vn, cc, cnr, cnc)


def _merge_kernel(a_ref, h_ref, u_ref, hist_ref, x_ref, wup_ref, wga_ref, wgb_ref, wpool_ref, ps_ref,
                  wout_ref, gpost_ref, gpre_ref, x1_ref, h2_ref, ext_ref, m_ref,
                  *, pos0, zero_first):
    nseg, tl, _ = u_ref.shape
    i = pl.program_id(1)
    hist = hist_ref[...]
    if zero_first:
        hist = jnp.where(i == 0, 0.0, hist)
    ext_ref[:, 0:HIST_ROWS, :] = hist
    ext_ref[:, HIST_ROWS:HIST_ROWS + tl, :] = u_ref[...]
    gw = u_ref.shape[2] // POOL_GROUPS
    ogw = wpool_ref.shape[2]
    pos = pos0 + i * tl + lax.broadcasted_iota(jnp.int32, (nseg, tl, gw), 1)
    a = a_ref[...]
    h = h_ref[...]
    for g, w in enumerate(POOL_WINDOWS):
        cs = slice(g * gw, (g + 1) * gw)
        cur = ext_ref[:, HIST_ROWS:HIST_ROWS + tl, cs]
        tot = cur
        for s in range(1, w):
            tot = tot + ext_ref[:, HIST_ROWS - s:HIST_ROWS - s + tl, cs]
        cnt = jnp.minimum(pos + 1, w).astype(F32)
        pooled = (tot / cnt - cur).reshape(nseg * tl, gw).astype(BF16)
        os_ = slice(g * ogw, (g + 1) * ogw)
        br_b = _dot(pooled, wpool_ref[g]) * ps_ref[:, os_]
        br_a = _dot(a, wup_ref[:, os_])
        ga = _dot(h, wga_ref[:, os_])
        gb = _dot(h, wgb_ref[:, os_])
        m_ref[:, os_] = (jax.nn.sigmoid(ga) * br_a + jax.nn.sigmoid(gb) * br_b).astype(BF16)
    x1 = x_ref[...] + _rms(_dot(m_ref[...], wout_ref[...])) * gpost_ref[...]
    x1_ref[...] = x1
    h2_ref[...] = (_rms(x1) * gpre_ref[...]).astype(BF16)


def _merge(a, h, u3, hist, hist_map, x, wup, wga, wgb, wpool, ps, wout, gpost, gpre,
           *, nseg, tl, pos0, zero_first):
    s_total, l_total, p = u3.shape
    n, d = x.shape
    tm = nseg * tl
    steps = l_total // tl
    assert nseg == 1 or steps == 1
    row = lambda width: pl.BlockSpec((tm, width), lambda s, i: (s * steps + i, 0))
    est = (_nbytes(wup.shape, BF16) + 2 * _nbytes(wga.shape, BF16) + _nbytes(wpool.shape, BF16)
           + _nbytes(wout.shape, BF16)
           + 2 * (_nbytes((tm, a.shape[1]), BF16) + 2 * _nbytes((tm, d), BF16) + 2 * _nbytes((tm, d), F32)
                  + _nbytes((tm + HIST_ROWS, p), F32))
           + _nbytes((tm + nseg * HIST_ROWS, p), F32) + _nbytes((tm, d), BF16) + 6 * _nbytes((tm, d), F32))
    return pl.pallas_call(
        functools.partial(_merge_kernel, pos0=pos0, zero_first=zero_first),
        grid=(s_total // nseg, steps),
        in_specs=[row(a.shape[1]), row(d),
                  pl.BlockSpec((nseg, tl, p), lambda s, i: (s, i, 0)),
                  pl.BlockSpec((nseg, HIST_ROWS, p), hist_map),
                  row(d),
                  _resident(wup.shape), _resident(wga.shape), _resident(wgb.shape), _resident(wpool.shape),
                  _resident((1, d)), _resident(wout.shape), _resident((1, d)), _resident((1, d))],
        out_specs=(row(d), row(d)),
        out_shape=(jax.ShapeDtypeStruct((n, d), F32), jax.ShapeDtypeStruct((n, d), BF16)),
        scratch_shapes=[pltpu.VMEM((nseg, HIST_ROWS + tl, p), F32), pltpu.VMEM((tm, d), BF16)],
        compiler_params=pltpu.CompilerParams(
            dimension_semantics=("arbitrary", "arbitrary"), vmem_limit_bytes=_vmem_limit(est)),
        name="merge",
    )(a, h, u3, hist, x, wup, wga, wgb, wpool, ps, wout, gpost, gpre)


def _ffn_kernel(h_ref, x1_ref, w1_ref, w2_ref, g_ref, o_ref):
    f = pl.program_id(1)

    @pl.when(f == 0)
    def _():
        o_ref[...] = jnp.zeros_like(o_ref)

    z = jnp.square(jnp.maximum(_dot(h_ref[...], w1_ref[...]), 0.0)).astype(BF16)
    o_ref[...] += _dot(z, w2_ref[...])

    @pl.when(f == pl.num_programs(1) - 1)
    def _():
        o_ref[...] = x1_ref[...] + _rms(o_ref[...]) * g_ref[...]


def _ffn(h2, x1, w1, w2, g, *, tm, tf):
    n, d = x1.shape
    dff = w1.shape[1]
    est = (2 * _nbytes((tm, d), BF16) + 4 * _nbytes((tm, d), F32)
           + 4 * _nbytes((d, tf), BF16) + 2 * _nbytes((tm, tf), F32) + _nbytes((tm, d), F32))
    return pl.pallas_call(
        _ffn_kernel,
        grid=(n // tm, dff // tf),
        in_specs=[pl.BlockSpec((tm, d), lambda i, f: (i, 0)),
                  pl.BlockSpec((tm, d), lambda i, f: (i, 0)),
                  pl.BlockSpec((d, tf), lambda i, f: (0, f)),
                  pl.BlockSpec((tf, d), lambda i, f: (f, 0)),
                  pl.BlockSpec((1, d), lambda i, f: (0, 0))],
        out_specs=pl.BlockSpec((tm, d), lambda i, f: (i, 0)),
        out_shape=jax.ShapeDtypeStruct((n, d), F32),
        compiler_params=pltpu.CompilerParams(
            dimension_semantics=("arbitrary", "arbitrary"), vmem_limit_bytes=_vmem_limit(est)),
        name="ffn",
    )(h2, x1, w1, w2, g)


def _tile(n, pref):
    t = min(n, pref)
    while n % t:
        t //= 2
    return t


INPROJ_TM = 256
ATTN_TQ = 512
MERGE_TM = 256
FFN_TM = 1024
FFN_TF = 512
CUMSUM_ROWS = 32
REPACK_ROWS = 256


def _cum_logf_rows(logf_bht):
    b, h, t = logf_bht.shape
    rows = logf_bht.reshape(b * h, t)
    pad = (-t) % LANES
    if pad:
        rows = jnp.pad(rows, ((0, 0), (0, pad)))
    return _cumsum_lanes(rows)


def _time_minor(x, nbatch):
    return jnp.transpose(x, (0, 2, 3, 1)).reshape(nbatch, -1, x.shape[1])


def _layer(xp, xs, cache_k, cache_v, cache_logf, state_pool, g_mix_pre, w_in, b_f, w_attn_up, w_pool,
           pool_scale, w_out, g_mix_post, g_ffn_pre, w_ff1, w_ff2, g_ffn_post):
    bp, tp, d = xp.shape
    bs, ts, _ = xs.shape
    past = cache_k.shape[1]
    a, nh, p = ATTN_W, N_HEADS, w_pool.shape[0] * w_pool.shape[1]
    off_f = 3 * a
    off_u = off_f + nh
    off_ga = off_u + p
    off_gb = off_ga + d

    w_qkvu, w_f, w_ga, w_gb = _repack_w_in(w_in.astype(F32), off_f=off_f, off_u=off_u, off_ga=off_ga,
                                           off_gb=off_gb, tr=_tile(w_in.shape[0], REPACK_ROWS))
    b_f2 = jnp.pad(b_f.reshape(1, -1), ((0, 0), (0, LANES - nh)))
    w_up = w_attn_up.astype(BF16)
    w_pl = w_pool.astype(BF16)
    w_o = w_out.astype(BF16)
    w1 = w_ff1.astype(BF16)
    w2 = w_ff2.astype(BF16)
    row = lambda v: v.reshape(1, -1)
    g_pre, ps = row(g_mix_pre), row(pool_scale)
    g_post, g_fpre, g_fpost = row(g_mix_post), row(g_ffn_pre), row(g_ffn_post)

    def project(x3, time_minor):
        x2 = x3.reshape(-1, d)
        tm = _tile(x3.shape[1] if time_minor else x2.shape[0], INPROJ_TM)
        return x2, _inproj(x2, g_pre, w_qkvu, w_f, b_f2, tm=tm, seq=x3.shape[1], time_minor=time_minor)

    def finish(x2, h, a_out, u3, hist, hist_map, nseg, tl, pos0, zero_first):
        x1, h2 = _merge(a_out, h, u3, hist, hist_map, x2, w_up, w_ga, w_gb, w_pl, ps, w_o, g_post, g_fpre,
                        nseg=nseg, tl=tl, pos0=pos0, zero_first=zero_first)
        n = x2.shape[0]
        return _ffn(h2, x1, w1, w2, g_fpost, tm=_tile(n, FFN_TM), tf=_tile(w1.shape[1], FFN_TF))

    x2, (h, q, kb, kt, vt, _, u, logft) = project(xp, True)
    c_rows = _cum_logf_rows(logft)
    nblk = nh // HEADS_PER_BLOCK
    three = lambda z: z.reshape(bp, tp, a)
    a_out = _attn_prompt(three(q), three(kb), vt, c_rows.reshape(bp, nblk, HEADS_PER_BLOCK, tp),
                         tq=_tile(tp, ATTN_TQ))
    to_bthd = lambda zt: jnp.transpose(zt.reshape(bp, nh, HEAD_DIM, tp), (0, 3, 1, 2))
    u3 = u.reshape(bp, tp, p)
    tl = _tile(tp, MERGE_TM)
    blocks_per_tile = tl // HIST_ROWS
    hist_map = lambda s, i: (s, jnp.maximum(i * blocks_per_tile - 1, 0), 0)
    yp = finish(x2, h, a_out.reshape(bp * tp, a), u3, u3, hist_map, 1, tl, 0, True).reshape(bp, tp, d)
    prompt_out = (yp, to_bthd(kt), to_bthd(vt), jnp.transpose(logft, (0, 2, 1)),
                  jnp.concatenate([jnp.zeros((bp, POOL_HIST, p), F32), u3], axis=1)[:, -POOL_HIST:])

    x2, (h, q, kb, k, v, vb, u, logf) = project(xs, False)
    logf3 = logf.reshape(bs, ts, nh)
    f_all = jnp.concatenate([cache_logf.astype(F32), logf3], axis=1)
    c_all = _cum_logf_rows(jnp.transpose(f_all, (0, 2, 1))).reshape(bs, nh, -1)
    c_cache = c_all[:, :, :past]
    c_new = c_all[:, :, past:past + ts]
    three = lambda z: z.reshape(bs, ts, a)
    a_out = _attn_sample(three(q), _time_minor(cache_k.astype(F32), bs), _time_minor(cache_v.astype(F32), bs),
                         three(kb), three(vb), c_cache, c_new, jnp.transpose(c_new, (0, 2, 1)))
    u3 = u.reshape(bs, ts, p)
    hist = jnp.pad(state_pool.astype(F32), ((0, 0), (HIST_ROWS - POOL_HIST, 0), (0, 0)))
    ys = finish(x2, h, a_out.reshape(bs * ts, a), u3, hist, lambda s, i: (s, 0, 0), bs, ts, past,
                False).reshape(bs, ts, d)
    sample_out = (ys, k.reshape(bs, ts, nh, HEAD_DIM), v.reshape(bs, ts, nh, HEAD_DIM), logf3,
                  jnp.concatenate([state_pool.astype(F32), u3], axis=1)[:, -POOL_HIST:])
    return prompt_out, sample_out


def kernel(x_prompt, x_sample, cache_k, cache_v, cache_logf, state_pool, g_mix_pre, w_in, b_f, w_attn_up,
           w_pool, pool_scale, w_out, g_mix_post, g_ffn_pre, w_ff1, w_ff2, g_ffn_post):
    depth = w_in.shape[0]
    xp, xs = x_prompt, x_sample
    per_layer = []
    for l in range(depth):
        po, so = _layer(xp, xs, cache_k[l], cache_v[l], cache_logf[l], state_pool[l], g_mix_pre[l], w_in[l],
                        b_f[l], w_attn_up[l], w_pool[l], pool_scale[l], w_out[l], g_mix_post[l],
                        g_ffn_pre[l], w_ff1[l], w_ff2[l], g_ffn_post[l])
        xp, xs = po[0], so[0]
        per_layer.append(po[1:] + so[1:])
    stacked = [jnp.stack(leaves, 0) for leaves in zip(*per_layer)]
    return (xp, xs, *stacked)
```

```python
import functools
import math

import jax
import jax.numpy as jnp
from jax import lax
from jax.experimental import pallas as pl
from jax.experimental.pallas import tpu as pltpu

N_HEADS = 16
HEAD_DIM = 64
ATTN_W = N_HEADS * HEAD_DIM
POOL_WINDOWS = (2, 4, 8, 16)
POOL_GROUPS = len(POOL_WINDOWS)
POOL_HIST = max(POOL_WINDOWS) - 1
EPS = 1e-6
SCALE = HEAD_DIM ** -0.5

LANES = 128
SUBLANES = 8
V7X_VMEM_BYTES = 64 * 1024 * 1024
VMEM_LIMIT_CAP = 60000 * 1024

HEADS_PER_BLOCK = LANES // HEAD_DIM
HIST_ROWS = 16

F32 = jnp.float32
BF16 = jnp.bfloat16


def _vmem_limit(nbytes):
    return int(min(VMEM_LIMIT_CAP, max(32 * 1024 * 1024, nbytes * 5 // 4)))


def _nbytes(shape, dtype):
    return math.prod(shape) * jnp.dtype(dtype).itemsize


def _rms(x):
    return x * lax.rsqrt(jnp.mean(x * x, axis=-1, keepdims=True) + EPS)


def _dot(a, b):
    return jnp.dot(a, b, preferred_element_type=F32)


def _dot_nt(a, b):
    return lax.dot_general(a, b, (((1,), (1,)), ((), ())), preferred_element_type=F32)


def _resident(shape):
    zeros = (0,) * len(shape)
    return pl.BlockSpec(shape, lambda *_: zeros, pipeline_mode=pl.Buffered(1))


def _repack_kernel(wt_ref, qkvu_ref, f_ref, ga_ref, gb_ref, *, off_f, off_u, off_ga, off_gb):
    nh = off_u - off_f
    p = off_ga - off_u
    d = off_gb - off_ga
    piece = lambda lo, hi: wt_ref[lo:hi, :].T.astype(BF16)
    qkvu_ref[:, 0:off_f] = piece(0, off_f)
    qkvu_ref[:, off_f:off_f + p] = piece(off_u, off_ga)
    fwin = wt_ref[off_f:off_f + LANES, :].T
    lane = lax.broadcasted_iota(jnp.int32, fwin.shape, 1)
    f_ref[...] = jnp.where(lane < nh, fwin, 0.0).astype(BF16)
    ga_ref[...] = piece(off_ga, off_gb)
    gb_ref[...] = piece(off_gb, off_gb + d)


def _repack_w_in(wt, *, off_f, off_u, off_ga, off_gb, tr):
    width, rows = wt.shape
    p = off_ga - off_u
    d = off_gb - off_ga
    blk = lambda cols: pl.BlockSpec((tr, cols), lambda i: (i, 0))
    est = 4 * _nbytes((width, tr), F32) + 2 * _nbytes((tr, off_f + p + LANES + 2 * d), BF16)
    return pl.pallas_call(
        functools.partial(_repack_kernel, off_f=off_f, off_u=off_u, off_ga=off_ga, off_gb=off_gb),
        grid=(rows // tr,),
        in_specs=[pl.BlockSpec((width, tr), lambda i: (0, i))],
        out_specs=(blk(off_f + p), blk(LANES), blk(d), blk(d)),
        out_shape=(jax.ShapeDtypeStruct((rows, off_f + p), BF16), jax.ShapeDtypeStruct((rows, LANES), BF16),
                   jax.ShapeDtypeStruct((rows, d), BF16), jax.ShapeDtypeStruct((rows, d), BF16)),
        compiler_params=pltpu.CompilerParams(
            dimension_semantics=("arbitrary",), vmem_limit_bytes=_vmem_limit(est)),
        name="repack_w_in",
    )(wt)


def _inproj_kernel(x_ref, g_ref, w_ref, wf_ref, bf_ref,
                   h_ref, q_ref, kb_ref, k_ref, v_ref, vb_ref, u_ref, f_ref, *, time_minor):
    h = (_rms(x_ref[...]) * g_ref[...]).astype(BF16)
    h_ref[...] = h
    a = q_ref.shape[1]
    q_ref[...] = (_dot(h, w_ref[:, 0:a]) * SCALE).astype(BF16)
    k = _dot(h, w_ref[:, a:2 * a])
    kb_ref[...] = k.astype(BF16)
    v = _dot(h, w_ref[:, 2 * a:3 * a])
    vb_ref[...] = v.astype(BF16)
    u_ref[...] = _dot(h, w_ref[:, 3 * a:])
    f = _dot(h, wf_ref[...]) + bf_ref[...]
    if time_minor:
        k_ref[0] = k.T
        v_ref[0] = v.T
        f_ref[0] = jax.nn.log_sigmoid(f.T[0:f_ref.shape[1], :])
    else:
        k_ref[...] = k
        v_ref[...] = v
        f_ref[...] = jax.nn.log_sigmoid(f[:, 0:f_ref.shape[1]])


def _inproj(x, g, w, wf, bf, *, tm, seq, time_minor):
    n, d = x.shape
    a = ATTN_W
    p = w.shape[1] - 3 * a
    nh = N_HEADS
    row = lambda width: pl.BlockSpec((tm, width), lambda i: (i, 0))
    if time_minor:
        steps = seq // tm
        feat = lambda rows: (jax.ShapeDtypeStruct((n // seq, rows, seq), F32),
                             pl.BlockSpec((1, rows, tm), lambda i: (i // steps, 0, i % steps)))
    else:
        feat = lambda rows: (jax.ShapeDtypeStruct((n, rows), F32), row(rows))
    (k_shape, k_spec), (f_shape, f_spec) = feat(a), feat(nh)
    out_shape = (
        jax.ShapeDtypeStruct((n, d), BF16),
        jax.ShapeDtypeStruct((n, a), BF16),
        jax.ShapeDtypeStruct((n, a), BF16),
        k_shape,
        k_shape,
        jax.ShapeDtypeStruct((n, a), BF16),
        jax.ShapeDtypeStruct((n, p), F32),
        f_shape,
    )
    est = (2 * _nbytes((tm, d), F32) + _nbytes(w.shape, BF16) + _nbytes(wf.shape, BF16)
           + 2 * (_nbytes((tm, d), BF16) + 3 * _nbytes((tm, a), BF16) + 2 * _nbytes((tm, a), F32)
                  + _nbytes((tm, p), F32) + _nbytes((tm, LANES), F32))
           + 4 * _nbytes((tm, a), F32))
    return pl.pallas_call(
        functools.partial(_inproj_kernel, time_minor=time_minor),
        grid=(n // tm,),
        in_specs=[row(d), _resident((1, d)), _resident(w.shape), _resident(wf.shape), _resident((1, LANES))],
        out_specs=(row(d), row(a), row(a), k_spec, k_spec, row(a), row(p), f_spec),
        out_shape=out_shape,
        compiler_params=pltpu.CompilerParams(
            dimension_semantics=("arbitrary",), vmem_limit_bytes=_vmem_limit(est)),
        name="inproj",
    )(x, g, w, wf, bf)


def _cumsum_kernel(x_ref, o_ref):
    x = x_ref[...]
    length = x.shape[1]
    lane = lax.broadcasted_iota(jnp.int32, x.shape, 1)
    shift = 1
    while shift < length:
        x = x + jnp.where(lane >= shift, pltpu.roll(x, shift, axis=1), 0.0)
        shift *= 2
    o_ref[...] = x


def _cumsum_lanes(x):
    rows, length = x.shape
    tr = _tile(rows, CUMSUM_ROWS)
    spec = pl.BlockSpec((tr, length), lambda i: (i, 0))
    return pl.pallas_call(
        _cumsum_kernel,
        grid=(rows // tr,),
        in_specs=[spec],
        out_specs=spec,
        out_shape=jax.ShapeDtypeStruct(x.shape, F32),
        compiler_params=pltpu.CompilerParams(dimension_semantics=("arbitrary",)),
        name="cumsum",
    )(x)


VALUE_ROWS = HEAD_DIM + 16


def _attn_kernel(q_ref, k_ref, vt_ref, c_ref, o_ref,
                 vta_ref, ckrep_ref, s_ref, p_ref, alpha_ref, m_ref, acc_ref, *, tq):
    t = k_ref.shape[1]
    nq = t // tq
    for a in range(HEADS_PER_BLOCK):
        vta_ref[a, 0:HEAD_DIM, :] = vt_ref[0, a * HEAD_DIM:(a + 1) * HEAD_DIM, :].astype(BF16)
        vta_ref[a, HEAD_DIM:VALUE_ROWS, :] = jnp.ones((VALUE_ROWS - HEAD_DIM, t), BF16)
        ckrep_ref[a] = jnp.broadcast_to(c_ref[0, 0, a:a + 1, :], (LANES, t)).T

    pairs = [(i, j) for i in range(nq) for j in range(i + 1)]
    rows = lambda j: slice(j * tq, (j + 1) * tq)

    def scores(w):
        i, j = pairs[w]
        q2 = q_ref[0, rows(i), :]
        lane = lax.broadcasted_iota(jnp.int32, q2.shape, 1)
        kt = k_ref[0, rows(j), :]
        for a in range(HEADS_PER_BLOCK):
            qa = jnp.where((lane // HEAD_DIM) == a, q2, jnp.zeros_like(q2))
            ckr = ckrep_ref[a, rows(j), :]
            s_ref[w % 2, a] = _dot_nt(kt, qa) - jnp.concatenate([ckr] * (tq // LANES), axis=1)

    def probs(w):
        i, j = pairs[w]
        for a in range(HEADS_PER_BLOCK):
            s = s_ref[w % 2, a]
            if j == i:
                r = lax.broadcasted_iota(jnp.int32, s.shape, 0)
                c = lax.broadcasted_iota(jnp.int32, s.shape, 1)
                s = jnp.where(r <= c, s, -jnp.inf)
            cqa = c_ref[0, 0, a:a + 1, rows(i)]
            smax = jnp.max(s, axis=0, keepdims=True) + cqa
            if j == 0:
                m_new = smax
            else:
                m_prev = m_ref[a]
                m_new = jnp.maximum(m_prev, smax)
                alpha_ref[w % 2, a] = jnp.exp(m_prev - m_new)
            p_ref[w % 2, a] = jnp.exp(s - (m_new - cqa)).astype(BF16)
            m_ref[a] = m_new

    def values(w):
        i, j = pairs[w]
        for a in range(HEADS_PER_BLOCK):
            pv = _dot(vta_ref[a, :, rows(j)], p_ref[w % 2, a])
            acc_ref[a] = pv if j == 0 else alpha_ref[w % 2, a] * acc_ref[a] + pv
        if j == i:
            ot = jnp.concatenate(
                [acc_ref[a, 0:HEAD_DIM, :] / acc_ref[a, HEAD_DIM:HEAD_DIM + 1, :]
                 for a in range(HEADS_PER_BLOCK)], axis=0)
            o_ref[0, rows(i), :] = ot.T.astype(o_ref.dtype)

    n = len(pairs)
    scores(0)
    for w in range(n):
        if w + 1 < n:
            scores(w + 1)
        probs(w)
        if w >= 1:
            values(w - 1)
    values(n - 1)


def _attn_prompt(q, kb, vt, c_rows, *, tq):
    b, t, a = q.shape
    nblk = a // LANES
    spec = pl.BlockSpec((1, t, LANES), lambda bi, hi: (bi, 0, hi))
    return pl.pallas_call(
        functools.partial(_attn_kernel, tq=tq),
        grid=(b, nblk),
        in_specs=[spec, spec, pl.BlockSpec((1, LANES, t), lambda bi, hi: (bi, hi, 0)),
                  pl.BlockSpec((1, 1, HEADS_PER_BLOCK, t), lambda bi, hi: (bi, hi, 0, 0))],
        out_specs=spec,
        out_shape=jax.ShapeDtypeStruct((b, t, a), BF16),
        scratch_shapes=[pltpu.VMEM((HEADS_PER_BLOCK, VALUE_ROWS, t), BF16),
                        pltpu.VMEM((HEADS_PER_BLOCK, t, LANES), F32),
                        pltpu.VMEM((2, HEADS_PER_BLOCK, tq, tq), F32),
                        pltpu.VMEM((2, HEADS_PER_BLOCK, tq, tq), BF16),
                        pltpu.VMEM((2, HEADS_PER_BLOCK, 1, tq), F32),
                        pltpu.VMEM((HEADS_PER_BLOCK, 1, tq), F32),
                        pltpu.VMEM((HEADS_PER_BLOCK, VALUE_ROWS, tq), F32)],
        compiler_params=pltpu.CompilerParams(dimension_semantics=("arbitrary", "arbitrary")),
        name="attn_prompt",
    )(q, kb, vt, c_rows)


def _attn_sample_kernel(q_ref, kct_ref, vct_ref, kn_ref, vn_ref, cc_ref, cnr_ref, cnc_ref, o_ref,
                        kb_ref, vb_ref, s_ref, p_ref, pn_ref):
    n, a = q_ref.shape[1], q_ref.shape[2]
    nh = a // HEAD_DIM
    q = q_ref[0]
    qt = jnp.concatenate([q] * nh, axis=0)
    row_h = lax.broadcasted_iota(jnp.int32, qt.shape, 0) // n
    col_h = lax.broadcasted_iota(jnp.int32, qt.shape, 1) // HEAD_DIM
    qbd = jnp.where(row_h == col_h, qt, jnp.zeros_like(qt))
    kb_ref[...] = kct_ref[0].astype(BF16)
    vb_ref[...] = vct_ref[0].astype(BF16)
    s_ref[...] = _dot(qbd, kb_ref[...])
    s_new = _dot_nt(qbd, kn_ref[0])
    cnc = cnc_ref[0]
    r = lax.broadcasted_iota(jnp.int32, (n, n), 0)
    c = lax.broadcasted_iota(jnp.int32, (n, n), 1)
    inv_l = []
    for h in range(nh):
        rows = slice(h * n, (h + 1) * n)
        cq = cnc[:, h:h + 1]
        sc = s_ref[rows, :] + (cq - cc_ref[0, h:h + 1, :])
        sn = s_new[rows, :] + (cq - cnr_ref[0, h:h + 1, :])
        sn = jnp.where(c <= r, sn, -jnp.inf)
        m = jnp.maximum(jnp.max(sc, axis=1, keepdims=True), jnp.max(sn, axis=1, keepdims=True))
        pc = jnp.exp(sc - m)
        pn = jnp.exp(sn - m)
        inv_l.append(1.0 / (jnp.sum(pc, axis=1, keepdims=True) + jnp.sum(pn, axis=1, keepdims=True)))
        p_ref[rows, :] = pc.astype(BF16)
        pn_ref[rows, :] = pn.astype(BF16)
    o = _dot_nt(p_ref[...], vb_ref[...]) + _dot(pn_ref[...], vn_ref[0])
    for h in range(nh):
        rows = slice(h * n, (h + 1) * n)
        cols = slice(h * HEAD_DIM, (h + 1) * HEAD_DIM)
        o_ref[0, :, cols] = (o[rows, cols] * inv_l[h]).astype(o_ref.dtype)


def _attn_sample(q, kct, vct, kn, vn, cc, cnr, cnc):
    b, n, a = q.shape
    past = kct.shape[2]
    nh = a // HEAD_DIM
    per_b = lambda *tail: pl.BlockSpec((1,) + tail, lambda bi: (bi,) + (0,) * len(tail))
    est = (4 * _nbytes((a, past), F32) + 2 * _nbytes((a, past), BF16)
           + _nbytes((nh * n, past), F32) + _nbytes((nh * n, past), BF16) + 4 * _nbytes((nh * n, a), F32))
    return pl.pallas_call(
        _attn_sample_kernel,
        grid=(b,),
        in_specs=[per_b(n, a), per_b(a, past), per_b(a, past), per_b(n, a), per_b(n, a),
                  per_b(nh, past), per_b(nh, n), per_b(n, nh)],
        out_specs=per_b(n, a),
        out_shape=jax.ShapeDtypeStruct((b, n, a), BF16),
        scratch_shapes=[pltpu.VMEM((a, past), BF16), pltpu.VMEM((a, past), BF16),
                        pltpu.VMEM((nh * n, past), F32), pltpu.VMEM((nh * n, past), BF16),
                        pltpu.VMEM((nh * n, n), BF16)],
        compiler_params=pltpu.CompilerParams(
            dimension_semantics=("arbitrary",), vmem_limit_bytes=_vmem_limit(est)),
        name="attn_sample",
    )(q, kct, vct, kn, vn, cc, cnr, cnc)


def _merge_kernel(a_ref, h_ref, u_ref, hist_ref, x_ref, wup_ref, wga_ref, wgb_ref, wpool_ref, ps_ref,
                  wout_ref, gpost_ref, gpre_ref, x1_ref, h2_ref, ext_ref, m_ref,
                  *, pos0, zero_first):
    nseg, tl, _ = u_ref.shape
    i = pl.program_id(1)
    hist = hist_ref[...]
    if zero_first:
        hist = jnp.where(i == 0, 0.0, hist)
    ext_ref[:, 0:HIST_ROWS, :] = hist
    ext_ref[:, HIST_ROWS:HIST_ROWS + tl, :] = u_ref[...]
    gw = u_ref.shape[2] // POOL_GROUPS
    ogw = wpool_ref.shape[2]
    pos = pos0 + i * tl + lax.broadcasted_iota(jnp.int32, (nseg, tl, gw), 1)
    a = a_ref[...]
    h = h_ref[...]
    for g, w in enumerate(POOL_WINDOWS):
        cs = slice(g * gw, (g + 1) * gw)
        cur = ext_ref[:, HIST_ROWS:HIST_ROWS + tl, cs]
        tot = cur
        for s in range(1, w):
            tot = tot + ext_ref[:, HIST_ROWS - s:HIST_ROWS - s + tl, cs]
        cnt = jnp.minimum(pos + 1, w).astype(F32)
        pooled = (tot / cnt - cur).reshape(nseg * tl, gw).astype(BF16)
        os_ = slice(g * ogw, (g + 1) * ogw)
        br_b = _dot(pooled, wpool_ref[g]) * ps_ref[:, os_]
        br_a = _dot(a, wup_ref[:, os_])
        ga = _dot(h, wga_ref[:, os_])
        gb = _dot(h, wgb_ref[:, os_])
        m_ref[:, os_] = (jax.nn.sigmoid(ga) * br_a + jax.nn.sigmoid(gb) * br_b).astype(BF16)
    x1 = x_ref[...] + _rms(_dot(m_ref[...], wout_ref[...])) * gpost_ref[...]
    x1_ref[...] = x1
    h2_ref[...] = (_rms(x1) * gpre_ref[...]).astype(BF16)


def _merge(a, h, u3, hist, hist_map, x, wup, wga, wgb, wpool, ps, wout, gpost, gpre,
           *, nseg, tl, pos0, zero_first):
    s_total, l_total, p = u3.shape
    n, d = x.shape
    tm = nseg * tl
    steps = l_total // tl
    assert nseg == 1 or steps == 1
    row = lambda width: pl.BlockSpec((tm, width), lambda s, i: (s * steps + i, 0))
    est = (_nbytes(wup.shape, BF16) + 2 * _nbytes(wga.shape, BF16) + _nbytes(wpool.shape, BF16)
           + _nbytes(wout.shape, BF16)
           + 2 * (_nbytes((tm, a.shape[1]), BF16) + 2 * _nbytes((tm, d), BF16) + 2 * _nbytes((tm, d), F32)
                  + _nbytes((tm + HIST_ROWS, p), F32))
           + _nbytes((tm + nseg * HIST_ROWS, p), F32) + _nbytes((tm, d), BF16) + 6 * _nbytes((tm, d), F32))
    return pl.pallas_call(
        functools.partial(_merge_kernel, pos0=pos0, zero_first=zero_first),
        grid=(s_total // nseg, steps),
        in_specs=[row(a.shape[1]), row(d),
                  pl.BlockSpec((nseg, tl, p), lambda s, i: (s, i, 0)),
                  pl.BlockSpec((nseg, HIST_ROWS, p), hist_map),
                  row(d),
                  _resident(wup.shape), _resident(wga.shape), _resident(wgb.shape), _resident(wpool.shape),
                  _resident((1, d)), _resident(wout.shape), _resident((1, d)), _resident((1, d))],
        out_specs=(row(d), row(d)),
        out_shape=(jax.ShapeDtypeStruct((n, d), F32), jax.ShapeDtypeStruct((n, d), BF16)),
        scratch_shapes=[pltpu.VMEM((nseg, HIST_ROWS + tl, p), F32), pltpu.VMEM((tm, d), BF16)],
        compiler_params=pltpu.CompilerParams(
            dimension_semantics=("arbitrary", "arbitrary"), vmem_limit_bytes=_vmem_limit(est)),
        name="merge",
    )(a, h, u3, hist, x, wup, wga, wgb, wpool, ps, wout, gpost, gpre)


def _ffn_kernel(h_ref, x1_ref, w1_ref, w2_ref, g_ref, o_ref):
    f = pl.program_id(1)

    @pl.when(f == 0)
    def _():
        o_ref[...] = jnp.zeros_like(o_ref)

    z = jnp.square(jnp.maximum(_dot(h_ref[...], w1_ref[...]), 0.0)).astype(BF16)
    o_ref[...] += _dot(z, w2_ref[...])

    @pl.when(f == pl.num_programs(1) - 1)
    def _():
        o_ref[...] = x1_ref[...] + _rms(o_ref[...]) * g_ref[...]


def _ffn(h2, x1, w1, w2, g, *, tm, tf):
    n, d = x1.shape
    dff = w1.shape[1]
    est = (2 * _nbytes((tm, d), BF16) + 4 * _nbytes((tm, d), F32)
           + 4 * _nbytes((d, tf), BF16) + 2 * _nbytes((tm, tf), F32) + _nbytes((tm, d), F32))
    return pl.pallas_call(
        _ffn_kernel,
        grid=(n // tm, dff // tf),
        in_specs=[pl.BlockSpec((tm, d), lambda i, f: (i, 0)),
                  pl.BlockSpec((tm, d), lambda i, f: (i, 0)),
                  pl.BlockSpec((d, tf), lambda i, f: (0, f)),
                  pl.BlockSpec((tf, d), lambda i, f: (f, 0)),
                  pl.BlockSpec((1, d), lambda i, f: (0, 0))],
        out_specs=pl.BlockSpec((tm, d), lambda i, f: (i, 0)),
        out_shape=jax.ShapeDtypeStruct((n, d), F32),
        compiler_params=pltpu.CompilerParams(
            dimension_semantics=("arbitrary", "arbitrary"), vmem_limit_bytes=_vmem_limit(est)),
        name="ffn",
    )(h2, x1, w1, w2, g)


def _tile(n, pref):
    t = min(n, pref)
    while n % t:
        t //= 2
    return t


INPROJ_TM = 256
ATTN_TQ = 512
MERGE_TM = 256
FFN_TM = 512
FFN_TF = 1024
CUMSUM_ROWS = 32
REPACK_ROWS = 256


def _cum_logf_rows(logf_bht):
    b, h, t = logf_bht.shape
    rows = logf_bht.reshape(b * h, t)
    pad = (-t) % LANES
    if pad:
        rows = jnp.pad(rows, ((0, 0), (0, pad)))
    return _cumsum_lanes(rows)


def _time_minor(x, nbatch):
    return jnp.transpose(x, (0, 2, 3, 1)).reshape(nbatch, -1, x.shape[1])


def _layer(xp, xs, cache_k, cache_v, cache_logf, state_pool, g_mix_pre, w_in, b_f, w_attn_up, w_pool,
           pool_scale, w_out, g_mix_post, g_ffn_pre, w_ff1, w_ff2, g_ffn_post):
    bp, tp, d = xp.shape
    bs, ts, _ = xs.shape
    past = cache_k.shape[1]
    a, nh, p = ATTN_W, N_HEADS, w_pool.shape[0] * w_pool.shape[1]
    off_f = 3 * a
    off_u = off_f + nh
    off_ga = off_u + p
    off_gb = off_ga + d

    w_qkvu, w_f, w_ga, w_gb = _repack_w_in(jnp.transpose(w_in.astype(F32)), off_f=off_f, off_u=off_u,
                                           off_ga=off_ga, off_gb=off_gb, tr=_tile(w_in.shape[0], REPACK_ROWS))
    b_f2 = jnp.pad(b_f.reshape(1, -1), ((0, 0), (0, LANES - nh)))
    w_up = w_attn_up.astype(BF16)
    w_pl = w_pool.astype(BF16)
    w_o = w_out.astype(BF16)
    w1 = w_ff1.astype(BF16)
    w2 = w_ff2.astype(BF16)
    row = lambda v: v.reshape(1, -1)
    g_pre, ps = row(g_mix_pre), row(pool_scale)
    g_post, g_fpre, g_fpost = row(g_mix_post), row(g_ffn_pre), row(g_ffn_post)

    def project(x3, time_minor):
        x2 = x3.reshape(-1, d)
        tm = _tile(x3.shape[1] if time_minor else x2.shape[0], INPROJ_TM)
        return x2, _inproj(x2, g_pre, w_qkvu, w_f, b_f2, tm=tm, seq=x3.shape[1], time_minor=time_minor)

    def finish(x2, h, a_out, u3, hist, hist_map, nseg, tl, pos0, zero_first):
        x1, h2 = _merge(a_out, h, u3, hist, hist_map, x2, w_up, w_ga, w_gb, w_pl, ps, w_o, g_post, g_fpre,
                        nseg=nseg, tl=tl, pos0=pos0, zero_first=zero_first)
        n = x2.shape[0]
        return _ffn(h2, x1, w1, w2, g_fpost, tm=_tile(n, FFN_TM), tf=_tile(w1.shape[1], FFN_TF))

    x2, (h, q, kb, kt, vt, _, u, logft) = project(xp, True)
    c_rows = _cum_logf_rows(logft)
    nblk = nh // HEADS_PER_BLOCK
    three = lambda z: z.reshape(bp, tp, a)
    a_out = _attn_prompt(three(q), three(kb), vt, c_rows.reshape(bp, nblk, HEADS_PER_BLOCK, tp),
                         tq=_tile(tp, ATTN_TQ))
    to_bthd = lambda zt: jnp.transpose(zt.reshape(bp, nh, HEAD_DIM, tp), (0, 3, 1, 2))
    u3 = u.reshape(bp, tp, p)
    tl = _tile(tp, MERGE_TM)
    blocks_per_tile = tl // HIST_ROWS
    hist_map = lambda s, i: (s, jnp.maximum(i * blocks_per_tile - 1, 0), 0)
    yp = finish(x2, h, a_out.reshape(bp * tp, a), u3, u3, hist_map, 1, tl, 0, True).reshape(bp, tp, d)
    prompt_out = (yp, to_bthd(kt), to_bthd(vt), jnp.transpose(logft, (0, 2, 1)),
                  jnp.concatenate([jnp.zeros((bp, POOL_HIST, p), F32), u3], axis=1)[:, -POOL_HIST:])

    x2, (h, q, kb, k, v, vb, u, logf) = project(xs, False)
    logf3 = logf.reshape(bs, ts, nh)
    f_all = jnp.concatenate([cache_logf.astype(F32), logf3], axis=1)
    c_all = _cum_logf_rows(jnp.transpose(f_all, (0, 2, 1))).reshape(bs, nh, -1)
    c_cache = c_all[:, :, :past]
    c_new = c_all[:, :, past:past + ts]
    three = lambda z: z.reshape(bs, ts, a)
    a_out = _attn_sample(three(q), _time_minor(cache_k.astype(F32), bs), _time_minor(cache_v.astype(F32), bs),
                         three(kb), three(vb), c_cache, c_new, jnp.transpose(c_new, (0, 2, 1)))
    u3 = u.reshape(bs, ts, p)
    hist = jnp.pad(state_pool.astype(F32), ((0, 0), (HIST_ROWS - POOL_HIST, 0), (0, 0)))
    ys = finish(x2, h, a_out.reshape(bs * ts, a), u3, hist, lambda s, i: (s, 0, 0), bs, ts, past,
                False).reshape(bs, ts, d)
    sample_out = (ys, k.reshape(bs, ts, nh, HEAD_DIM), v.reshape(bs, ts, nh, HEAD_DIM), logf3,
                  jnp.concatenate([state_pool.astype(F32), u3], axis=1)[:, -POOL_HIST:])
    return prompt_out, sample_out


def kernel(x_prompt, x_sample, cache_k, cache_v, cache_logf, state_pool, g_mix_pre, w_in, b_f, w_attn_up,
           w_pool, pool_scale, w_out, g_mix_post, g_ffn_pre, w_ff1, w_ff2, g_ffn_post):
    depth = w_in.shape[0]
    xp, xs = x_prompt, x_sample
    per_layer = []
    for l in range(depth):
        po, so = _layer(xp, xs, cache_k[l], cache_v[l], cache_logf[l], state_pool[l], g_mix_pre[l], w_in[l],
                        b_f[l], w_attn_up[l], w_pool[l], pool_scale[l], w_out[l], g_mix_post[l],
                        g_ffn_pre[l], w_ff1[l], w_ff2[l], g_ffn_post[l])
        xp, xs = po[0], so[0]
        per_layer.append(po[1:] + so[1:])
    stacked = [jnp.stack(leaves, 0) for leaves in zip(*per_layer)]
    return (xp, xs, *stacked)
```

```python
import functools
import math

import jax
import jax.numpy as jnp
from jax import lax
from jax.experimental import pallas as pl
from jax.experimental.pallas import tpu as pltpu

N_HEADS = 16
HEAD_DIM = 64
ATTN_W = N_HEADS * HEAD_DIM
POOL_WINDOWS = (2, 4, 8, 16)
POOL_GROUPS = len(POOL_WINDOWS)
POOL_HIST = max(POOL_WINDOWS) - 1
EPS = 1e-6
SCALE = HEAD_DIM ** -0.5

LANES = 128
SUBLANES = 8
V7X_VMEM_BYTES = 64 * 1024 * 1024
VMEM_LIMIT_CAP = 60000 * 1024

HEADS_PER_BLOCK = LANES // HEAD_DIM
HIST_ROWS = 16

F32 = jnp.float32
BF16 = jnp.bfloat16


def _vmem_limit(nbytes):
    return int(min(VMEM_LIMIT_CAP, max(32 * 1024 * 1024, nbytes * 5 // 4)))


def _nbytes(shape, dtype):
    return math.prod(shape) * jnp.dtype(dtype).itemsize


def _rms(x):
    return x * lax.rsqrt(jnp.mean(x * x, axis=-1, keepdims=True) + EPS)


def _dot(a, b):
    return jnp.dot(a, b, preferred_element_type=F32)


def _dot_nt(a, b):
    return lax.dot_general(a, b, (((1,), (1,)), ((), ())), preferred_element_type=F32)


def _resident(shape):
    zeros = (0,) * len(shape)
    return pl.BlockSpec(shape, lambda *_: zeros, pipeline_mode=pl.Buffered(1))


def _repack_kernel(wt_ref, qkvu_ref, f_ref, ga_ref, gb_ref, *, off_f, off_u, off_ga, off_gb):
    nh = off_u - off_f
    p = off_ga - off_u
    d = off_gb - off_ga
    piece = lambda lo, hi: wt_ref[lo:hi, :].T.astype(BF16)
    qkvu_ref[:, 0:off_f] = piece(0, off_f)
    qkvu_ref[:, off_f:off_f + p] = piece(off_u, off_ga)
    fwin = wt_ref[off_f:off_f + LANES, :].T
    lane = lax.broadcasted_iota(jnp.int32, fwin.shape, 1)
    f_ref[...] = jnp.where(lane < nh, fwin, 0.0).astype(BF16)
    ga_ref[...] = piece(off_ga, off_gb)
    gb_ref[...] = piece(off_gb, off_gb + d)


def _repack_w_in(wt, *, off_f, off_u, off_ga, off_gb, tr):
    width, rows = wt.shape
    p = off_ga - off_u
    d = off_gb - off_ga
    blk = lambda cols: pl.BlockSpec((tr, cols), lambda i: (i, 0))
    est = 4 * _nbytes((width, tr), F32) + 2 * _nbytes((tr, off_f + p + LANES + 2 * d), BF16)
    return pl.pallas_call(
        functools.partial(_repack_kernel, off_f=off_f, off_u=off_u, off_ga=off_ga, off_gb=off_gb),
        grid=(rows // tr,),
        in_specs=[pl.BlockSpec((width, tr), lambda i: (0, i))],
        out_specs=(blk(off_f + p), blk(LANES), blk(d), blk(d)),
        out_shape=(jax.ShapeDtypeStruct((rows, off_f + p), BF16), jax.ShapeDtypeStruct((rows, LANES), BF16),
                   jax.ShapeDtypeStruct((rows, d), BF16), jax.ShapeDtypeStruct((rows, d), BF16)),
        compiler_params=pltpu.CompilerParams(
            dimension_semantics=("arbitrary",), vmem_limit_bytes=_vmem_limit(est)),
        name="repack_w_in",
    )(wt)


def _inproj_kernel(x_ref, g_ref, w_ref, wf_ref, bf_ref,
                   h_ref, q_ref, kb_ref, k_ref, v_ref, vb_ref, u_ref, f_ref, *, time_minor):
    h = (_rms(x_ref[...]) * g_ref[...]).astype(BF16)
    h_ref[...] = h
    a = ATTN_W
    q = (_dot(h, w_ref[:, 0:a]) * SCALE).astype(BF16)
    k = _dot(h, w_ref[:, a:2 * a])
    kb = k.astype(BF16)
    v = _dot(h, w_ref[:, 2 * a:3 * a])
    vb_ref[...] = v.astype(BF16)
    u_ref[...] = _dot(h, w_ref[:, 3 * a:])
    f = _dot(h, wf_ref[...]) + bf_ref[...]
    if time_minor:
        for c in range(a // LANES):
            q_ref[0, c] = q[:, c * LANES:(c + 1) * LANES]
            kb_ref[0, c] = kb[:, c * LANES:(c + 1) * LANES]
        k_ref[0] = k.T
        v_ref[0] = v.T
        f_ref[0] = jax.nn.log_sigmoid(f.T[0:f_ref.shape[1], :])
    else:
        q_ref[...] = q
        kb_ref[...] = kb
        k_ref[...] = k
        v_ref[...] = v
        f_ref[...] = jax.nn.log_sigmoid(f[:, 0:f_ref.shape[1]])


def _inproj(x, g, w, wf, bf, *, tm, seq, time_minor):
    n, d = x.shape
    a = ATTN_W
    p = w.shape[1] - 3 * a
    nh = N_HEADS
    row = lambda width: pl.BlockSpec((tm, width), lambda i: (i, 0))
    if time_minor:
        steps = seq // tm
        feat = lambda rows: (jax.ShapeDtypeStruct((n // seq, rows, seq), F32),
                             pl.BlockSpec((1, rows, tm), lambda i: (i // steps, 0, i % steps)))
        q_shape = jax.ShapeDtypeStruct((n // seq, a // LANES, seq, LANES), BF16)
        q_spec = pl.BlockSpec((1, a // LANES, tm, LANES), lambda i: (i // steps, 0, i % steps, 0))
    else:
        feat = lambda rows: (jax.ShapeDtypeStruct((n, rows), F32), row(rows))
        q_shape, q_spec = jax.ShapeDtypeStruct((n, a), BF16), row(a)
    (k_shape, k_spec), (f_shape, f_spec) = feat(a), feat(nh)
    out_shape = (
        jax.ShapeDtypeStruct((n, d), BF16),
        q_shape,
        q_shape,
        k_shape,
        k_shape,
        jax.ShapeDtypeStruct((n, a), BF16),
        jax.ShapeDtypeStruct((n, p), F32),
        f_shape,
    )
    est = (2 * _nbytes((tm, d), F32) + _nbytes(w.shape, BF16) + _nbytes(wf.shape, BF16)
           + 2 * (_nbytes((tm, d), BF16) + 3 * _nbytes((tm, a), BF16) + 2 * _nbytes((tm, a), F32)
                  + _nbytes((tm, p), F32) + _nbytes((tm, LANES), F32))
           + 4 * _nbytes((tm, a), F32))
    return pl.pallas_call(
        functools.partial(_inproj_kernel, time_minor=time_minor),
        grid=(n // tm,),
        in_specs=[row(d), _resident((1, d)), _resident(w.shape), _resident(wf.shape), _resident((1, LANES))],
        out_specs=(row(d), q_spec, q_spec, k_spec, k_spec, row(a), row(p), f_spec),
        out_shape=out_shape,
        compiler_params=pltpu.CompilerParams(
            dimension_semantics=("arbitrary",), vmem_limit_bytes=_vmem_limit(est)),
        name="inproj",
    )(x, g, w, wf, bf)


def _cumsum_kernel(x_ref, o_ref):
    x = x_ref[...]
    length = x.shape[1]
    lane = lax.broadcasted_iota(jnp.int32, x.shape, 1)
    shift = 1
    while shift < length:
        x = x + jnp.where(lane >= shift, pltpu.roll(x, shift, axis=1), 0.0)
        shift *= 2
    o_ref[...] = x


def _cumsum_lanes(x):
    rows, length = x.shape
    tr = _tile(rows, CUMSUM_ROWS)
    spec = pl.BlockSpec((tr, length), lambda i: (i, 0))
    return pl.pallas_call(
        _cumsum_kernel,
        grid=(rows // tr,),
        in_specs=[spec],
        out_specs=spec,
        out_shape=jax.ShapeDtypeStruct(x.shape, F32),
        compiler_params=pltpu.CompilerParams(dimension_semantics=("arbitrary",)),
        name="cumsum",
    )(x)


VALUE_ROWS = HEAD_DIM + 16


def _attn_kernel(q_ref, k_ref, vt_ref, c_ref, o_ref,
                 vta_ref, ckrep_ref, s_ref, p_ref, alpha_ref, m_ref, acc_ref, *, tq):
    t = k_ref.shape[2]
    nq = t // tq
    for a in range(HEADS_PER_BLOCK):
        vta_ref[a, 0:HEAD_DIM, :] = vt_ref[0, a * HEAD_DIM:(a + 1) * HEAD_DIM, :].astype(BF16)
        vta_ref[a, HEAD_DIM:VALUE_ROWS, :] = jnp.ones((VALUE_ROWS - HEAD_DIM, t), BF16)
        ckrep_ref[a] = jnp.broadcast_to(c_ref[0, 0, a:a + 1, :], (LANES, t)).T

    pairs = [(i, j) for i in range(nq) for j in range(i + 1)]
    rows = lambda j: slice(j * tq, (j + 1) * tq)

    def scores(w):
        i, j = pairs[w]
        q2 = q_ref[0, 0, rows(i), :]
        lane = lax.broadcasted_iota(jnp.int32, q2.shape, 1)
        kt = k_ref[0, 0, rows(j), :]
        for a in range(HEADS_PER_BLOCK):
            qa = jnp.where((lane // HEAD_DIM) == a, q2, jnp.zeros_like(q2))
            ckr = ckrep_ref[a, rows(j), :]
            s_ref[w % 2, a] = _dot_nt(kt, qa) - jnp.concatenate([ckr] * (tq // LANES), axis=1)

    def probs(w):
        i, j = pairs[w]
        for a in range(HEADS_PER_BLOCK):
            s = s_ref[w % 2, a]
            if j == i:
                r = lax.broadcasted_iota(jnp.int32, s.shape, 0)
                c = lax.broadcasted_iota(jnp.int32, s.shape, 1)
                s = jnp.where(r <= c, s, -jnp.inf)
            cqa = c_ref[0, 0, a:a + 1, rows(i)]
            smax = jnp.max(s, axis=0, keepdims=True) + cqa
            if j == 0:
                m_new = smax
            else:
                m_prev = m_ref[a]
                m_new = jnp.maximum(m_prev, smax)
                alpha_ref[w % 2, a] = jnp.exp(m_prev - m_new)
            p_ref[w % 2, a] = jnp.exp(s - (m_new - cqa)).astype(BF16)
            m_ref[a] = m_new

    def values(w):
        i, j = pairs[w]
        for a in range(HEADS_PER_BLOCK):
            pv = _dot(vta_ref[a, :, rows(j)], p_ref[w % 2, a])
            acc_ref[a] = pv if j == 0 else alpha_ref[w % 2, a] * acc_ref[a] + pv
        if j == i:
            ot = jnp.concatenate(
                [acc_ref[a, 0:HEAD_DIM, :] / acc_ref[a, HEAD_DIM:HEAD_DIM + 1, :]
                 for a in range(HEADS_PER_BLOCK)], axis=0)
            o_ref[0, 0, rows(i), :] = ot.T.astype(o_ref.dtype)

    n = len(pairs)
    scores(0)
    for w in range(n):
        if w + 1 < n:
            scores(w + 1)
        probs(w)
        if w >= 1:
            values(w - 1)
    values(n - 1)


def _attn_prompt(q, kb, vt, c_rows, *, tq):
    b, nblk, t, _ = q.shape
    spec = pl.BlockSpec((1, 1, t, LANES), lambda bi, hi: (bi, hi, 0, 0))
    return pl.pallas_call(
        functools.partial(_attn_kernel, tq=tq),
        grid=(b, nblk),
        in_specs=[spec, spec, pl.BlockSpec((1, LANES, t), lambda bi, hi: (bi, hi, 0)),
                  pl.BlockSpec((1, 1, HEADS_PER_BLOCK, t), lambda bi, hi: (bi, hi, 0, 0))],
        out_specs=spec,
        out_shape=jax.ShapeDtypeStruct((b, nblk, t, LANES), BF16),
        scratch_shapes=[pltpu.VMEM((HEADS_PER_BLOCK, VALUE_ROWS, t), BF16),
                        pltpu.VMEM((HEADS_PER_BLOCK, t, LANES), F32),
                        pltpu.VMEM((2, HEADS_PER_BLOCK, tq, tq), F32),
                        pltpu.VMEM((2, HEADS_PER_BLOCK, tq, tq), BF16),
                        pltpu.VMEM((2, HEADS_PER_BLOCK, 1, tq), F32),
                        pltpu.VMEM((HEADS_PER_BLOCK, 1, tq), F32),
                        pltpu.VMEM((HEADS_PER_BLOCK, VALUE_ROWS, tq), F32)],
        compiler_params=pltpu.CompilerParams(dimension_semantics=("arbitrary", "arbitrary")),
        name="attn_prompt",
    )(q, kb, vt, c_rows)


def _attn_sample_kernel(q_ref, kct_ref, vct_ref, kn_ref, vn_ref, cc_ref, cnr_ref, cnc_ref, o_ref,
                        kb_ref, vb_ref, s_ref, p_ref, pn_ref):
    n, a = q_ref.shape[1], q_ref.shape[2]
    nh = a // HEAD_DIM
    q = q_ref[0]
    qt = jnp.concatenate([q] * nh, axis=0)
    row_h = lax.broadcasted_iota(jnp.int32, qt.shape, 0) // n
    col_h = lax.broadcasted_iota(jnp.int32, qt.shape, 1) // HEAD_DIM
    qbd = jnp.where(row_h == col_h, qt, jnp.zeros_like(qt))
    kb_ref[...] = kct_ref[0].astype(BF16)
    vb_ref[...] = vct_ref[0].astype(BF16)
    s_ref[...] = _dot(qbd, kb_ref[...])
    s_new = _dot_nt(qbd, kn_ref[0])
    cnc = cnc_ref[0]
    r = lax.broadcasted_iota(jnp.int32, (n, n), 0)
    c = lax.broadcasted_iota(jnp.int32, (n, n), 1)
    inv_l = []
    for h in range(nh):
        rows = slice(h * n, (h + 1) * n)
        cq = cnc[:, h:h + 1]
        sc = s_ref[rows, :] + (cq - cc_ref[0, h:h + 1, :])
        sn = s_new[rows, :] + (cq - cnr_ref[0, h:h + 1, :])
        sn = jnp.where(c <= r, sn, -jnp.inf)
        m = jnp.maximum(jnp.max(sc, axis=1, keepdims=True), jnp.max(sn, axis=1, keepdims=True))
        pc = jnp.exp(sc - m)
        pn = jnp.exp(sn - m)
        inv_l.append(1.0 / (jnp.sum(pc, axis=1, keepdims=True) + jnp.sum(pn, axis=1, keepdims=True)))
        p_ref[rows, :] = pc.astype(BF16)
        pn_ref[rows, :] = pn.astype(BF16)
    o = _dot_nt(p_ref[...], vb_ref[...]) + _dot(pn_ref[...], vn_ref[0])
    for h in range(nh):
        rows = slice(h * n, (h + 1) * n)
        cols = slice(h * HEAD_DIM, (h + 1) * HEAD_DIM)
        o_ref[0, :, cols] = (o[rows, cols] * inv_l[h]).astype(o_ref.dtype)


def _attn_sample(q, kct, vct, kn, vn, cc, cnr, cnc):
    b, n, a = q.shape
    past = kct.shape[2]
    nh = a // HEAD_DIM
    per_b = lambda *tail: pl.BlockSpec((1,) + tail, lambda bi: (bi,) + (0,) * len(tail))
    est = (4 * _nbytes((a, past), F32) + 2 * _nbytes((a, past), BF16)
           + _nbytes((nh * n, past), F32) + _nbytes((nh * n, past), BF16) + 4 * _nbytes((nh * n, a), F32))
    return pl.pallas_call(
        _attn_sample_kernel,
        grid=(b,),
        in_specs=[per_b(n, a), per_b(a, past), per_b(a, past), per_b(n, a), per_b(n, a),
                  per_b(nh, past), per_b(nh, n), per_b(n, nh)],
        out_specs=per_b(n, a),
        out_shape=jax.ShapeDtypeStruct((b, n, a), BF16),
        scratch_shapes=[pltpu.VMEM((a, past), BF16), pltpu.VMEM((a, past), BF16),
                        pltpu.VMEM((nh * n, past), F32), pltpu.VMEM((nh * n, past), BF16),
                        pltpu.VMEM((nh * n, n), BF16)],
        compiler_params=pltpu.CompilerParams(
            dimension_semantics=("arbitrary",), vmem_limit_bytes=_vmem_limit(est)),
        name="attn_sample",
    )(q, kct, vct, kn, vn, cc, cnr, cnc)


def _merge_kernel(a_ref, h_ref, u_ref, hist_ref, x_ref, wup_ref, wga_ref, wgb_ref, wpool_ref, ps_ref,
                  wout_ref, gpost_ref, gpre_ref, x1_ref, h2_ref, ext_ref, m_ref,
                  *, pos0, zero_first):
    nseg, tl, _ = u_ref.shape
    i = pl.program_id(1)
    hist = hist_ref[...]
    if zero_first:
        hist = jnp.where(i == 0, 0.0, hist)
    ext_ref[:, 0:HIST_ROWS, :] = hist
    ext_ref[:, HIST_ROWS:HIST_ROWS + tl, :] = u_ref[...]
    gw = u_ref.shape[2] // POOL_GROUPS
    ogw = wpool_ref.shape[2]
    pos = pos0 + i * tl + lax.broadcasted_iota(jnp.int32, (nseg, tl, gw), 1)
    a = jnp.concatenate([a_ref[:, c].reshape(nseg * tl, LANES) for c in range(a_ref.shape[1])], axis=1)
    h = h_ref[...]
    for g, w in enumerate(POOL_WINDOWS):
        cs = slice(g * gw, (g + 1) * gw)
        cur = ext_ref[:, HIST_ROWS:HIST_ROWS + tl, cs]
        tot = cur
        for s in range(1, w):
            tot = tot + ext_ref[:, HIST_ROWS - s:HIST_ROWS - s + tl, cs]
        cnt = jnp.minimum(pos + 1, w).astype(F32)
        pooled = (tot / cnt - cur).reshape(nseg * tl, gw).astype(BF16)
        os_ = slice(g * ogw, (g + 1) * ogw)
        br_b = _dot(pooled, wpool_ref[g]) * ps_ref[:, os_]
        br_a = _dot(a, wup_ref[:, os_])
        ga = _dot(h, wga_ref[:, os_])
        gb = _dot(h, wgb_ref[:, os_])
        m_ref[:, os_] = (jax.nn.sigmoid(ga) * br_a + jax.nn.sigmoid(gb) * br_b).astype(BF16)
    x1 = x_ref[...] + _rms(_dot(m_ref[...], wout_ref[...])) * gpost_ref[...]
    x1_ref[...] = x1
    h2_ref[...] = (_rms(x1) * gpre_ref[...]).astype(BF16)


def _merge(a, h, u3, hist, hist_map, x, wup, wga, wgb, wpool, ps, wout, gpost, gpre,
           *, nseg, tl, pos0, zero_first):
    s_total, l_total, p = u3.shape
    n, d = x.shape
    tm = nseg * tl
    steps = l_total // tl
    assert nseg == 1 or steps == 1
    row = lambda width: pl.BlockSpec((tm, width), lambda s, i: (s * steps + i, 0))
    nblk = a.shape[1]
    est = (_nbytes(wup.shape, BF16) + 2 * _nbytes(wga.shape, BF16) + _nbytes(wpool.shape, BF16)
           + _nbytes(wout.shape, BF16)
           + 2 * (_nbytes((tm, nblk * LANES), BF16) + 2 * _nbytes((tm, d), BF16) + 2 * _nbytes((tm, d), F32)
                  + _nbytes((tm + HIST_ROWS, p), F32))
           + _nbytes((tm + nseg * HIST_ROWS, p), F32) + _nbytes((tm, d), BF16) + 6 * _nbytes((tm, d), F32))
    return pl.pallas_call(
        functools.partial(_merge_kernel, pos0=pos0, zero_first=zero_first),
        grid=(s_total // nseg, steps),
        in_specs=[pl.BlockSpec((nseg, nblk, tl, LANES), lambda s, i: (s, 0, i, 0)), row(d),
                  pl.BlockSpec((nseg, tl, p), lambda s, i: (s, i, 0)),
                  pl.BlockSpec((nseg, HIST_ROWS, p), hist_map),
                  row(d),
                  _resident(wup.shape), _resident(wga.shape), _resident(wgb.shape), _resident(wpool.shape),
                  _resident((1, d)), _resident(wout.shape), _resident((1, d)), _resident((1, d))],
        out_specs=(row(d), row(d)),
        out_shape=(jax.ShapeDtypeStruct((n, d), F32), jax.ShapeDtypeStruct((n, d), BF16)),
        scratch_shapes=[pltpu.VMEM((nseg, HIST_ROWS + tl, p), F32), pltpu.VMEM((tm, d), BF16)],
        compiler_params=pltpu.CompilerParams(
            dimension_semantics=("arbitrary", "arbitrary"), vmem_limit_bytes=_vmem_limit(est)),
        name="merge",
    )(a, h, u3, hist, x, wup, wga, wgb, wpool, ps, wout, gpost, gpre)


def _ffn_kernel(h_ref, x1_ref, w1_ref, w2_ref, g_ref, o_ref):
    f = pl.program_id(1)

    @pl.when(f == 0)
    def _():
        o_ref[...] = jnp.zeros_like(o_ref)

    z = jnp.square(jnp.maximum(_dot(h_ref[...], w1_ref[0]), 0.0)).astype(BF16)
    o_ref[...] += _dot(z, w2_ref[...])

    @pl.when(f == pl.num_programs(1) - 1)
    def _():
        o_ref[...] = x1_ref[...] + _rms(o_ref[...]) * g_ref[...]


def _ffn(h2, x1, w1, w2, g, *, tm, tf):
    n, d = x1.shape
    dff = w2.shape[0]
    est = (2 * _nbytes((tm, d), BF16) + 4 * _nbytes((tm, d), F32)
           + 4 * _nbytes((d, tf), BF16) + 2 * _nbytes((tm, tf), F32) + _nbytes((tm, d), F32))
    return pl.pallas_call(
        _ffn_kernel,
        grid=(n // tm, dff // tf),
        in_specs=[pl.BlockSpec((tm, d), lambda i, f: (i, 0)),
                  pl.BlockSpec((tm, d), lambda i, f: (i, 0)),
                  pl.BlockSpec((1, d, tf), lambda i, f: (f, 0, 0)),
                  pl.BlockSpec((tf, d), lambda i, f: (f, 0)),
                  pl.BlockSpec((1, d), lambda i, f: (0, 0))],
        out_specs=pl.BlockSpec((tm, d), lambda i, f: (i, 0)),
        out_shape=jax.ShapeDtypeStruct((n, d), F32),
        compiler_params=pltpu.CompilerParams(
            dimension_semantics=("arbitrary", "arbitrary"), vmem_limit_bytes=_vmem_limit(est)),
        name="ffn",
    )(h2, x1, w1, w2, g)


def _tile(n, pref):
    t = min(n, pref)
    while n % t:
        t //= 2
    return t


INPROJ_TM = 256
ATTN_TQ = 512
MERGE_TM = 256
FFN_TM = 512
FFN_TF = 1024
CUMSUM_ROWS = 32
REPACK_ROWS = 256


def _cum_logf_rows(logf_bht):
    b, h, t = logf_bht.shape
    rows = logf_bht.reshape(b * h, t)
    pad = (-t) % LANES
    if pad:
        rows = jnp.pad(rows, ((0, 0), (0, pad)))
    return _cumsum_lanes(rows)


def _time_minor(x, nbatch):
    return jnp.transpose(x, (0, 2, 3, 1)).reshape(nbatch, -1, x.shape[1])


def _layer(xp, xs, cache_k, cache_v, cache_logf, state_pool, g_mix_pre, w_in, b_f, w_attn_up, w_pool,
           pool_scale, w_out, g_mix_post, g_ffn_pre, w_ff1, w_ff2, g_ffn_post):
    bp, tp, d = xp.shape
    bs, ts, _ = xs.shape
    past = cache_k.shape[1]
    a, nh, p = ATTN_W, N_HEADS, w_pool.shape[0] * w_pool.shape[1]
    off_f = 3 * a
    off_u = off_f + nh
    off_ga = off_u + p
    off_gb = off_ga + d

    w_qkvu, w_f, w_ga, w_gb = _repack_w_in(jnp.transpose(w_in.astype(F32)), off_f=off_f, off_u=off_u,
                                           off_ga=off_ga, off_gb=off_gb, tr=_tile(w_in.shape[0], REPACK_ROWS))
    b_f2 = jnp.pad(b_f.reshape(1, -1), ((0, 0), (0, LANES - nh)))
    w_up = w_attn_up.astype(BF16)
    w_pl = w_pool.astype(BF16)
    w_o = w_out.astype(BF16)
    tf = _tile(w_ff1.shape[1], FFN_TF)
    w1 = jnp.transpose(w_ff1.astype(BF16).reshape(d, -1, tf), (1, 0, 2))
    w2 = w_ff2.astype(BF16)
    row = lambda v: v.reshape(1, -1)
    g_pre, ps = row(g_mix_pre), row(pool_scale)
    g_post, g_fpre, g_fpost = row(g_mix_post), row(g_ffn_pre), row(g_ffn_post)

    def project(x3, time_minor):
        x2 = x3.reshape(-1, d)
        tm = _tile(x3.shape[1] if time_minor else x2.shape[0], INPROJ_TM)
        return x2, _inproj(x2, g_pre, w_qkvu, w_f, b_f2, tm=tm, seq=x3.shape[1], time_minor=time_minor)

    def finish(x2, h, a_out, u3, hist, hist_map, nseg, tl, pos0, zero_first):
        x1, h2 = _merge(a_out, h, u3, hist, hist_map, x2, w_up, w_ga, w_gb, w_pl, ps, w_o, g_post, g_fpre,
                        nseg=nseg, tl=tl, pos0=pos0, zero_first=zero_first)
        n = x2.shape[0]
        return _ffn(h2, x1, w1, w2, g_fpost, tm=_tile(n, FFN_TM), tf=tf)

    x2, (h, q, kb, kt, vt, _, u, logft) = project(xp, True)
    c_rows = _cum_logf_rows(logft)
    nblk = nh // HEADS_PER_BLOCK
    a_out = _attn_prompt(q, kb, vt, c_rows.reshape(bp, nblk, HEADS_PER_BLOCK, tp), tq=_tile(tp, ATTN_TQ))
    to_bthd = lambda zt: jnp.transpose(zt.reshape(bp, nh, HEAD_DIM, tp), (0, 3, 1, 2))
    u3 = u.reshape(bp, tp, p)
    tl = _tile(tp, MERGE_TM)
    blocks_per_tile = tl // HIST_ROWS
    hist_map = lambda s, i: (s, jnp.maximum(i * blocks_per_tile - 1, 0), 0)
    yp = finish(x2, h, a_out, u3, u3, hist_map, 1, tl, 0, True).reshape(bp, tp, d)
    prompt_out = (yp, to_bthd(kt), to_bthd(vt), jnp.transpose(logft, (0, 2, 1)),
                  jnp.concatenate([jnp.zeros((bp, POOL_HIST, p), F32), u3], axis=1)[:, -POOL_HIST:])

    x2, (h, q, kb, k, v, vb, u, logf) = project(xs, False)
    logf3 = logf.reshape(bs, ts, nh)
    f_all = jnp.concatenate([cache_logf.astype(F32), logf3], axis=1)
    c_all = _cum_logf_rows(jnp.transpose(f_all, (0, 2, 1))).reshape(bs, nh, -1)
    c_cache = c_all[:, :, :past]
    c_new = c_all[:, :, past:past + ts]
    three = lambda z: z.reshape(bs, ts, a)
    a_out = _attn_sample(three(q), _time_minor(cache_k.astype(F32), bs), _time_minor(cache_v.astype(F32), bs),
                         three(kb), three(vb), c_cache, c_new, jnp.transpose(c_new, (0, 2, 1)))
    u3 = u.reshape(bs, ts, p)
    hist = jnp.pad(state_pool.astype(F32), ((0, 0), (HIST_ROWS - POOL_HIST, 0), (0, 0)))
    a_blocks = jnp.transpose(a_out.reshape(bs, ts, nblk, LANES), (0, 2, 1, 3))
    ys = finish(x2, h, a_blocks, u3, hist, lambda s, i: (s, 0, 0), bs, ts, past, False).reshape(bs, ts, d)
    sample_out = (ys, k.reshape(bs, ts, nh, HEAD_DIM), v.reshape(bs, ts, nh, HEAD_DIM), logf3,
                  jnp.concatenate([state_pool.astype(F32), u3], axis=1)[:, -POOL_HIST:])
    return prompt_out, sample_out


def kernel(x_prompt, x_sample, cache_k, cache_v, cache_logf, state_pool, g_mix_pre, w_in, b_f, w_attn_up,
           w_pool, pool_scale, w_out, g_mix_post, g_ffn_pre, w_ff1, w_ff2, g_ffn_post):
    depth = w_in.shape[0]
    xp, xs = x_prompt, x_sample
    per_layer = []
    for l in range(depth):
        po, so = _layer(xp, xs, cache_k[l], cache_v[l], cache_logf[l], state_pool[l], g_mix_pre[l], w_in[l],
                        b_f[l], w_attn_up[l], w_pool[l], pool_scale[l], w_out[l], g_mix_post[l],
                        g_ffn_pre[l], w_ff1[l], w_ff2[l], g_ffn_post[l])
        xp, xs = po[0], so[0]
        per_layer.append(po[1:] + so[1:])
    stacked = [jnp.stack(leaves, 0) for leaves in zip(*per_layer)]
    return (xp, xs, *stacked)
```

```python
import functools
import math

import jax
import jax.numpy as jnp
from jax import lax
from jax.experimental import pallas as pl
from jax.experimental.pallas import tpu as pltpu

N_HEADS = 16
HEAD_DIM = 64
ATTN_W = N_HEADS * HEAD_DIM
POOL_WINDOWS = (2, 4, 8, 16)
POOL_GROUPS = len(POOL_WINDOWS)
POOL_HIST = max(POOL_WINDOWS) - 1
EPS = 1e-6
SCALE = HEAD_DIM ** -0.5

LANES = 128
SUBLANES = 8
V7X_VMEM_BYTES = 64 * 1024 * 1024
VMEM_LIMIT_CAP = 60000 * 1024

HEADS_PER_BLOCK = LANES // HEAD_DIM
HIST_ROWS = 16

F32 = jnp.float32
BF16 = jnp.bfloat16


def _vmem_limit(nbytes):
    return int(min(VMEM_LIMIT_CAP, max(32 * 1024 * 1024, nbytes * 5 // 4)))


def _nbytes(shape, dtype):
    return math.prod(shape) * jnp.dtype(dtype).itemsize


def _rms(x):
    return x * lax.rsqrt(jnp.mean(x * x, axis=-1, keepdims=True) + EPS)


def _dot(a, b):
    return jnp.dot(a, b, preferred_element_type=F32)


def _dot_nt(a, b):
    return lax.dot_general(a, b, (((1,), (1,)), ((), ())), preferred_element_type=F32)


def _resident(shape):
    zeros = (0,) * len(shape)
    return pl.BlockSpec(shape, lambda *_: zeros, pipeline_mode=pl.Buffered(1))


def _repack_kernel(wt_ref, qkvu_ref, f_ref, ga_ref, gb_ref, *, off_f, off_u, off_ga, off_gb):
    nh = off_u - off_f
    p = off_ga - off_u
    d = off_gb - off_ga
    piece = lambda lo, hi: wt_ref[lo:hi, :].T.astype(BF16)
    qkvu_ref[:, 0:off_f] = piece(0, off_f)
    qkvu_ref[:, off_f:off_f + p] = piece(off_u, off_ga)
    fwin = wt_ref[off_f:off_f + LANES, :].T
    lane = lax.broadcasted_iota(jnp.int32, fwin.shape, 1)
    f_ref[...] = jnp.where(lane < nh, fwin, 0.0).astype(BF16)
    ga_ref[...] = piece(off_ga, off_gb)
    gb_ref[...] = piece(off_gb, off_gb + d)


def _repack_w_in(wt, *, off_f, off_u, off_ga, off_gb, tr):
    width, rows = wt.shape
    p = off_ga - off_u
    d = off_gb - off_ga
    blk = lambda cols: pl.BlockSpec((tr, cols), lambda i: (i, 0))
    est = 4 * _nbytes((width, tr), F32) + 2 * _nbytes((tr, off_f + p + LANES + 2 * d), BF16)
    return pl.pallas_call(
        functools.partial(_repack_kernel, off_f=off_f, off_u=off_u, off_ga=off_ga, off_gb=off_gb),
        grid=(rows // tr,),
        in_specs=[pl.BlockSpec((width, tr), lambda i: (0, i))],
        out_specs=(blk(off_f + p), blk(LANES), blk(d), blk(d)),
        out_shape=(jax.ShapeDtypeStruct((rows, off_f + p), BF16), jax.ShapeDtypeStruct((rows, LANES), BF16),
                   jax.ShapeDtypeStruct((rows, d), BF16), jax.ShapeDtypeStruct((rows, d), BF16)),
        compiler_params=pltpu.CompilerParams(
            dimension_semantics=("arbitrary",), vmem_limit_bytes=_vmem_limit(est)),
        name="repack_w_in",
    )(wt)


def _inproj_kernel(x_ref, g_ref, w_ref, wf_ref, bf_ref,
                   h_ref, q_ref, kb_ref, k_ref, v_ref, vb_ref, u_ref, f_ref, *, time_minor):
    h = (_rms(x_ref[...]) * g_ref[...]).astype(BF16)
    h_ref[...] = h
    a = q_ref.shape[1]
    q_ref[...] = (_dot(h, w_ref[:, 0:a]) * SCALE).astype(BF16)
    k = _dot(h, w_ref[:, a:2 * a])
    kb_ref[...] = k.astype(BF16)
    v = _dot(h, w_ref[:, 2 * a:3 * a])
    vb_ref[...] = v.astype(BF16)
    u_ref[...] = _dot(h, w_ref[:, 3 * a:])
    f = _dot(h, wf_ref[...]) + bf_ref[...]
    if time_minor:
        k_ref[0] = k.T
        v_ref[0] = v.T
        f_ref[0] = jax.nn.log_sigmoid(f.T[0:f_ref.shape[1], :])
    else:
        k_ref[...] = k
        v_ref[...] = v
        f_ref[...] = jax.nn.log_sigmoid(f[:, 0:f_ref.shape[1]])


def _inproj(x, g, w, wf, bf, *, tm, seq, time_minor):
    n, d = x.shape
    a = ATTN_W
    p = w.shape[1] - 3 * a
    nh = N_HEADS
    row = lambda width: pl.BlockSpec((tm, width), lambda i: (i, 0))
    if time_minor:
        steps = seq // tm
        feat = lambda rows: (jax.ShapeDtypeStruct((n // seq, rows, seq), F32),
                             pl.BlockSpec((1, rows, tm), lambda i: (i // steps, 0, i % steps)))
    else:
        feat = lambda rows: (jax.ShapeDtypeStruct((n, rows), F32), row(rows))
    (k_shape, k_spec), (f_shape, f_spec) = feat(a), feat(nh)
    out_shape = (
        jax.ShapeDtypeStruct((n, d), BF16),
        jax.ShapeDtypeStruct((n, a), BF16),
        jax.ShapeDtypeStruct((n, a), BF16),
        k_shape,
        k_shape,
        jax.ShapeDtypeStruct((n, a), BF16),
        jax.ShapeDtypeStruct((n, p), F32),
        f_shape,
    )
    est = (2 * _nbytes((tm, d), F32) + _nbytes(w.shape, BF16) + _nbytes(wf.shape, BF16)
           + 2 * (_nbytes((tm, d), BF16) + 3 * _nbytes((tm, a), BF16) + 2 * _nbytes((tm, a), F32)
                  + _nbytes((tm, p), F32) + _nbytes((tm, LANES), F32))
           + 4 * _nbytes((tm, a), F32))
    return pl.pallas_call(
        functools.partial(_inproj_kernel, time_minor=time_minor),
        grid=(n // tm,),
        in_specs=[row(d), _resident((1, d)), _resident(w.shape), _resident(wf.shape), _resident((1, LANES))],
        out_specs=(row(d), row(a), row(a), k_spec, k_spec, row(a), row(p), f_spec),
        out_shape=out_shape,
        compiler_params=pltpu.CompilerParams(
            dimension_semantics=("arbitrary",), vmem_limit_bytes=_vmem_limit(est)),
        name="inproj",
    )(x, g, w, wf, bf)


def _cumsum_kernel(x_ref, o_ref):
    x = x_ref[...]
    length = x.shape[1]
    lane = lax.broadcasted_iota(jnp.int32, x.shape, 1)
    shift = 1
    while shift < length:
        x = x + jnp.where(lane >= shift, pltpu.roll(x, shift, axis=1), 0.0)
        shift *= 2
    o_ref[...] = x


def _cumsum_lanes(x):
    rows, length = x.shape
    tr = _tile(rows, CUMSUM_ROWS)
    spec = pl.BlockSpec((tr, length), lambda i: (i, 0))
    return pl.pallas_call(
        _cumsum_kernel,
        grid=(rows // tr,),
        in_specs=[spec],
        out_specs=spec,
        out_shape=jax.ShapeDtypeStruct(x.shape, F32),
        compiler_params=pltpu.CompilerParams(dimension_semantics=("arbitrary",)),
        name="cumsum",
    )(x)


VALUE_ROWS = HEAD_DIM + 16


def _attn_kernel(q_ref, k_ref, vt_ref, c_ref, o_ref,
                 vta_ref, ckrep_ref, s_ref, p_ref, alpha_ref, m_ref, acc_ref, *, tq):
    t = k_ref.shape[1]
    nq = t // tq
    for a in range(HEADS_PER_BLOCK):
        vta_ref[a, 0:HEAD_DIM, :] = vt_ref[0, a * HEAD_DIM:(a + 1) * HEAD_DIM, :].astype(BF16)
        vta_ref[a, HEAD_DIM:VALUE_ROWS, :] = jnp.ones((VALUE_ROWS - HEAD_DIM, t), BF16)
        ckrep_ref[a] = jnp.broadcast_to(c_ref[0, 0, a:a + 1, :], (LANES, t)).T

    pairs = [(i, j) for i in range(nq) for j in range(i + 1)]
    rows = lambda j: slice(j * tq, (j + 1) * tq)

    def scores(w):
        i, j = pairs[w]
        q2 = q_ref[0, rows(i), :]
        lane = lax.broadcasted_iota(jnp.int32, q2.shape, 1)
        kt = k_ref[0, rows(j), :]
        for a in range(HEADS_PER_BLOCK):
            qa = jnp.where((lane // HEAD_DIM) == a, q2, jnp.zeros_like(q2))
            ckr = ckrep_ref[a, rows(j), :]
            s_ref[w % 2, a] = _dot_nt(kt, qa) - jnp.concatenate([ckr] * (tq // LANES), axis=1)

    def probs(w):
        i, j = pairs[w]
        for a in range(HEADS_PER_BLOCK):
            s = s_ref[w % 2, a]
            if j == i:
                r = lax.broadcasted_iota(jnp.int32, s.shape, 0)
                c = lax.broadcasted_iota(jnp.int32, s.shape, 1)
                s = jnp.where(r <= c, s, -jnp.inf)
            cqa = c_ref[0, 0, a:a + 1, rows(i)]
            smax = jnp.max(s, axis=0, keepdims=True) + cqa
            if j == 0:
                m_new = smax
            else:
                m_prev = m_ref[a]
                m_new = jnp.maximum(m_prev, smax)
                alpha_ref[w % 2, a] = jnp.exp(m_prev - m_new)
            p_ref[w % 2, a] = jnp.exp(s - (m_new - cqa)).astype(BF16)
            m_ref[a] = m_new

    def values(w):
        i, j = pairs[w]
        for a in range(HEADS_PER_BLOCK):
            pv = _dot(vta_ref[a, :, rows(j)], p_ref[w % 2, a])
            acc_ref[a] = pv if j == 0 else alpha_ref[w % 2, a] * acc_ref[a] + pv
        if j == i:
            ot = jnp.concatenate(
                [acc_ref[a, 0:HEAD_DIM, :] / acc_ref[a, HEAD_DIM:HEAD_DIM + 1, :]
                 for a in range(HEADS_PER_BLOCK)], axis=0)
            o_ref[0, rows(i), :] = ot.T.astype(o_ref.dtype)

    n = len(pairs)
    scores(0)
    for w in range(n):
        if w + 1 < n:
            scores(w + 1)
        probs(w)
        if w >= 1:
            values(w - 1)
    values(n - 1)


def _attn_prompt(q, kb, vt, c_rows, *, tq):
    b, t, a = q.shape
    nblk = a // LANES
    spec = pl.BlockSpec((1, t, LANES), lambda bi, hi: (bi, 0, hi))
    return pl.pallas_call(
        functools.partial(_attn_kernel, tq=tq),
        grid=(b, nblk),
        in_specs=[spec, spec, pl.BlockSpec((1, LANES, t), lambda bi, hi: (bi, hi, 0)),
                  pl.BlockSpec((1, 1, HEADS_PER_BLOCK, t), lambda bi, hi: (bi, hi, 0, 0))],
        out_specs=spec,
        out_shape=jax.ShapeDtypeStruct((b, t, a), BF16),
        scratch_shapes=[pltpu.VMEM((HEADS_PER_BLOCK, VALUE_ROWS, t), BF16),
                        pltpu.VMEM((HEADS_PER_BLOCK, t, LANES), F32),
                        pltpu.VMEM((2, HEADS_PER_BLOCK, tq, tq), F32),
                        pltpu.VMEM((2, HEADS_PER_BLOCK, tq, tq), BF16),
                        pltpu.VMEM((2, HEADS_PER_BLOCK, 1, tq), F32),
                        pltpu.VMEM((HEADS_PER_BLOCK, 1, tq), F32),
                        pltpu.VMEM((HEADS_PER_BLOCK, VALUE_ROWS, tq), F32)],
        compiler_params=pltpu.CompilerParams(dimension_semantics=("arbitrary", "arbitrary")),
        name="attn_prompt",
    )(q, kb, vt, c_rows)


def _attn_sample_kernel(q_ref, kct_ref, vct_ref, kn_ref, vn_ref, cc_ref, cnr_ref, cnc_ref, o_ref,
                        kb_ref, vb_ref, s_ref, p_ref, pn_ref):
    n, a = q_ref.shape[1], q_ref.shape[2]
    nh = a // HEAD_DIM
    q = q_ref[0]
    qt = jnp.concatenate([q] * nh, axis=0)
    row_h = lax.broadcasted_iota(jnp.int32, qt.shape, 0) // n
    col_h = lax.broadcasted_iota(jnp.int32, qt.shape, 1) // HEAD_DIM
    qbd = jnp.where(row_h == col_h, qt, jnp.zeros_like(qt))
    kb_ref[...] = kct_ref[0].astype(BF16)
    vb_ref[...] = vct_ref[0].astype(BF16)
    s_ref[...] = _dot(qbd, kb_ref[...])
    s_new = _dot_nt(qbd, kn_ref[0])
    cnc = cnc_ref[0]
    r = lax.broadcasted_iota(jnp.int32, (n, n), 0)
    c = lax.broadcasted_iota(jnp.int32, (n, n), 1)
    inv_l = []
    for h in range(nh):
        rows = slice(h * n, (h + 1) * n)
        cq = cnc[:, h:h + 1]
        sc = s_ref[rows, :] + (cq - cc_ref[0, h:h + 1, :])
        sn = s_new[rows, :] + (cq - cnr_ref[0, h:h + 1, :])
        sn = jnp.where(c <= r, sn, -jnp.inf)
        m = jnp.maximum(jnp.max(sc, axis=1, keepdims=True), jnp.max(sn, axis=1, keepdims=True))
        pc = jnp.exp(sc - m)
        pn = jnp.exp(sn - m)
        inv_l.append(1.0 / (jnp.sum(pc, axis=1, keepdims=True) + jnp.sum(pn, axis=1, keepdims=True)))
        p_ref[rows, :] = pc.astype(BF16)
        pn_ref[rows, :] = pn.astype(BF16)
    o = _dot_nt(p_ref[...], vb_ref[...]) + _dot(pn_ref[...], vn_ref[0])
    for h in range(nh):
        rows = slice(h * n, (h + 1) * n)
        cols = slice(h * HEAD_DIM, (h + 1) * HEAD_DIM)
        o_ref[0, :, cols] = (o[rows, cols] * inv_l[h]).astype(o_ref.dtype)


def _attn_sample(q, kct, vct, kn, vn, cc, cnr, cnc):
    b, n, a = q.shape
    past = kct.shape[2]
    nh = a // HEAD_DIM
    per_b = lambda *tail: pl.BlockSpec((1,) + tail, lambda bi: (bi,) + (0,) * len(tail))
    est = (4 * _nbytes((a, past), F32) + 2 * _nbytes((a, past), BF16)
           + _nbytes((nh * n, past), F32) + _nbytes((nh * n, past), BF16) + 4 * _nbytes((nh * n, a), F32))
    return pl.pallas_call(
        _attn_sample_kernel,
        grid=(b,),
        in_specs=[per_b(n, a), per_b(a, past), per_b(a, past), per_b(n, a), per_b(n, a),
                  per_b(nh, past), per_b(nh, n), per_b(n, nh)],
        out_specs=per_b(n, a),
        out_shape=jax.ShapeDtypeStruct((b, n, a), BF16),
        scratch_shapes=[pltpu.VMEM((a, past), BF16), pltpu.VMEM((a, past), BF16),
                        pltpu.VMEM((nh * n, past), F32), pltpu.VMEM((nh * n, past), BF16),
                        pltpu.VMEM((nh * n, n), BF16)],
        compiler_params=pltpu.CompilerParams(
            dimension_semantics=("arbitrary",), vmem_limit_bytes=_vmem_limit(est)),
        name="attn_sample",
    )(q, kct, vct, kn, vn, cc, cnr, cnc)


def _merge_kernel(a_ref, h_ref, u_ref, hist_ref, x_ref, wup_ref, wga_ref, wgb_ref, wpool_ref, ps_ref,
                  wout_ref, gpost_ref, gpre_ref, *rest, pos0, zero_first, n_cast):
    cast_in, (x1_ref, h2_ref), cast_out = rest[:n_cast], rest[n_cast:n_cast + 2], rest[n_cast + 2:2 * n_cast + 2]
    ext_ref, m_ref = rest[2 * n_cast + 2:]
    for src, dst in zip(cast_in, cast_out):
        dst[...] = src[...].astype(dst.dtype)
    nseg, tl, _ = u_ref.shape
    i = pl.program_id(1)
    hist = hist_ref[...]
    if zero_first:
        hist = jnp.where(i == 0, 0.0, hist)
    ext_ref[:, 0:HIST_ROWS, :] = hist
    ext_ref[:, HIST_ROWS:HIST_ROWS + tl, :] = u_ref[...]
    gw = u_ref.shape[2] // POOL_GROUPS
    ogw = wpool_ref.shape[2]
    pos = pos0 + i * tl + lax.broadcasted_iota(jnp.int32, (nseg, tl, gw), 1)
    a = a_ref[...]
    h = h_ref[...]
    for g, w in enumerate(POOL_WINDOWS):
        cs = slice(g * gw, (g + 1) * gw)
        cur = ext_ref[:, HIST_ROWS:HIST_ROWS + tl, cs]
        tot = cur
        for s in range(1, w):
            tot = tot + ext_ref[:, HIST_ROWS - s:HIST_ROWS - s + tl, cs]
        cnt = jnp.minimum(pos + 1, w).astype(F32)
        pooled = (tot / cnt - cur).reshape(nseg * tl, gw).astype(BF16)
        os_ = slice(g * ogw, (g + 1) * ogw)
        br_b = _dot(pooled, wpool_ref[g]) * ps_ref[:, os_]
        br_a = _dot(a, wup_ref[:, os_])
        ga = _dot(h, wga_ref[:, os_])
        gb = _dot(h, wgb_ref[:, os_])
        m_ref[:, os_] = (jax.nn.sigmoid(ga) * br_a + jax.nn.sigmoid(gb) * br_b).astype(BF16)
    x1 = x_ref[...] + _rms(_dot(m_ref[...], wout_ref[...])) * gpost_ref[...]
    x1_ref[...] = x1
    h2_ref[...] = (_rms(x1) * gpre_ref[...]).astype(BF16)


def _merge(a, h, u3, hist, hist_map, x, wup, wga, wgb, wpool, ps, wout, gpost, gpre,
           *, nseg, tl, pos0, zero_first, to_bf16=()):
    s_total, l_total, p = u3.shape
    n, d = x.shape
    tm = nseg * tl
    steps = l_total // tl
    assert nseg == 1 or steps == 1
    row = lambda width: pl.BlockSpec((tm, width), lambda s, i: (s * steps + i, 0))
    n_steps = (s_total // nseg) * steps
    slab = lambda w: pl.BlockSpec((w.shape[0] // n_steps, w.shape[1]), lambda s, i: (s * steps + i, 0))
    est = (_nbytes(wup.shape, BF16) + 2 * _nbytes(wga.shape, BF16) + _nbytes(wpool.shape, BF16)
           + _nbytes(wout.shape, BF16)
           + 2 * (_nbytes((tm, a.shape[1]), BF16) + 2 * _nbytes((tm, d), BF16) + 2 * _nbytes((tm, d), F32)
                  + _nbytes((tm + HIST_ROWS, p), F32))
           + _nbytes((tm + nseg * HIST_ROWS, p), F32) + _nbytes((tm, d), BF16) + 6 * _nbytes((tm, d), F32)
           + sum(3 * _nbytes(w.shape, F32) // n_steps for w in to_bf16))
    return pl.pallas_call(
        functools.partial(_merge_kernel, pos0=pos0, zero_first=zero_first, n_cast=len(to_bf16)),
        grid=(s_total // nseg, steps),
        in_specs=[row(a.shape[1]), row(d),
                  pl.BlockSpec((nseg, tl, p), lambda s, i: (s, i, 0)),
                  pl.BlockSpec((nseg, HIST_ROWS, p), hist_map),
                  row(d),
                  _resident(wup.shape), _resident(wga.shape), _resident(wgb.shape), _resident(wpool.shape),
                  _resident((1, d)), _resident(wout.shape), _resident((1, d)), _resident((1, d))]
                 + [slab(w) for w in to_bf16],
        out_specs=(row(d), row(d)) + tuple(slab(w) for w in to_bf16),
        out_shape=(jax.ShapeDtypeStruct((n, d), F32), jax.ShapeDtypeStruct((n, d), BF16))
                  + tuple(jax.ShapeDtypeStruct(w.shape, BF16) for w in to_bf16),
        scratch_shapes=[pltpu.VMEM((nseg, HIST_ROWS + tl, p), F32), pltpu.VMEM((tm, d), BF16)],
        compiler_params=pltpu.CompilerParams(
            dimension_semantics=("arbitrary", "arbitrary"), vmem_limit_bytes=_vmem_limit(est)),
        name="merge",
    )(a, h, u3, hist, x, wup, wga, wgb, wpool, ps, wout, gpost, gpre, *to_bf16)


def _ffn_kernel(h_ref, x1_ref, w1_ref, w2_ref, g_ref, o_ref):
    f = pl.program_id(1)

    @pl.when(f == 0)
    def _():
        o_ref[...] = jnp.zeros_like(o_ref)

    z = jnp.square(jnp.maximum(_dot(h_ref[...], w1_ref[...]), 0.0)).astype(BF16)
    o_ref[...] += _dot(z, w2_ref[...])

    @pl.when(f == pl.num_programs(1) - 1)
    def _():
        o_ref[...] = x1_ref[...] + _rms(o_ref[...]) * g_ref[...]


def _ffn(h2, x1, w1, w2, g, *, tm, tf):
    n, d = x1.shape
    dff = w1.shape[1]
    est = (2 * _nbytes((tm, d), BF16) + 4 * _nbytes((tm, d), F32)
           + 4 * _nbytes((d, tf), BF16) + 2 * _nbytes((tm, tf), F32) + _nbytes((tm, d), F32))
    return pl.pallas_call(
        _ffn_kernel,
        grid=(n // tm, dff // tf),
        in_specs=[pl.BlockSpec((tm, d), lambda i, f: (i, 0)),
                  pl.BlockSpec((tm, d), lambda i, f: (i, 0)),
                  pl.BlockSpec((d, tf), lambda i, f: (0, f)),
                  pl.BlockSpec((tf, d), lambda i, f: (f, 0)),
                  pl.BlockSpec((1, d), lambda i, f: (0, 0))],
        out_specs=pl.BlockSpec((tm, d), lambda i, f: (i, 0)),
        out_shape=jax.ShapeDtypeStruct((n, d), F32),
        compiler_params=pltpu.CompilerParams(
            dimension_semantics=("arbitrary", "arbitrary"), vmem_limit_bytes=_vmem_limit(est)),
        name="ffn",
    )(h2, x1, w1, w2, g)


def _tile(n, pref):
    t = min(n, pref)
    while n % t:
        t //= 2
    return t


INPROJ_TM = 256
ATTN_TQ = 512
MERGE_TM = 256
FFN_TM = 512
FFN_TF = 1024
CUMSUM_ROWS = 32
REPACK_ROWS = 256


def _cum_logf_rows(logf_bht):
    b, h, t = logf_bht.shape
    rows = logf_bht.reshape(b * h, t)
    pad = (-t) % LANES
    if pad:
        rows = jnp.pad(rows, ((0, 0), (0, pad)))
    return _cumsum_lanes(rows)


def _time_minor(x, nbatch):
    return jnp.transpose(x, (0, 2, 3, 1)).reshape(nbatch, -1, x.shape[1])


def _layer(xp, xs, cache_k, cache_v, cache_logf, state_pool, g_mix_pre, w_in, b_f, w_attn_up, w_pool,
           pool_scale, w_out, g_mix_post, g_ffn_pre, w_ff1, w_ff2, g_ffn_post):
    bp, tp, d = xp.shape
    bs, ts, _ = xs.shape
    past = cache_k.shape[1]
    a, nh, p = ATTN_W, N_HEADS, w_pool.shape[0] * w_pool.shape[1]
    off_f = 3 * a
    off_u = off_f + nh
    off_ga = off_u + p
    off_gb = off_ga + d

    w_qkvu, w_f, w_ga, w_gb = _repack_w_in(jnp.transpose(w_in.astype(F32)), off_f=off_f, off_u=off_u,
                                           off_ga=off_ga, off_gb=off_gb, tr=_tile(w_in.shape[0], REPACK_ROWS))
    b_f2 = jnp.pad(b_f.reshape(1, -1), ((0, 0), (0, LANES - nh)))
    w_up = w_attn_up.astype(BF16)
    w_pl = w_pool.astype(BF16)
    w_o = w_out.astype(BF16)
    row = lambda v: v.reshape(1, -1)
    g_pre, ps = row(g_mix_pre), row(pool_scale)
    g_post, g_fpre, g_fpost = row(g_mix_post), row(g_ffn_pre), row(g_ffn_post)

    def project(x3, time_minor):
        x2 = x3.reshape(-1, d)
        tm = _tile(x3.shape[1] if time_minor else x2.shape[0], INPROJ_TM)
        return x2, _inproj(x2, g_pre, w_qkvu, w_f, b_f2, tm=tm, seq=x3.shape[1], time_minor=time_minor)

    def finish(x2, h, a_out, u3, hist, hist_map, nseg, tl, pos0, zero_first, ffn_w):
        to_bf16 = () if ffn_w else (w_ff1.astype(F32), w_ff2.astype(F32))
        x1, h2, *cast = _merge(a_out, h, u3, hist, hist_map, x2, w_up, w_ga, w_gb, w_pl, ps, w_o, g_post, g_fpre,
                               nseg=nseg, tl=tl, pos0=pos0, zero_first=zero_first, to_bf16=to_bf16)
        w1, w2 = ffn_w or cast
        n = x2.shape[0]
        return _ffn(h2, x1, w1, w2, g_fpost, tm=_tile(n, FFN_TM), tf=_tile(w1.shape[1], FFN_TF)), (w1, w2)

    x2, (h, q, kb, kt, vt, _, u, logft) = project(xp, True)
    c_rows = _cum_logf_rows(logft)
    nblk = nh // HEADS_PER_BLOCK
    three = lambda z: z.reshape(bp, tp, a)
    a_out = _attn_prompt(three(q), three(kb), vt, c_rows.reshape(bp, nblk, HEADS_PER_BLOCK, tp),
                         tq=_tile(tp, ATTN_TQ))
    to_bthd = lambda zt: jnp.transpose(zt.reshape(bp, nh, HEAD_DIM, tp), (0, 3, 1, 2))
    u3 = u.reshape(bp, tp, p)
    tl = _tile(tp, MERGE_TM)
    blocks_per_tile = tl // HIST_ROWS
    hist_map = lambda s, i: (s, jnp.maximum(i * blocks_per_tile - 1, 0), 0)
    yp, ffn_w = finish(x2, h, a_out.reshape(bp * tp, a), u3, u3, hist_map, 1, tl, 0, True, None)
    yp = yp.reshape(bp, tp, d)
    prompt_out = (yp, to_bthd(kt), to_bthd(vt), jnp.transpose(logft, (0, 2, 1)),
                  jnp.concatenate([jnp.zeros((bp, POOL_HIST, p), F32), u3], axis=1)[:, -POOL_HIST:])

    x2, (h, q, kb, k, v, vb, u, logf) = project(xs, False)
    logf3 = logf.reshape(bs, ts, nh)
    f_all = jnp.concatenate([cache_logf.astype(F32), logf3], axis=1)
    c_all = _cum_logf_rows(jnp.transpose(f_all, (0, 2, 1))).reshape(bs, nh, -1)
    c_cache = c_all[:, :, :past]
    c_new = c_all[:, :, past:past + ts]
    three = lambda z: z.reshape(bs, ts, a)
    a_out = _attn_sample(three(q), _time_minor(cache_k.astype(F32), bs), _time_minor(cache_v.astype(F32), bs),
                         three(kb), three(vb), c_cache, c_new, jnp.transpose(c_new, (0, 2, 1)))
    u3 = u.reshape(bs, ts, p)
    hist = jnp.pad(state_pool.astype(F32), ((0, 0), (HIST_ROWS - POOL_HIST, 0), (0, 0)))
    ys, _ = finish(x2, h, a_out.reshape(bs * ts, a), u3, hist, lambda s, i: (s, 0, 0), bs, ts, past, False, ffn_w)
    ys = ys.reshape(bs, ts, d)
    sample_out = (ys, k.reshape(bs, ts, nh, HEAD_DIM), v.reshape(bs, ts, nh, HEAD_DIM), logf3,
                  jnp.concatenate([state_pool.astype(F32), u3], axis=1)[:, -POOL_HIST:])
    return prompt_out, sample_out


def kernel(x_prompt, x_sample, cache_k, cache_v, cache_logf, state_pool, g_mix_pre, w_in, b_f, w_attn_up,
           w_pool, pool_scale, w_out, g_mix_post, g_ffn_pre, w_ff1, w_ff2, g_ffn_post):
    depth = w_in.shape[0]
    xp, xs = x_prompt, x_sample
    per_layer = []
    for l in range(depth):
        po, so = _layer(xp, xs, cache_k[l], cache_v[l], cache_logf[l], state_pool[l], g_mix_pre[l], w_in[l],
                        b_f[l], w_attn_up[l], w_pool[l], pool_scale[l], w_out[l], g_mix_post[l],
                        g_ffn_pre[l], w_ff1[l], w_ff2[l], g_ffn_post[l])
        xp, xs = po[0], so[0]
        per_layer.append(po[1:] + so[1:])
    stacked = [jnp.stack(leaves, 0) for leaves in zip(*per_layer)]
    return (xp, xs, *stacked)
```

```python
import functools
import math

import jax
import jax.numpy as jnp
from jax import lax
from jax.experimental import pallas as pl
from jax.experimental.pallas import tpu as pltpu

N_HEADS = 16
HEAD_DIM = 64
ATTN_W = N_HEADS * HEAD_DIM
POOL_WINDOWS = (2, 4, 8, 16)
POOL_GROUPS = len(POOL_WINDOWS)
POOL_HIST = max(POOL_WINDOWS) - 1
EPS = 1e-6
SCALE = HEAD_DIM ** -0.5

LANES = 128
SUBLANES = 8
V7X_VMEM_BYTES = 64 * 1024 * 1024
VMEM_LIMIT_CAP = 60000 * 1024

HEADS_PER_BLOCK = LANES // HEAD_DIM
HIST_ROWS = 16

F32 = jnp.float32
BF16 = jnp.bfloat16


def _vmem_limit(nbytes):
    return int(min(VMEM_LIMIT_CAP, max(32 * 1024 * 1024, nbytes * 5 // 4)))


def _nbytes(shape, dtype):
    return math.prod(shape) * jnp.dtype(dtype).itemsize


def _rms(x):
    return x * lax.rsqrt(jnp.mean(x * x, axis=-1, keepdims=True) + EPS)


def _dot(a, b):
    return jnp.dot(a, b, preferred_element_type=F32)


def _dot_nt(a, b):
    return lax.dot_general(a, b, (((1,), (1,)), ((), ())), preferred_element_type=F32)


def _resident(shape):
    zeros = (0,) * len(shape)
    return pl.BlockSpec(shape, lambda *_: zeros, pipeline_mode=pl.Buffered(1))


def _repack_kernel(wt_ref, qkvu_ref, f_ref, ga_ref, gb_ref, *, off_f, off_u, off_ga, off_gb):
    nh = off_u - off_f
    p = off_ga - off_u
    d = off_gb - off_ga
    piece = lambda lo, hi: wt_ref[lo:hi, :].T.astype(BF16)
    qkvu_ref[:, 0:off_f] = piece(0, off_f)
    qkvu_ref[:, off_f:off_f + p] = piece(off_u, off_ga)
    fwin = wt_ref[off_f:off_f + LANES, :].T
    lane = lax.broadcasted_iota(jnp.int32, fwin.shape, 1)
    f_ref[...] = jnp.where(lane < nh, fwin, 0.0).astype(BF16)
    ga_ref[...] = piece(off_ga, off_gb)
    gb_ref[...] = piece(off_gb, off_gb + d)


def _repack_w_in(wt, *, off_f, off_u, off_ga, off_gb, tr):
    width, rows = wt.shape
    p = off_ga - off_u
    d = off_gb - off_ga
    blk = lambda cols: pl.BlockSpec((tr, cols), lambda i: (i, 0))
    est = 4 * _nbytes((width, tr), F32) + 2 * _nbytes((tr, off_f + p + LANES + 2 * d), BF16)
    return pl.pallas_call(
        functools.partial(_repack_kernel, off_f=off_f, off_u=off_u, off_ga=off_ga, off_gb=off_gb),
        grid=(rows // tr,),
        in_specs=[pl.BlockSpec((width, tr), lambda i: (0, i))],
        out_specs=(blk(off_f + p), blk(LANES), blk(d), blk(d)),
        out_shape=(jax.ShapeDtypeStruct((rows, off_f + p), BF16), jax.ShapeDtypeStruct((rows, LANES), BF16),
                   jax.ShapeDtypeStruct((rows, d), BF16), jax.ShapeDtypeStruct((rows, d), BF16)),
        compiler_params=pltpu.CompilerParams(
            dimension_semantics=("arbitrary",), vmem_limit_bytes=_vmem_limit(est)),
        name="repack_w_in",
    )(wt)


def _convert_slabs(refs, n_cast):
    for src, dst in zip(refs[:n_cast], refs[len(refs) - n_cast:]):
        dst[...] = src[...].astype(dst.dtype)
    return refs[n_cast:len(refs) - n_cast]


def _slab_specs(to_bf16, n_steps, index_map):
    specs = [pl.BlockSpec((w.shape[0] // n_steps, w.shape[1]), index_map) for w in to_bf16]
    return specs, tuple(jax.ShapeDtypeStruct(w.shape, BF16) for w in to_bf16)


def _inproj_kernel(x_ref, g_ref, w_ref, wf_ref, bf_ref, *rest, time_minor, n_cast):
    h_ref, q_ref, kb_ref, k_ref, v_ref, vb_ref, u_ref, f_ref = _convert_slabs(rest, n_cast)
    h = (_rms(x_ref[...]) * g_ref[...]).astype(BF16)
    h_ref[...] = h
    a = q_ref.shape[1]
    q_ref[...] = (_dot(h, w_ref[:, 0:a]) * SCALE).astype(BF16)
    k = _dot(h, w_ref[:, a:2 * a])
    kb_ref[...] = k.astype(BF16)
    v = _dot(h, w_ref[:, 2 * a:3 * a])
    vb_ref[...] = v.astype(BF16)
    u_ref[...] = _dot(h, w_ref[:, 3 * a:])
    f = _dot(h, wf_ref[...]) + bf_ref[...]
    if time_minor:
        k_ref[0] = k.T
        v_ref[0] = v.T
        f_ref[0] = jax.nn.log_sigmoid(f.T[0:f_ref.shape[1], :])
    else:
        k_ref[...] = k
        v_ref[...] = v
        f_ref[...] = jax.nn.log_sigmoid(f[:, 0:f_ref.shape[1]])


def _inproj(x, g, w, wf, bf, *, tm, seq, time_minor, to_bf16=()):
    n, d = x.shape
    cast_specs, cast_shapes = _slab_specs(to_bf16, n // tm, lambda i: (i, 0))
    a = ATTN_W
    p = w.shape[1] - 3 * a
    nh = N_HEADS
    row = lambda width: pl.BlockSpec((tm, width), lambda i: (i, 0))
    if time_minor:
        steps = seq // tm
        feat = lambda rows: (jax.ShapeDtypeStruct((n // seq, rows, seq), F32),
                             pl.BlockSpec((1, rows, tm), lambda i: (i // steps, 0, i % steps)))
    else:
        feat = lambda rows: (jax.ShapeDtypeStruct((n, rows), F32), row(rows))
    (k_shape, k_spec), (f_shape, f_spec) = feat(a), feat(nh)
    out_shape = (
        jax.ShapeDtypeStruct((n, d), BF16),
        jax.ShapeDtypeStruct((n, a), BF16),
        jax.ShapeDtypeStruct((n, a), BF16),
        k_shape,
        k_shape,
        jax.ShapeDtypeStruct((n, a), BF16),
        jax.ShapeDtypeStruct((n, p), F32),
        f_shape,
    )
    est = (2 * _nbytes((tm, d), F32) + _nbytes(w.shape, BF16) + _nbytes(wf.shape, BF16)
           + 2 * (_nbytes((tm, d), BF16) + 3 * _nbytes((tm, a), BF16) + 2 * _nbytes((tm, a), F32)
                  + _nbytes((tm, p), F32) + _nbytes((tm, LANES), F32))
           + 4 * _nbytes((tm, a), F32) + sum(3 * _nbytes(c.shape, F32) * tm // n for c in to_bf16))
    return pl.pallas_call(
        functools.partial(_inproj_kernel, time_minor=time_minor, n_cast=len(to_bf16)),
        grid=(n // tm,),
        in_specs=[row(d), _resident((1, d)), _resident(w.shape), _resident(wf.shape), _resident((1, LANES))]
                 + cast_specs,
        out_specs=(row(d), row(a), row(a), k_spec, k_spec, row(a), row(p), f_spec) + tuple(cast_specs),
        out_shape=out_shape + cast_shapes,
        compiler_params=pltpu.CompilerParams(
            dimension_semantics=("arbitrary",), vmem_limit_bytes=_vmem_limit(est)),
        name="inproj",
    )(x, g, w, wf, bf, *to_bf16)


def _cumsum_kernel(x_ref, o_ref):
    x = x_ref[...]
    length = x.shape[1]
    lane = lax.broadcasted_iota(jnp.int32, x.shape, 1)
    shift = 1
    while shift < length:
        x = x + jnp.where(lane >= shift, pltpu.roll(x, shift, axis=1), 0.0)
        shift *= 2
    o_ref[...] = x


def _cumsum_lanes(x):
    rows, length = x.shape
    tr = _tile(rows, CUMSUM_ROWS)
    spec = pl.BlockSpec((tr, length), lambda i: (i, 0))
    return pl.pallas_call(
        _cumsum_kernel,
        grid=(rows // tr,),
        in_specs=[spec],
        out_specs=spec,
        out_shape=jax.ShapeDtypeStruct(x.shape, F32),
        compiler_params=pltpu.CompilerParams(dimension_semantics=("arbitrary",)),
        name="cumsum",
    )(x)


VALUE_ROWS = HEAD_DIM + 16


def _attn_kernel(q_ref, k_ref, vt_ref, c_ref, o_ref,
                 vta_ref, ckrep_ref, s_ref, p_ref, alpha_ref, m_ref, acc_ref, *, tq):
    t = k_ref.shape[1]
    nq = t // tq
    for a in range(HEADS_PER_BLOCK):
        vta_ref[a, 0:HEAD_DIM, :] = vt_ref[0, a * HEAD_DIM:(a + 1) * HEAD_DIM, :].astype(BF16)
        vta_ref[a, HEAD_DIM:VALUE_ROWS, :] = jnp.ones((VALUE_ROWS - HEAD_DIM, t), BF16)
        ckrep_ref[a] = jnp.broadcast_to(c_ref[0, 0, a:a + 1, :], (LANES, t)).T

    pairs = [(i, j) for i in range(nq) for j in range(i + 1)]
    rows = lambda j: slice(j * tq, (j + 1) * tq)

    def scores(w):
        i, j = pairs[w]
        q2 = q_ref[0, rows(i), :]
        lane = lax.broadcasted_iota(jnp.int32, q2.shape, 1)
        kt = k_ref[0, rows(j), :]
        for a in range(HEADS_PER_BLOCK):
            qa = jnp.where((lane // HEAD_DIM) == a, q2, jnp.zeros_like(q2))
            ckr = ckrep_ref[a, rows(j), :]
            s_ref[w % 2, a] = _dot_nt(kt, qa) - jnp.concatenate([ckr] * (tq // LANES), axis=1)

    def probs(w):
        i, j = pairs[w]
        for a in range(HEADS_PER_BLOCK):
            s = s_ref[w % 2, a]
            if j == i:
                r = lax.broadcasted_iota(jnp.int32, s.shape, 0)
                c = lax.broadcasted_iota(jnp.int32, s.shape, 1)
                s = jnp.where(r <= c, s, -jnp.inf)
            cqa = c_ref[0, 0, a:a + 1, rows(i)]
            smax = jnp.max(s, axis=0, keepdims=True) + cqa
            if j == 0:
                m_new = smax
            else:
                m_prev = m_ref[a]
                m_new = jnp.maximum(m_prev, smax)
                alpha_ref[w % 2, a] = jnp.exp(m_prev - m_new)
            p_ref[w % 2, a] = jnp.exp(s - (m_new - cqa)).astype(BF16)
            m_ref[a] = m_new

    def values(w):
        i, j = pairs[w]
        for a in range(HEADS_PER_BLOCK):
            pv = _dot(vta_ref[a, :, rows(j)], p_ref[w % 2, a])
            acc_ref[a] = pv if j == 0 else alpha_ref[w % 2, a] * acc_ref[a] + pv
        if j == i:
            ot = jnp.concatenate(
                [acc_ref[a, 0:HEAD_DIM, :] / acc_ref[a, HEAD_DIM:HEAD_DIM + 1, :]
                 for a in range(HEADS_PER_BLOCK)], axis=0)
            o_ref[0, rows(i), :] = ot.T.astype(o_ref.dtype)

    n = len(pairs)
    scores(0)
    for w in range(n):
        if w + 1 < n:
            scores(w + 1)
        probs(w)
        if w >= 1:
            values(w - 1)
    values(n - 1)


def _attn_prompt(q, kb, vt, c_rows, *, tq):
    b, t, a = q.shape
    nblk = a // LANES
    spec = pl.BlockSpec((1, t, LANES), lambda bi, hi: (bi, 0, hi))
    return pl.pallas_call(
        functools.partial(_attn_kernel, tq=tq),
        grid=(b, nblk),
        in_specs=[spec, spec, pl.BlockSpec((1, LANES, t), lambda bi, hi: (bi, hi, 0)),
                  pl.BlockSpec((1, 1, HEADS_PER_BLOCK, t), lambda bi, hi: (bi, hi, 0, 0))],
        out_specs=spec,
        out_shape=jax.ShapeDtypeStruct((b, t, a), BF16),
        scratch_shapes=[pltpu.VMEM((HEADS_PER_BLOCK, VALUE_ROWS, t), BF16),
                        pltpu.VMEM((HEADS_PER_BLOCK, t, LANES), F32),
                        pltpu.VMEM((2, HEADS_PER_BLOCK, tq, tq), F32),
                        pltpu.VMEM((2, HEADS_PER_BLOCK, tq, tq), BF16),
                        pltpu.VMEM((2, HEADS_PER_BLOCK, 1, tq), F32),
                        pltpu.VMEM((HEADS_PER_BLOCK, 1, tq), F32),
                        pltpu.VMEM((HEADS_PER_BLOCK, VALUE_ROWS, tq), F32)],
        compiler_params=pltpu.CompilerParams(dimension_semantics=("arbitrary", "arbitrary")),
        name="attn_prompt",
    )(q, kb, vt, c_rows)


def _attn_sample_kernel(q_ref, kct_ref, vct_ref, kn_ref, vn_ref, cc_ref, cnr_ref, cnc_ref, o_ref,
                        kb_ref, vb_ref, s_ref, p_ref, pn_ref):
    n, a = q_ref.shape[1], q_ref.shape[2]
    nh = a // HEAD_DIM
    q = q_ref[0]
    qt = jnp.concatenate([q] * nh, axis=0)
    row_h = lax.broadcasted_iota(jnp.int32, qt.shape, 0) // n
    col_h = lax.broadcasted_iota(jnp.int32, qt.shape, 1) // HEAD_DIM
    qbd = jnp.where(row_h == col_h, qt, jnp.zeros_like(qt))
    kb_ref[...] = kct_ref[0].astype(BF16)
    vb_ref[...] = vct_ref[0].astype(BF16)
    s_ref[...] = _dot(qbd, kb_ref[...])
    s_new = _dot_nt(qbd, kn_ref[0])
    cnc = cnc_ref[0]
    r = lax.broadcasted_iota(jnp.int32, (n, n), 0)
    c = lax.broadcasted_iota(jnp.int32, (n, n), 1)
    inv_l = []
    for h in range(nh):
        rows = slice(h * n, (h + 1) * n)
        cq = cnc[:, h:h + 1]
        sc = s_ref[rows, :] + (cq - cc_ref[0, h:h + 1, :])
        sn = s_new[rows, :] + (cq - cnr_ref[0, h:h + 1, :])
        sn = jnp.where(c <= r, sn, -jnp.inf)
        m = jnp.maximum(jnp.max(sc, axis=1, keepdims=True), jnp.max(sn, axis=1, keepdims=True))
        pc = jnp.exp(sc - m)
        pn = jnp.exp(sn - m)
        inv_l.append(1.0 / (jnp.sum(pc, axis=1, keepdims=True) + jnp.sum(pn, axis=1, keepdims=True)))
        p_ref[rows, :] = pc.astype(BF16)
        pn_ref[rows, :] = pn.astype(BF16)
    o = _dot_nt(p_ref[...], vb_ref[...]) + _dot(pn_ref[...], vn_ref[0])
    for h in range(nh):
        rows = slice(h * n, (h + 1) * n)
        cols = slice(h * HEAD_DIM, (h + 1) * HEAD_DIM)
        o_ref[0, :, cols] = (o[rows, cols] * inv_l[h]).astype(o_ref.dtype)


def _attn_sample(q, kct, vct, kn, vn, cc, cnr, cnc):
    b, n, a = q.shape
    past = kct.shape[2]
    nh = a // HEAD_DIM
    per_b = lambda *tail: pl.BlockSpec((1,) + tail, lambda bi: (bi,) + (0,) * len(tail))
    est = (4 * _nbytes((a, past), F32) + 2 * _nbytes((a, past), BF16)
           + _nbytes((nh * n, past), F32) + _nbytes((nh * n, past), BF16) + 4 * _nbytes((nh * n, a), F32))
    return pl.pallas_call(
        _attn_sample_kernel,
        grid=(b,),
        in_specs=[per_b(n, a), per_b(a, past), per_b(a, past), per_b(n, a), per_b(n, a),
                  per_b(nh, past), per_b(nh, n), per_b(n, nh)],
        out_specs=per_b(n, a),
        out_shape=jax.ShapeDtypeStruct((b, n, a), BF16),
        scratch_shapes=[pltpu.VMEM((a, past), BF16), pltpu.VMEM((a, past), BF16),
                        pltpu.VMEM((nh * n, past), F32), pltpu.VMEM((nh * n, past), BF16),
                        pltpu.VMEM((nh * n, n), BF16)],
        compiler_params=pltpu.CompilerParams(
            dimension_semantics=("arbitrary",), vmem_limit_bytes=_vmem_limit(est)),
        name="attn_sample",
    )(q, kct, vct, kn, vn, cc, cnr, cnc)


def _merge_kernel(a_ref, h_ref, u_ref, hist_ref, x_ref, wup_ref, wga_ref, wgb_ref, wpool_ref, ps_ref,
                  wout_ref, gpost_ref, gpre_ref, *rest, pos0, zero_first, n_cast):
    ext_ref, m_ref = rest[len(rest) - 2:]
    x1_ref, h2_ref = _convert_slabs(rest[:len(rest) - 2], n_cast)
    nseg, tl, _ = u_ref.shape
    i = pl.program_id(1)
    hist = hist_ref[...]
    if zero_first:
        hist = jnp.where(i == 0, 0.0, hist)
    ext_ref[:, 0:HIST_ROWS, :] = hist
    ext_ref[:, HIST_ROWS:HIST_ROWS + tl, :] = u_ref[...]
    gw = u_ref.shape[2] // POOL_GROUPS
    ogw = wpool_ref.shape[2]
    pos = pos0 + i * tl + lax.broadcasted_iota(jnp.int32, (nseg, tl, gw), 1)
    a = a_ref[...]
    h = h_ref[...]
    for g, w in enumerate(POOL_WINDOWS):
        os_ = slice(g * ogw, (g + 1) * ogw)
        br_a = _dot(a, wup_ref[:, os_])
        ga = _dot(h, wga_ref[:, os_])
        gb = _dot(h, wgb_ref[:, os_])
        cs = slice(g * gw, (g + 1) * gw)
        cur = ext_ref[:, HIST_ROWS:HIST_ROWS + tl, cs]
        tot = cur
        for s in range(1, w):
            tot = tot + ext_ref[:, HIST_ROWS - s:HIST_ROWS - s + tl, cs]
        cnt = jnp.minimum(pos + 1, w).astype(F32)
        pooled = (tot / cnt - cur).reshape(nseg * tl, gw).astype(BF16)
        br_b = _dot(pooled, wpool_ref[g]) * ps_ref[:, os_]
        m_ref[:, os_] = (jax.nn.sigmoid(ga) * br_a + jax.nn.sigmoid(gb) * br_b).astype(BF16)
    x1 = x_ref[...] + _rms(_dot(m_ref[...], wout_ref[...])) * gpost_ref[...]
    x1_ref[...] = x1
    h2_ref[...] = (_rms(x1) * gpre_ref[...]).astype(BF16)


def _merge(a, h, u3, hist, hist_map, x, wup, wga, wgb, wpool, ps, wout, gpost, gpre,
           *, nseg, tl, pos0, zero_first, to_bf16=()):
    s_total, l_total, p = u3.shape
    n, d = x.shape
    tm = nseg * tl
    steps = l_total // tl
    assert nseg == 1 or steps == 1
    row = lambda width: pl.BlockSpec((tm, width), lambda s, i: (s * steps + i, 0))
    n_steps = (s_total // nseg) * steps
    cast_specs, cast_shapes = _slab_specs(to_bf16, n_steps, lambda s, i: (s * steps + i, 0))
    est = (_nbytes(wup.shape, BF16) + 2 * _nbytes(wga.shape, BF16) + _nbytes(wpool.shape, BF16)
           + _nbytes(wout.shape, BF16)
           + 2 * (_nbytes((tm, a.shape[1]), BF16) + 2 * _nbytes((tm, d), BF16) + 2 * _nbytes((tm, d), F32)
                  + _nbytes((tm + HIST_ROWS, p), F32))
           + _nbytes((tm + nseg * HIST_ROWS, p), F32) + _nbytes((tm, d), BF16) + 6 * _nbytes((tm, d), F32)
           + sum(3 * _nbytes(w.shape, F32) // n_steps for w in to_bf16))
    return pl.pallas_call(
        functools.partial(_merge_kernel, pos0=pos0, zero_first=zero_first, n_cast=len(to_bf16)),
        grid=(s_total // nseg, steps),
        in_specs=[row(a.shape[1]), row(d),
                  pl.BlockSpec((nseg, tl, p), lambda s, i: (s, i, 0)),
                  pl.BlockSpec((nseg, HIST_ROWS, p), hist_map),
                  row(d),
                  _resident(wup.shape), _resident(wga.shape), _resident(wgb.shape), _resident(wpool.shape),
                  _resident((1, d)), _resident(wout.shape), _resident((1, d)), _resident((1, d))]
                 + cast_specs,
        out_specs=(row(d), row(d)) + tuple(cast_specs),
        out_shape=(jax.ShapeDtypeStruct((n, d), F32), jax.ShapeDtypeStruct((n, d), BF16)) + cast_shapes,
        scratch_shapes=[pltpu.VMEM((nseg, HIST_ROWS + tl, p), F32), pltpu.VMEM((tm, d), BF16)],
        compiler_params=pltpu.CompilerParams(
            dimension_semantics=("arbitrary", "arbitrary"), vmem_limit_bytes=_vmem_limit(est)),
        name="merge",
    )(a, h, u3, hist, x, wup, wga, wgb, wpool, ps, wout, gpost, gpre, *to_bf16)


def _ffn_kernel(h_ref, x1_ref, w1_ref, w2_ref, g_ref, o_ref):
    f = pl.program_id(1)

    @pl.when(f == 0)
    def _():
        o_ref[...] = jnp.zeros_like(o_ref)

    z = jnp.square(jnp.maximum(_dot(h_ref[...], w1_ref[...]), 0.0)).astype(BF16)
    o_ref[...] += _dot(z, w2_ref[...])

    @pl.when(f == pl.num_programs(1) - 1)
    def _():
        o_ref[...] = x1_ref[...] + _rms(o_ref[...]) * g_ref[...]


def _ffn(h2, x1, w1, w2, g, *, tm, tf):
    n, d = x1.shape
    dff = w1.shape[1]
    est = (2 * _nbytes((tm, d), BF16) + 4 * _nbytes((tm, d), F32)
           + 4 * _nbytes((d, tf), BF16) + 2 * _nbytes((tm, tf), F32) + _nbytes((tm, d), F32))
    return pl.pallas_call(
        _ffn_kernel,
        grid=(n // tm, dff // tf),
        in_specs=[pl.BlockSpec((tm, d), lambda i, f: (i, 0)),
                  pl.BlockSpec((tm, d), lambda i, f: (i, 0)),
                  pl.BlockSpec((d, tf), lambda i, f: (0, f)),
                  pl.BlockSpec((tf, d), lambda i, f: (f, 0)),
                  pl.BlockSpec((1, d), lambda i, f: (0, 0))],
        out_specs=pl.BlockSpec((tm, d), lambda i, f: (i, 0)),
        out_shape=jax.ShapeDtypeStruct((n, d), F32),
        compiler_params=pltpu.CompilerParams(
            dimension_semantics=("arbitrary", "arbitrary"), vmem_limit_bytes=_vmem_limit(est)),
        name="ffn",
    )(h2, x1, w1, w2, g)


def _tile(n, pref):
    t = min(n, pref)
    while n % t:
        t //= 2
    return t


INPROJ_TM = 256
ATTN_TQ = 512
MERGE_TM = 256
FFN_TM = 512
FFN_TF = 1024
FFN_TF_STREAMING = 2048
CUMSUM_ROWS = 32
REPACK_ROWS = 256


def _cum_logf_rows(logf_bht):
    b, h, t = logf_bht.shape
    rows = logf_bht.reshape(b * h, t)
    pad = (-t) % LANES
    if pad:
        rows = jnp.pad(rows, ((0, 0), (0, pad)))
    return _cumsum_lanes(rows)


def _time_minor(x, nbatch):
    return jnp.transpose(x, (0, 2, 3, 1)).reshape(nbatch, -1, x.shape[1])


def _layer(xp, xs, cache_k, cache_v, cache_logf, state_pool, g_mix_pre, w_in, b_f, w_attn_up, w_pool,
           pool_scale, w_out, g_mix_post, g_ffn_pre, w_ff1, w_ff2, g_ffn_post):
    bp, tp, d = xp.shape
    bs, ts, _ = xs.shape
    past = cache_k.shape[1]
    a, nh, p = ATTN_W, N_HEADS, w_pool.shape[0] * w_pool.shape[1]
    off_f = 3 * a
    off_u = off_f + nh
    off_ga = off_u + p
    off_gb = off_ga + d

    w_qkvu, w_f, w_ga, w_gb = _repack_w_in(jnp.transpose(w_in.astype(F32)), off_f=off_f, off_u=off_u,
                                           off_ga=off_ga, off_gb=off_gb, tr=_tile(w_in.shape[0], REPACK_ROWS))
    b_f2 = jnp.pad(b_f.reshape(1, -1), ((0, 0), (0, LANES - nh)))
    row = lambda v: v.reshape(1, -1)
    g_pre, ps = row(g_mix_pre), row(pool_scale)
    g_post, g_fpre, g_fpost = row(g_mix_post), row(g_ffn_pre), row(g_ffn_post)

    def project(x3, time_minor, to_bf16=()):
        x2 = x3.reshape(-1, d)
        tm = _tile(x3.shape[1] if time_minor else x2.shape[0], INPROJ_TM)
        return x2, _inproj(x2, g_pre, w_qkvu, w_f, b_f2, tm=tm, seq=x3.shape[1], time_minor=time_minor,
                           to_bf16=to_bf16)

    def finish(x2, h, a_out, u3, hist, hist_map, nseg, tl, pos0, zero_first, ffn_w):
        to_bf16 = () if ffn_w else (w_ff1.astype(F32), w_ff2.astype(F32))
        x1, h2, *cast = _merge(a_out, h, u3, hist, hist_map, x2, w_up, w_ga, w_gb, w_pl, ps, w_o, g_post, g_fpre,
                               nseg=nseg, tl=tl, pos0=pos0, zero_first=zero_first, to_bf16=to_bf16)
        w1, w2 = ffn_w or cast
        n = x2.shape[0]
        tf = FFN_TF if n > FFN_TM else FFN_TF_STREAMING
        return _ffn(h2, x1, w1, w2, g_fpost, tm=_tile(n, FFN_TM), tf=_tile(w1.shape[1], tf)), (w1, w2)

    merge_w = (w_attn_up.astype(F32), w_pool.astype(F32).reshape(p, -1), w_out.astype(F32))
    x2, (h, q, kb, kt, vt, _, u, logft, w_up, w_pl, w_o) = project(xp, True, merge_w)
    w_pl = w_pl.reshape(w_pool.shape)
    c_rows = _cum_logf_rows(logft)
    nblk = nh // HEADS_PER_BLOCK
    three = lambda z: z.reshape(bp, tp, a)
    a_out = _attn_prompt(three(q), three(kb), vt, c_rows.reshape(bp, nblk, HEADS_PER_BLOCK, tp),
                         tq=_tile(tp, ATTN_TQ))
    to_bthd = lambda zt: jnp.transpose(zt.reshape(bp, nh, HEAD_DIM, tp), (0, 3, 1, 2))
    u3 = u.reshape(bp, tp, p)
    tl = _tile(tp, MERGE_TM)
    blocks_per_tile = tl // HIST_ROWS
    hist_map = lambda s, i: (s, jnp.maximum(i * blocks_per_tile - 1, 0), 0)
    yp, ffn_w = finish(x2, h, a_out.reshape(bp * tp, a), u3, u3, hist_map, 1, tl, 0, True, None)
    yp = yp.reshape(bp, tp, d)
    prompt_out = (yp, to_bthd(kt), to_bthd(vt), jnp.transpose(logft, (0, 2, 1)),
                  jnp.concatenate([jnp.zeros((bp, POOL_HIST, p), F32), u3], axis=1)[:, -POOL_HIST:])

    x2, (h, q, kb, k, v, vb, u, logf) = project(xs, False)
    logf3 = logf.reshape(bs, ts, nh)
    f_all = jnp.concatenate([cache_logf.astype(F32), logf3], axis=1)
    c_all = _cum_logf_rows(jnp.transpose(f_all, (0, 2, 1))).reshape(bs, nh, -1)
    c_cache = c_all[:, :, :past]
    c_new = c_all[:, :, past:past + ts]
    three = lambda z: z.reshape(bs, ts, a)
    a_out = _attn_sample(three(q), _time_minor(cache_k.astype(F32), bs), _time_minor(cache_v.astype(F32), bs),
                         three(kb), three(vb), c_cache, c_new, jnp.transpose(c_new, (0, 2, 1)))
    u3 = u.reshape(bs, ts, p)
    hist = jnp.pad(state_pool.astype(F32), ((0, 0), (HIST_ROWS - POOL_HIST, 0), (0, 0)))
    ys, _ = finish(x2, h, a_out.reshape(bs * ts, a), u3, hist, lambda s, i: (s, 0, 0), bs, ts, past, False, ffn_w)
    ys = ys.reshape(bs, ts, d)
    sample_out = (ys, k.reshape(bs, ts, nh, HEAD_DIM), v.reshape(bs, ts, nh, HEAD_DIM), logf3,
                  jnp.concatenate([state_pool.astype(F32), u3], axis=1)[:, -POOL_HIST:])
    return prompt_out, sample_out


def kernel(x_prompt, x_sample, cache_k, cache_v, cache_logf, state_pool, g_mix_pre, w_in, b_f, w_attn_up,
           w_pool, pool_scale, w_out, g_mix_post, g_ffn_pre, w_ff1, w_ff2, g_ffn_post):
    depth = w_in.shape[0]
    xp, xs = x_prompt, x_sample
    per_layer = []
    for l in range(depth):
        po, so = _layer(xp, xs, cache_k[l], cache_v[l], cache_logf[l], state_pool[l], g_mix_pre[l], w_in[l],
                        b_f[l], w_attn_up[l], w_pool[l], pool_scale[l], w_out[l], g_mix_post[l],
                        g_ffn_pre[l], w_ff1[l], w_ff2[l], g_ffn_post[l])
        xp, xs = po[0], so[0]
        per_layer.append(po[1:] + so[1:])
    stacked = [jnp.stack(leaves, 0) for leaves in zip(*per_layer)]
    return (xp, xs, *stacked)
```

```python
import functools
import math

import jax
import jax.numpy as jnp
from jax import lax
from jax.experimental import pallas as pl
from jax.experimental.pallas import tpu as pltpu

N_HEADS = 16
HEAD_DIM = 64
ATTN_W = N_HEADS * HEAD_DIM
POOL_WINDOWS = (2, 4, 8, 16)
POOL_GROUPS = len(POOL_WINDOWS)
POOL_HIST = max(POOL_WINDOWS) - 1
EPS = 1e-6
SCALE = HEAD_DIM ** -0.5

LANES = 128
SUBLANES = 8
V7X_VMEM_BYTES = 64 * 1024 * 1024
VMEM_LIMIT_CAP = 60000 * 1024

HEADS_PER_BLOCK = LANES // HEAD_DIM
HIST_ROWS = 16

F32 = jnp.float32
BF16 = jnp.bfloat16


def _vmem_limit(nbytes):
    return int(min(VMEM_LIMIT_CAP, max(32 * 1024 * 1024, nbytes * 5 // 4)))


def _nbytes(shape, dtype):
    return math.prod(shape) * jnp.dtype(dtype).itemsize


def _rms(x):
    return x * lax.rsqrt(jnp.mean(x * x, axis=-1, keepdims=True) + EPS)


def _dot(a, b):
    return jnp.dot(a, b, preferred_element_type=F32)


def _dot_nt(a, b):
    return lax.dot_general(a, b, (((1,), (1,)), ((), ())), preferred_element_type=F32)


def _resident(shape):
    zeros = (0,) * len(shape)
    return pl.BlockSpec(shape, lambda *_: zeros, pipeline_mode=pl.Buffered(1))


def _repack_kernel(wt_ref, qkvu_ref, f_ref, ga_ref, gb_ref, *, off_f, off_u, off_ga, off_gb):
    nh = off_u - off_f
    p = off_ga - off_u
    d = off_gb - off_ga
    piece = lambda lo, hi: wt_ref[lo:hi, :].T.astype(BF16)
    qkvu_ref[:, 0:off_f] = piece(0, off_f)
    qkvu_ref[:, off_f:off_f + p] = piece(off_u, off_ga)
    fwin = wt_ref[off_f:off_f + LANES, :].T
    lane = lax.broadcasted_iota(jnp.int32, fwin.shape, 1)
    f_ref[...] = jnp.where(lane < nh, fwin, 0.0).astype(BF16)
    ga_ref[...] = piece(off_ga, off_gb)
    gb_ref[...] = piece(off_gb, off_gb + d)


def _repack_w_in(wt, *, off_f, off_u, off_ga, off_gb, tr):
    width, rows = wt.shape
    p = off_ga - off_u
    d = off_gb - off_ga
    blk = lambda cols: pl.BlockSpec((tr, cols), lambda i: (i, 0))
    est = 4 * _nbytes((width, tr), F32) + 2 * _nbytes((tr, off_f + p + LANES + 2 * d), BF16)
    return pl.pallas_call(
        functools.partial(_repack_kernel, off_f=off_f, off_u=off_u, off_ga=off_ga, off_gb=off_gb),
        grid=(rows // tr,),
        in_specs=[pl.BlockSpec((width, tr), lambda i: (0, i))],
        out_specs=(blk(off_f + p), blk(LANES), blk(d), blk(d)),
        out_shape=(jax.ShapeDtypeStruct((rows, off_f + p), BF16), jax.ShapeDtypeStruct((rows, LANES), BF16),
                   jax.ShapeDtypeStruct((rows, d), BF16), jax.ShapeDtypeStruct((rows, d), BF16)),
        compiler_params=pltpu.CompilerParams(
            dimension_semantics=("arbitrary",), vmem_limit_bytes=_vmem_limit(est)),
        name="repack_w_in",
    )(wt)


def _convert_slabs(refs, n_cast):
    for src, dst in zip(refs[:n_cast], refs[len(refs) - n_cast:]):
        dst[...] = src[...].astype(dst.dtype)
    return refs[n_cast:len(refs) - n_cast]


def _slab_specs(to_bf16, n_steps, index_map):
    specs = [pl.BlockSpec((w.shape[0] // n_steps, w.shape[1]), index_map) for w in to_bf16]
    return specs, tuple(jax.ShapeDtypeStruct(w.shape, BF16) for w in to_bf16)


def _inproj_kernel(x_ref, g_ref, w_ref, wf_ref, bf_ref, *rest, time_minor, n_cast):
    h_ref, q_ref, kb_ref, k_ref, v_ref, vb_ref, u_ref, f_ref = _convert_slabs(rest, n_cast)
    h = (_rms(x_ref[...]) * g_ref[...]).astype(BF16)
    h_ref[...] = h
    a = q_ref.shape[1]
    q_ref[...] = (_dot(h, w_ref[:, 0:a]) * SCALE).astype(BF16)
    k = _dot(h, w_ref[:, a:2 * a])
    kb_ref[...] = k.astype(BF16)
    v = _dot(h, w_ref[:, 2 * a:3 * a])
    vb_ref[...] = v.astype(BF16)
    u_ref[...] = _dot(h, w_ref[:, 3 * a:])
    f = _dot(h, wf_ref[...]) + bf_ref[...]
    if time_minor:
        k_ref[0] = k.T
        v_ref[0] = v.T
        f_ref[0] = jax.nn.log_sigmoid(f.T[0:f_ref.shape[1], :])
    else:
        k_ref[...] = k
        v_ref[...] = v
        f_ref[...] = jax.nn.log_sigmoid(f[:, 0:f_ref.shape[1]])


def _inproj(x, g, w, wf, bf, *, tm, seq, time_minor, to_bf16=()):
    n, d = x.shape
    cast_specs, cast_shapes = _slab_specs(to_bf16, n // tm, lambda i: (i, 0))
    a = ATTN_W
    p = w.shape[1] - 3 * a
    nh = N_HEADS
    row = lambda width: pl.BlockSpec((tm, width), lambda i: (i, 0))
    if time_minor:
        steps = seq // tm
        feat = lambda rows: (jax.ShapeDtypeStruct((n // seq, rows, seq), F32),
                             pl.BlockSpec((1, rows, tm), lambda i: (i // steps, 0, i % steps)))
    else:
        feat = lambda rows: (jax.ShapeDtypeStruct((n, rows), F32), row(rows))
    (k_shape, k_spec), (f_shape, f_spec) = feat(a), feat(nh)
    out_shape = (
        jax.ShapeDtypeStruct((n, d), BF16),
        jax.ShapeDtypeStruct((n, a), BF16),
        jax.ShapeDtypeStruct((n, a), BF16),
        k_shape,
        k_shape,
        jax.ShapeDtypeStruct((n, a), BF16),
        jax.ShapeDtypeStruct((n, p), F32),
        f_shape,
    )
    est = (2 * _nbytes((tm, d), F32) + _nbytes(w.shape, BF16) + _nbytes(wf.shape, BF16)
           + 2 * (_nbytes((tm, d), BF16) + 3 * _nbytes((tm, a), BF16) + 2 * _nbytes((tm, a), F32)
                  + _nbytes((tm, p), F32) + _nbytes((tm, LANES), F32))
           + 4 * _nbytes((tm, a), F32) + sum(3 * _nbytes(c.shape, F32) * tm // n for c in to_bf16))
    return pl.pallas_call(
        functools.partial(_inproj_kernel, time_minor=time_minor, n_cast=len(to_bf16)),
        grid=(n // tm,),
        in_specs=[row(d), _resident((1, d)), _resident(w.shape), _resident(wf.shape), _resident((1, LANES))]
                 + cast_specs,
        out_specs=(row(d), row(a), row(a), k_spec, k_spec, row(a), row(p), f_spec) + tuple(cast_specs),
        out_shape=out_shape + cast_shapes,
        compiler_params=pltpu.CompilerParams(
            dimension_semantics=("arbitrary",), vmem_limit_bytes=_vmem_limit(est)),
        name="inproj",
    )(x, g, w, wf, bf, *to_bf16)


def _cumsum_kernel(x_ref, o_ref):
    x = x_ref[...]
    length = x.shape[1]
    lane = lax.broadcasted_iota(jnp.int32, x.shape, 1)
    shift = 1
    while shift < length:
        x = x + jnp.where(lane >= shift, pltpu.roll(x, shift, axis=1), 0.0)
        shift *= 2
    o_ref[...] = x


def _cumsum_lanes(x):
    rows, length = x.shape
    tr = _tile(rows, CUMSUM_ROWS)
    spec = pl.BlockSpec((tr, length), lambda i: (i, 0))
    return pl.pallas_call(
        _cumsum_kernel,
        grid=(rows // tr,),
        in_specs=[spec],
        out_specs=spec,
        out_shape=jax.ShapeDtypeStruct(x.shape, F32),
        compiler_params=pltpu.CompilerParams(dimension_semantics=("arbitrary",)),
        name="cumsum",
    )(x)


VALUE_ROWS = HEAD_DIM + 16


def _attn_kernel(q_ref, k_ref, vt_ref, c_ref, o_ref,
                 vta_ref, ckrep_ref, s_ref, p_ref, alpha_ref, m_ref, acc_ref, *, tq):
    t = k_ref.shape[1]
    nq = t // tq
    for a in range(HEADS_PER_BLOCK):
        vta_ref[a, 0:HEAD_DIM, :] = vt_ref[0, a * HEAD_DIM:(a + 1) * HEAD_DIM, :].astype(BF16)
        vta_ref[a, HEAD_DIM:VALUE_ROWS, :] = jnp.ones((VALUE_ROWS - HEAD_DIM, t), BF16)
        ckrep_ref[a] = jnp.broadcast_to(c_ref[0, 0, a:a + 1, :], (LANES, t)).T

    pairs = [(i, j) for i in range(nq) for j in range(i + 1)]
    rows = lambda j: slice(j * tq, (j + 1) * tq)

    def scores(w):
        i, j = pairs[w]
        q2 = q_ref[0, rows(i), :]
        lane = lax.broadcasted_iota(jnp.int32, q2.shape, 1)
        kt = k_ref[0, rows(j), :]
        for a in range(HEADS_PER_BLOCK):
            qa = jnp.where((lane // HEAD_DIM) == a, q2, jnp.zeros_like(q2))
            ckr = ckrep_ref[a, rows(j), :]
            s_ref[w % 2, a] = _dot_nt(kt, qa) - jnp.concatenate([ckr] * (tq // LANES), axis=1)

    def probs(w):
        i, j = pairs[w]
        for a in range(HEADS_PER_BLOCK):
            s = s_ref[w % 2, a]
            if j == i:
                r = lax.broadcasted_iota(jnp.int32, s.shape, 0)
                c = lax.broadcasted_iota(jnp.int32, s.shape, 1)
                s = jnp.where(r <= c, s, -jnp.inf)
            cqa = c_ref[0, 0, a:a + 1, rows(i)]
            smax = jnp.max(s, axis=0, keepdims=True) + cqa
            if j == 0:
                m_new = smax
            else:
                m_prev = m_ref[a]
                m_new = jnp.maximum(m_prev, smax)
                alpha_ref[w % 2, a] = jnp.exp(m_prev - m_new)
            p_ref[w % 2, a] = jnp.exp(s - (m_new - cqa)).astype(BF16)
            m_ref[a] = m_new

    def values(w):
        i, j = pairs[w]
        for a in range(HEADS_PER_BLOCK):
            pv = _dot(vta_ref[a, :, rows(j)], p_ref[w % 2, a])
            acc_ref[a] = pv if j == 0 else alpha_ref[w % 2, a] * acc_ref[a] + pv
        if j == i:
            ot = jnp.concatenate(
                [acc_ref[a, 0:HEAD_DIM, :] / acc_ref[a, HEAD_DIM:HEAD_DIM + 1, :]
                 for a in range(HEADS_PER_BLOCK)], axis=0)
            o_ref[0, rows(i), :] = ot.T.astype(o_ref.dtype)

    n = len(pairs)
    scores(0)
    for w in range(n):
        if w + 1 < n:
            scores(w + 1)
        probs(w)
        if w >= 1:
            values(w - 1)
    values(n - 1)


def _attn_prompt(q, kb, vt, c_rows, *, tq):
    b, t, a = q.shape
    nblk = a // LANES
    spec = pl.BlockSpec((1, t, LANES), lambda bi, hi: (bi, 0, hi))
    return pl.pallas_call(
        functools.partial(_attn_kernel, tq=tq),
        grid=(b, nblk),
        in_specs=[spec, spec, pl.BlockSpec((1, LANES, t), lambda bi, hi: (bi, hi, 0)),
                  pl.BlockSpec((1, 1, HEADS_PER_BLOCK, t), lambda bi, hi: (bi, hi, 0, 0))],
        out_specs=spec,
        out_shape=jax.ShapeDtypeStruct((b, t, a), BF16),
        scratch_shapes=[pltpu.VMEM((HEADS_PER_BLOCK, VALUE_ROWS, t), BF16),
                        pltpu.VMEM((HEADS_PER_BLOCK, t, LANES), F32),
                        pltpu.VMEM((2, HEADS_PER_BLOCK, tq, tq), F32),
                        pltpu.VMEM((2, HEADS_PER_BLOCK, tq, tq), BF16),
                        pltpu.VMEM((2, HEADS_PER_BLOCK, 1, tq), F32),
                        pltpu.VMEM((HEADS_PER_BLOCK, 1, tq), F32),
                        pltpu.VMEM((HEADS_PER_BLOCK, VALUE_ROWS, tq), F32)],
        compiler_params=pltpu.CompilerParams(dimension_semantics=("arbitrary", "arbitrary")),
        name="attn_prompt",
    )(q, kb, vt, c_rows)


def _attn_sample_kernel(q_ref, kct_ref, vct_ref, kn_ref, vn_ref, cc_ref, cnr_ref, cnc_ref, o_ref,
                        kb_ref, vb_ref, s_ref, p_ref, pn_ref):
    n, a = q_ref.shape[1], q_ref.shape[2]
    nh = a // HEAD_DIM
    q = q_ref[0]
    qt = jnp.concatenate([q] * nh, axis=0)
    row_h = lax.broadcasted_iota(jnp.int32, qt.shape, 0) // n
    col_h = lax.broadcasted_iota(jnp.int32, qt.shape, 1) // HEAD_DIM
    qbd = jnp.where(row_h == col_h, qt, jnp.zeros_like(qt))
    kb_ref[...] = kct_ref[0].astype(BF16)
    vb_ref[...] = vct_ref[0].astype(BF16)
    s_ref[...] = _dot(qbd, kb_ref[...])
    s_new = _dot_nt(qbd, kn_ref[0])
    cnc = cnc_ref[0]
    r = lax.broadcasted_iota(jnp.int32, (n, n), 0)
    c = lax.broadcasted_iota(jnp.int32, (n, n), 1)
    inv_l = []
    for h in range(nh):
        rows = slice(h * n, (h + 1) * n)
        cq = cnc[:, h:h + 1]
        sc = s_ref[rows, :] + (cq - cc_ref[0, h:h + 1, :])
        sn = s_new[rows, :] + (cq - cnr_ref[0, h:h + 1, :])
        sn = jnp.where(c <= r, sn, -jnp.inf)
        m = jnp.maximum(jnp.max(sc, axis=1, keepdims=True), jnp.max(sn, axis=1, keepdims=True))
        pc = jnp.exp(sc - m)
        pn = jnp.exp(sn - m)
        inv_l.append(1.0 / (jnp.sum(pc, axis=1, keepdims=True) + jnp.sum(pn, axis=1, keepdims=True)))
        p_ref[rows, :] = pc.astype(BF16)
        pn_ref[rows, :] = pn.astype(BF16)
    o = _dot_nt(p_ref[...], vb_ref[...]) + _dot(pn_ref[...], vn_ref[0])
    for h in range(nh):
        rows = slice(h * n, (h + 1) * n)
        cols = slice(h * HEAD_DIM, (h + 1) * HEAD_DIM)
        o_ref[0, :, cols] = (o[rows, cols] * inv_l[h]).astype(o_ref.dtype)


def _attn_sample(q, kct, vct, kn, vn, cc, cnr, cnc):
    b, n, a = q.shape
    past = kct.shape[2]
    nh = a // HEAD_DIM
    per_b = lambda *tail: pl.BlockSpec((1,) + tail, lambda bi: (bi,) + (0,) * len(tail))
    est = (4 * _nbytes((a, past), F32) + 2 * _nbytes((a, past), BF16)
           + _nbytes((nh * n, past), F32) + _nbytes((nh * n, past), BF16) + 4 * _nbytes((nh * n, a), F32))
    return pl.pallas_call(
        _attn_sample_kernel,
        grid=(b,),
        in_specs=[per_b(n, a), per_b(a, past), per_b(a, past), per_b(n, a), per_b(n, a),
                  per_b(nh, past), per_b(nh, n), per_b(n, nh)],
        out_specs=per_b(n, a),
        out_shape=jax.ShapeDtypeStruct((b, n, a), BF16),
        scratch_shapes=[pltpu.VMEM((a, past), BF16), pltpu.VMEM((a, past), BF16),
                        pltpu.VMEM((nh * n, past), F32), pltpu.VMEM((nh * n, past), BF16),
                        pltpu.VMEM((nh * n, n), BF16)],
        compiler_params=pltpu.CompilerParams(
            dimension_semantics=("arbitrary",), vmem_limit_bytes=_vmem_limit(est)),
        name="attn_sample",
    )(q, kct, vct, kn, vn, cc, cnr, cnc)


def _merge_kernel(a_ref, h_ref, u_ref, hist_ref, x_ref, wup_ref, wga_ref, wgb_ref, wpool_ref, ps_ref,
                  wout_ref, gpost_ref, gpre_ref, *rest, pos0, zero_first, n_cast):
    ext_ref, m_ref = rest[len(rest) - 2:]
    x1_ref, h2_ref = _convert_slabs(rest[:len(rest) - 2], n_cast)
    nseg, tl, _ = u_ref.shape
    i = pl.program_id(1)
    hist = hist_ref[...]
    if zero_first:
        hist = jnp.where(i == 0, 0.0, hist)
    ext_ref[:, 0:HIST_ROWS, :] = hist
    ext_ref[:, HIST_ROWS:HIST_ROWS + tl, :] = u_ref[...]
    gw = u_ref.shape[2] // POOL_GROUPS
    ogw = wpool_ref.shape[2]
    pos = pos0 + i * tl + lax.broadcasted_iota(jnp.int32, (nseg, tl, gw), 1)
    a = a_ref[...]
    h = h_ref[...]
    for g, w in enumerate(POOL_WINDOWS):
        os_ = slice(g * ogw, (g + 1) * ogw)
        br_a = _dot(a, wup_ref[:, os_])
        ga = _dot(h, wga_ref[:, os_])
        gb = _dot(h, wgb_ref[:, os_])
        cs = slice(g * gw, (g + 1) * gw)
        cur = ext_ref[:, HIST_ROWS:HIST_ROWS + tl, cs]
        tot = cur
        for s in range(1, w):
            tot = tot + ext_ref[:, HIST_ROWS - s:HIST_ROWS - s + tl, cs]
        cnt = jnp.minimum(pos + 1, w).astype(F32)
        pooled = (tot / cnt - cur).reshape(nseg * tl, gw).astype(BF16)
        br_b = _dot(pooled, wpool_ref[g]) * ps_ref[:, os_]
        m_ref[:, os_] = (jax.nn.sigmoid(ga) * br_a + jax.nn.sigmoid(gb) * br_b).astype(BF16)
    x1 = x_ref[...] + _rms(_dot(m_ref[...], wout_ref[...])) * gpost_ref[...]
    x1_ref[...] = x1
    h2_ref[...] = (_rms(x1) * gpre_ref[...]).astype(BF16)


def _merge(a, h, u3, hist, hist_map, x, wup, wga, wgb, wpool, ps, wout, gpost, gpre,
           *, nseg, tl, pos0, zero_first, to_bf16=()):
    s_total, l_total, p = u3.shape
    n, d = x.shape
    tm = nseg * tl
    steps = l_total // tl
    assert nseg == 1 or steps == 1
    row = lambda width: pl.BlockSpec((tm, width), lambda s, i: (s * steps + i, 0))
    n_steps = (s_total // nseg) * steps
    cast_specs, cast_shapes = _slab_specs(to_bf16, n_steps, lambda s, i: (s * steps + i, 0))
    est = (_nbytes(wup.shape, BF16) + 2 * _nbytes(wga.shape, BF16) + _nbytes(wpool.shape, BF16)
           + _nbytes(wout.shape, BF16)
           + 2 * (_nbytes((tm, a.shape[1]), BF16) + 2 * _nbytes((tm, d), BF16) + 2 * _nbytes((tm, d), F32)
                  + _nbytes((tm + HIST_ROWS, p), F32))
           + _nbytes((tm + nseg * HIST_ROWS, p), F32) + _nbytes((tm, d), BF16) + 6 * _nbytes((tm, d), F32)
           + sum(3 * _nbytes(w.shape, F32) // n_steps for w in to_bf16))
    return pl.pallas_call(
        functools.partial(_merge_kernel, pos0=pos0, zero_first=zero_first, n_cast=len(to_bf16)),
        grid=(s_total // nseg, steps),
        in_specs=[row(a.shape[1]), row(d),
                  pl.BlockSpec((nseg, tl, p), lambda s, i: (s, i, 0)),
                  pl.BlockSpec((nseg, HIST_ROWS, p), hist_map),
                  row(d),
                  _resident(wup.shape), _resident(wga.shape), _resident(wgb.shape), _resident(wpool.shape),
                  _resident((1, d)), _resident(wout.shape), _resident((1, d)), _resident((1, d))]
                 + cast_specs,
        out_specs=(row(d), row(d)) + tuple(cast_specs),
        out_shape=(jax.ShapeDtypeStruct((n, d), F32), jax.ShapeDtypeStruct((n, d), BF16)) + cast_shapes,
        scratch_shapes=[pltpu.VMEM((nseg, HIST_ROWS + tl, p), F32), pltpu.VMEM((tm, d), BF16)],
        compiler_params=pltpu.CompilerParams(
            dimension_semantics=("arbitrary", "arbitrary"), vmem_limit_bytes=_vmem_limit(est)),
        name="merge",
    )(a, h, u3, hist, x, wup, wga, wgb, wpool, ps, wout, gpost, gpre, *to_bf16)


def _ffn_kernel(h_ref, x1_ref, w1_ref, w2_ref, g_ref, o_ref):
    f = pl.program_id(1)

    @pl.when(f == 0)
    def _():
        o_ref[...] = jnp.zeros_like(o_ref)

    z = jnp.square(jnp.maximum(_dot(h_ref[...], w1_ref[...]), 0.0)).astype(BF16)
    o_ref[...] += _dot(z, w2_ref[...])

    @pl.when(f == pl.num_programs(1) - 1)
    def _():
        o_ref[...] = x1_ref[...] + _rms(o_ref[...]) * g_ref[...]


def _ffn(h2, x1, w1, w2, g, *, tm, tf):
    n, d = x1.shape
    dff = w1.shape[1]
    est = (2 * _nbytes((tm, d), BF16) + 4 * _nbytes((tm, d), F32)
           + 4 * _nbytes((d, tf), BF16) + 2 * _nbytes((tm, tf), F32) + _nbytes((tm, d), F32))
    return pl.pallas_call(
        _ffn_kernel,
        grid=(n // tm, dff // tf),
        in_specs=[pl.BlockSpec((tm, d), lambda i, f: (i, 0)),
                  pl.BlockSpec((tm, d), lambda i, f: (i, 0)),
                  pl.BlockSpec((d, tf), lambda i, f: (0, f)),
                  pl.BlockSpec((tf, d), lambda i, f: (f, 0)),
                  pl.BlockSpec((1, d), lambda i, f: (0, 0))],
        out_specs=pl.BlockSpec((tm, d), lambda i, f: (i, 0)),
        out_shape=jax.ShapeDtypeStruct((n, d), F32),
        compiler_params=pltpu.CompilerParams(
            dimension_semantics=("arbitrary", "arbitrary"), vmem_limit_bytes=_vmem_limit(est)),
        name="ffn",
    )(h2, x1, w1, w2, g)


def _tile(n, pref):
    t = min(n, pref)
    while n % t:
        t //= 2
    return t


INPROJ_TM = 256
ATTN_TQ = 512
MERGE_TM = 256
FFN_TM = 512
FFN_TF = 1024
CUMSUM_ROWS = 32
REPACK_ROWS = 256


def _cum_logf_rows(logf_bht):
    b, h, t = logf_bht.shape
    rows = logf_bht.reshape(b * h, t)
    pad = (-t) % LANES
    if pad:
        rows = jnp.pad(rows, ((0, 0), (0, pad)))
    return _cumsum_lanes(rows)


def _time_minor(x, nbatch):
    return jnp.transpose(x, (0, 2, 3, 1)).reshape(nbatch, -1, x.shape[1])


def _layer(xp, xs, cache_k, cache_v, cache_logf, state_pool, g_mix_pre, w_in, b_f, w_attn_up, w_pool,
           pool_scale, w_out, g_mix_post, g_ffn_pre, w_ff1, w_ff2, g_ffn_post):
    bp, tp, d = xp.shape
    bs, ts, _ = xs.shape
    past = cache_k.shape[1]
    a, nh, p = ATTN_W, N_HEADS, w_pool.shape[0] * w_pool.shape[1]
    off_f = 3 * a
    off_u = off_f + nh
    off_ga = off_u + p
    off_gb = off_ga + d

    w_qkvu, w_f, w_ga, w_gb = _repack_w_in(jnp.transpose(w_in.astype(F32)), off_f=off_f, off_u=off_u,
                                           off_ga=off_ga, off_gb=off_gb, tr=_tile(w_in.shape[0], REPACK_ROWS))
    b_f2 = jnp.pad(b_f.reshape(1, -1), ((0, 0), (0, LANES - nh)))
    row = lambda v: v.reshape(1, -1)
    g_pre, ps = row(g_mix_pre), row(pool_scale)
    g_post, g_fpre, g_fpost = row(g_mix_post), row(g_ffn_pre), row(g_ffn_post)

    def project(x3, time_minor, to_bf16=()):
        x2 = x3.reshape(-1, d)
        tm = _tile(x3.shape[1] if time_minor else x2.shape[0], INPROJ_TM)
        return x2, _inproj(x2, g_pre, w_qkvu, w_f, b_f2, tm=tm, seq=x3.shape[1], time_minor=time_minor,
                           to_bf16=to_bf16)

    def finish(x2, h, a_out, u3, hist, hist_map, nseg, tl, pos0, zero_first, ffn_w):
        to_bf16 = () if ffn_w else (w_ff1.astype(F32), w_ff2.astype(F32))
        x1, h2, *cast = _merge(a_out, h, u3, hist, hist_map, x2, w_up, w_ga, w_gb, w_pl, ps, w_o, g_post, g_fpre,
                               nseg=nseg, tl=tl, pos0=pos0, zero_first=zero_first, to_bf16=to_bf16)
        w1, w2 = ffn_w or cast
        n = x2.shape[0]
        return _ffn(h2, x1, w1, w2, g_fpost, tm=_tile(n, FFN_TM), tf=_tile(w1.shape[1], FFN_TF)), (w1, w2)

    merge_w = (w_attn_up.astype(F32), w_pool.astype(F32).reshape(p, -1), w_out.astype(F32))
    x2, (h, q, kb, kt, vt, _, u, logft, w_up, w_pl, w_o) = project(xp, True, merge_w)
    w_pl = w_pl.reshape(w_pool.shape)
    c_rows = _cum_logf_rows(logft)
    nblk = nh // HEADS_PER_BLOCK
    three = lambda z: z.reshape(bp, tp, a)
    a_out = _attn_prompt(three(q), three(kb), vt, c_rows.reshape(bp, nblk, HEADS_PER_BLOCK, tp),
                         tq=_tile(tp, ATTN_TQ))
    to_bthd = lambda zt: jnp.transpose(zt.reshape(bp, nh, HEAD_DIM, tp), (0, 3, 1, 2))
    u3 = u.reshape(bp, tp, p)
    tl = _tile(tp, MERGE_TM)
    blocks_per_tile = tl // HIST_ROWS
    hist_map = lambda s, i: (s, jnp.maximum(i * blocks_per_tile - 1, 0), 0)
    yp, ffn_w = finish(x2, h, a_out.reshape(bp * tp, a), u3, u3, hist_map, 1, tl, 0, True, None)
    yp = yp.reshape(bp, tp, d)
    prompt_out = (yp, to_bthd(kt), to_bthd(vt), jnp.transpose(logft, (0, 2, 1)),
                  jnp.concatenate([jnp.zeros((bp, POOL_HIST, p), F32), u3], axis=1)[:, -POOL_HIST:])

    x2, (h, q, kb, k, v, vb, u, logf) = project(xs, False)
    logf3 = logf.reshape(bs, ts, nh)
    f_all = jnp.concatenate([cache_logf.astype(F32), logf3], axis=1)
    c_all = _cum_logf_rows(jnp.transpose(f_all, (0, 2, 1))).reshape(bs, nh, -1)
    c_cache = c_all[:, :, :past]
    c_new = c_all[:, :, past:past + ts]
    three = lambda z: z.reshape(bs, ts, a)
    a_out = _attn_sample(three(q), _time_minor(cache_k.astype(F32), bs), _time_minor(cache_v.astype(F32), bs),
                         three(kb), three(vb), c_cache, c_new, jnp.transpose(c_new, (0, 2, 1)))
    u3 = u.reshape(bs, ts, p)
    hist = jnp.pad(state_pool.astype(F32), ((0, 0), (HIST_ROWS - POOL_HIST, 0), (0, 0)))
    ys, _ = finish(x2, h, a_out.reshape(bs * ts, a), u3, hist, lambda s, i: (s, 0, 0), bs, ts, past, False, ffn_w)
    ys = ys.reshape(bs, ts, d)
    sample_out = (ys, k.reshape(bs, ts, nh, HEAD_DIM), v.reshape(bs, ts, nh, HEAD_DIM), logf3,
                  jnp.concatenate([state_pool.astype(F32), u3], axis=1)[:, -POOL_HIST:])
    return prompt_out, sample_out


def kernel(x_prompt, x_sample, cache_k, cache_v, cache_logf, state_pool, g_mix_pre, w_in, b_f, w_attn_up,
           w_pool, pool_scale, w_out, g_mix_post, g_ffn_pre, w_ff1, w_ff2, g_ffn_post):
    depth = w_in.shape[0]
    xp, xs = x_prompt, x_sample
    per_layer = []
    for l in range(depth):
        po, so = _layer(xp, xs, cache_k[l], cache_v[l], cache_logf[l], state_pool[l], g_mix_pre[l], w_in[l],
                        b_f[l], w_attn_up[l], w_pool[l], pool_scale[l], w_out[l], g_mix_post[l],
                        g_ffn_pre[l], w_ff1[l], w_ff2[l], g_ffn_post[l])
        xp, xs = po[0], so[0]
        per_layer.append(po[1:] + so[1:])
    stacked = [jnp.stack(leaves, 0) for leaves in zip(*per_layer)]
    return (xp, xs, *stacked)
```

```python
import functools
import math

import jax
import jax.numpy as jnp
from jax import lax
from jax.experimental import pallas as pl
from jax.experimental.pallas import tpu as pltpu

N_HEADS = 16
HEAD_DIM = 64
ATTN_W = N_HEADS * HEAD_DIM
POOL_WINDOWS = (2, 4, 8, 16)
POOL_GROUPS = len(POOL_WINDOWS)
POOL_HIST = max(POOL_WINDOWS) - 1
EPS = 1e-6
SCALE = HEAD_DIM ** -0.5

LANES = 128
VMEM_LIMIT_CAP = 60000 * 1024

HEADS_PER_BLOCK = LANES // HEAD_DIM
HIST_ROWS = 16

F32 = jnp.float32
BF16 = jnp.bfloat16


def _vmem_limit(nbytes):
    return int(min(VMEM_LIMIT_CAP, max(32 * 1024 * 1024, nbytes * 5 // 4)))


def _nbytes(shape, dtype):
    return math.prod(shape) * jnp.dtype(dtype).itemsize


def _rms(x):
    return x * lax.rsqrt(jnp.mean(x * x, axis=-1, keepdims=True) + EPS)


def _dot(a, b):
    return jnp.dot(a, b, preferred_element_type=F32)


def _dot_nt(a, b):
    return lax.dot_general(a, b, (((1,), (1,)), ((), ())), preferred_element_type=F32)


def _resident(shape):
    zeros = (0,) * len(shape)
    return pl.BlockSpec(shape, lambda *_: zeros, pipeline_mode=pl.Buffered(1))


def _repack_kernel(wt_ref, qkvu_ref, f_ref, ga_ref, gb_ref, *, off_f, off_u, off_ga, off_gb):
    nh = off_u - off_f
    p = off_ga - off_u
    d = off_gb - off_ga
    piece = lambda lo, hi: wt_ref[lo:hi, :].T.astype(BF16)
    qkvu_ref[:, 0:off_f] = piece(0, off_f)
    qkvu_ref[:, off_f:off_f + p] = piece(off_u, off_ga)
    fwin = wt_ref[off_f:off_f + LANES, :].T
    lane = lax.broadcasted_iota(jnp.int32, fwin.shape, 1)
    f_ref[...] = jnp.where(lane < nh, fwin, 0.0).astype(BF16)
    ga_ref[...] = piece(off_ga, off_gb)
    gb_ref[...] = piece(off_gb, off_gb + d)


def _repack_w_in(wt, *, off_f, off_u, off_ga, off_gb, tr):
    width, rows = wt.shape
    p = off_ga - off_u
    d = off_gb - off_ga
    blk = lambda cols: pl.BlockSpec((tr, cols), lambda i: (i, 0))
    est = 4 * _nbytes((width, tr), F32) + 2 * _nbytes((tr, off_f + p + LANES + 2 * d), BF16)
    return pl.pallas_call(
        functools.partial(_repack_kernel, off_f=off_f, off_u=off_u, off_ga=off_ga, off_gb=off_gb),
        grid=(rows // tr,),
        in_specs=[pl.BlockSpec((width, tr), lambda i: (0, i))],
        out_specs=(blk(off_f + p), blk(LANES), blk(d), blk(d)),
        out_shape=(jax.ShapeDtypeStruct((rows, off_f + p), BF16), jax.ShapeDtypeStruct((rows, LANES), BF16),
                   jax.ShapeDtypeStruct((rows, d), BF16), jax.ShapeDtypeStruct((rows, d), BF16)),
        compiler_params=pltpu.CompilerParams(
            dimension_semantics=("arbitrary",), vmem_limit_bytes=_vmem_limit(est)),
        name="repack_w_in",
    )(wt)


def _convert_slabs(refs, n_cast):
    for src, dst in zip(refs[:n_cast], refs[len(refs) - n_cast:]):
        dst[...] = src[...].astype(dst.dtype)
    return refs[n_cast:len(refs) - n_cast]


def _slab_specs(to_bf16, n_steps, index_map):
    specs = [pl.BlockSpec((w.shape[0] // n_steps, w.shape[1]), index_map) for w in to_bf16]
    return specs, tuple(jax.ShapeDtypeStruct(w.shape, BF16) for w in to_bf16)


def _inproj_kernel(x_ref, g_ref, w_ref, wf_ref, bf_ref, *rest, time_minor, n_cast):
    h_ref, q_ref, kb_ref, k_ref, v_ref, vb_ref, u_ref, f_ref = _convert_slabs(rest, n_cast)
    h = (_rms(x_ref[...]) * g_ref[...]).astype(BF16)
    h_ref[...] = h
    a = q_ref.shape[1]
    f = _dot(h, wf_ref[...]) + bf_ref[...]
    k = _dot(h, w_ref[:, a:2 * a])
    kb_ref[...] = k.astype(BF16)
    v = _dot(h, w_ref[:, 2 * a:3 * a])
    vb_ref[...] = v.astype(BF16)
    q_ref[...] = (_dot(h, w_ref[:, 0:a]) * SCALE).astype(BF16)
    u_ref[...] = _dot(h, w_ref[:, 3 * a:])
    if time_minor:
        k_ref[0] = k.T
        v_ref[0] = v.T
        f_ref[0] = jax.nn.log_sigmoid(f.T[0:f_ref.shape[1], :])
    else:
        k_ref[...] = k
        v_ref[...] = v
        f_ref[...] = jax.nn.log_sigmoid(f[:, 0:f_ref.shape[1]])


def _inproj(x, g, w, wf, bf, *, tm, seq, time_minor, to_bf16=()):
    n, d = x.shape
    cast_specs, cast_shapes = _slab_specs(to_bf16, n // tm, lambda i: (i, 0))
    a = ATTN_W
    p = w.shape[1] - 3 * a
    nh = N_HEADS
    row = lambda width: pl.BlockSpec((tm, width), lambda i: (i, 0))
    if time_minor:
        steps = seq // tm
        feat = lambda rows: (jax.ShapeDtypeStruct((n // seq, rows, seq), F32),
                             pl.BlockSpec((1, rows, tm), lambda i: (i // steps, 0, i % steps)))
    else:
        feat = lambda rows: (jax.ShapeDtypeStruct((n, rows), F32), row(rows))
    (k_shape, k_spec), (f_shape, f_spec) = feat(a), feat(nh)
    out_shape = (
        jax.ShapeDtypeStruct((n, d), BF16),
        jax.ShapeDtypeStruct((n, a), BF16),
        jax.ShapeDtypeStruct((n, a), BF16),
        k_shape,
        k_shape,
        jax.ShapeDtypeStruct((n, a), BF16),
        jax.ShapeDtypeStruct((n, p), F32),
        f_shape,
    )
    est = (2 * _nbytes((tm, d), F32) + _nbytes(w.shape, BF16) + _nbytes(wf.shape, BF16)
           + 2 * (_nbytes((tm, d), BF16) + 3 * _nbytes((tm, a), BF16) + 2 * _nbytes((tm, a), F32)
                  + _nbytes((tm, p), F32) + _nbytes((tm, LANES), F32))
           + 4 * _nbytes((tm, a), F32) + sum(3 * _nbytes(c.shape, F32) * tm // n for c in to_bf16))
    return pl.pallas_call(
        functools.partial(_inproj_kernel, time_minor=time_minor, n_cast=len(to_bf16)),
        grid=(n // tm,),
        in_specs=[row(d), _resident((1, d)), _resident(w.shape), _resident(wf.shape), _resident((1, LANES))]
                 + cast_specs,
        out_specs=(row(d), row(a), row(a), k_spec, k_spec, row(a), row(p), f_spec) + tuple(cast_specs),
        out_shape=out_shape + cast_shapes,
        compiler_params=pltpu.CompilerParams(
            dimension_semantics=("arbitrary",), vmem_limit_bytes=_vmem_limit(est)),
        name="inproj",
    )(x, g, w, wf, bf, *to_bf16)


def _cumsum_kernel(x_ref, o_ref):
    x = x_ref[...]
    length = x.shape[1]
    lane = lax.broadcasted_iota(jnp.int32, x.shape, 1)
    shift = 1
    while shift < length:
        x = x + jnp.where(lane >= shift, pltpu.roll(x, shift, axis=1), 0.0)
        shift *= 2
    o_ref[...] = x


def _cumsum_lanes(x):
    rows, length = x.shape
    tr = _tile(rows, CUMSUM_ROWS)
    spec = pl.BlockSpec((tr, length), lambda i: (i, 0))
    return pl.pallas_call(
        _cumsum_kernel,
        grid=(rows // tr,),
        in_specs=[spec],
        out_specs=spec,
        out_shape=jax.ShapeDtypeStruct(x.shape, F32),
        compiler_params=pltpu.CompilerParams(dimension_semantics=("arbitrary",)),
        name="cumsum",
    )(x)


VALUE_ROWS = HEAD_DIM + 16


def _attn_kernel(q_ref, k_ref, vt_ref, c_ref, o_ref,
                 vta_ref, ckrep_ref, s_ref, p_ref, alpha_ref, m_ref, acc_ref, *, tq):
    t = k_ref.shape[1]
    nq = t // tq
    for a in range(HEADS_PER_BLOCK):
        vta_ref[a, 0:HEAD_DIM, :] = vt_ref[0, a * HEAD_DIM:(a + 1) * HEAD_DIM, :].astype(BF16)
        vta_ref[a, HEAD_DIM:VALUE_ROWS, :] = jnp.ones((VALUE_ROWS - HEAD_DIM, t), BF16)
        ckrep_ref[a] = jnp.broadcast_to(c_ref[0, 0, a:a + 1, :], (LANES, t)).T

    pairs = [(i, j) for i in range(nq) for j in range(i + 1)]
    rows = lambda j: slice(j * tq, (j + 1) * tq)

    def scores(w):
        i, j = pairs[w]
        q2 = q_ref[0, rows(i), :]
        lane = lax.broadcasted_iota(jnp.int32, q2.shape, 1)
        kt = k_ref[0, rows(j), :]
        for a in range(HEADS_PER_BLOCK):
            qa = jnp.where((lane // HEAD_DIM) == a, q2, jnp.zeros_like(q2))
            ckr = ckrep_ref[a, rows(j), :]
            s_ref[w % 2, a] = _dot_nt(kt, qa) - jnp.concatenate([ckr] * (tq // LANES), axis=1)

    def probs(w):
        i, j = pairs[w]
        for a in range(HEADS_PER_BLOCK):
            s = s_ref[w % 2, a]
            if j == i:
                r = lax.broadcasted_iota(jnp.int32, s.shape, 0)
                c = lax.broadcasted_iota(jnp.int32, s.shape, 1)
                s = jnp.where(r <= c, s, -jnp.inf)
            cqa = c_ref[0, 0, a:a + 1, rows(i)]
            smax = jnp.max(s, axis=0, keepdims=True) + cqa
            if j == 0:
                m_new = smax
            else:
                m_prev = m_ref[a]
                m_new = jnp.maximum(m_prev, smax)
                alpha_ref[w % 2, a] = jnp.exp(m_prev - m_new)
            p_ref[w % 2, a] = jnp.exp(s - (m_new - cqa)).astype(BF16)
            m_ref[a] = m_new

    def values(w):
        i, j = pairs[w]
        for a in range(HEADS_PER_BLOCK):
            pv = _dot(vta_ref[a, :, rows(j)], p_ref[w % 2, a])
            acc_ref[a] = pv if j == 0 else alpha_ref[w % 2, a] * acc_ref[a] + pv
        if j == i:
            ot = jnp.concatenate(
                [acc_ref[a, 0:HEAD_DIM, :] / acc_ref[a, HEAD_DIM:HEAD_DIM + 1, :]
                 for a in range(HEADS_PER_BLOCK)], axis=0)
            o_ref[0, rows(i), :] = ot.T.astype(o_ref.dtype)

    n = len(pairs)
    scores(0)
    for w in range(n):
        if w + 1 < n:
            scores(w + 1)
        probs(w)
        if w >= 1:
            values(w - 1)
    values(n - 1)


def _attn_prompt(q, kb, vt, c_rows, *, tq):
    b, t, a = q.shape
    nblk = a // LANES
    spec = pl.BlockSpec((1, t, LANES), lambda bi, hi: (bi, 0, hi))
    return pl.pallas_call(
        functools.partial(_attn_kernel, tq=tq),
        grid=(b, nblk),
        in_specs=[spec, spec, pl.BlockSpec((1, LANES, t), lambda bi, hi: (bi, hi, 0)),
                  pl.BlockSpec((1, 1, HEADS_PER_BLOCK, t), lambda bi, hi: (bi, hi, 0, 0))],
        out_specs=spec,
        out_shape=jax.ShapeDtypeStruct((b, t, a), BF16),
        scratch_shapes=[pltpu.VMEM((HEADS_PER_BLOCK, VALUE_ROWS, t), BF16),
                        pltpu.VMEM((HEADS_PER_BLOCK, t, LANES), F32),
                        pltpu.VMEM((2, HEADS_PER_BLOCK, tq, tq), F32),
                        pltpu.VMEM((2, HEADS_PER_BLOCK, tq, tq), BF16),
                        pltpu.VMEM((2, HEADS_PER_BLOCK, 1, tq), F32),
                        pltpu.VMEM((HEADS_PER_BLOCK, 1, tq), F32),
                        pltpu.VMEM((HEADS_PER_BLOCK, VALUE_ROWS, tq), F32)],
        compiler_params=pltpu.CompilerParams(dimension_semantics=("arbitrary", "arbitrary")),
        name="attn_prompt",
    )(q, kb, vt, c_rows)


def _attn_sample_kernel(q_ref, kct_ref, vct_ref, kn_ref, vn_ref, cc_ref, cnr_ref, cnc_ref, o_ref,
                        kb_ref, vb_ref, s_ref, p_ref, pn_ref):
    n, a = q_ref.shape[1], q_ref.shape[2]
    nh = a // HEAD_DIM
    q = q_ref[0]
    qt = jnp.concatenate([q] * nh, axis=0)
    row_h = lax.broadcasted_iota(jnp.int32, qt.shape, 0) // n
    col_h = lax.broadcasted_iota(jnp.int32, qt.shape, 1) // HEAD_DIM
    qbd = jnp.where(row_h == col_h, qt, jnp.zeros_like(qt))
    kb_ref[...] = kct_ref[0].astype(BF16)
    vb_ref[...] = vct_ref[0].astype(BF16)
    s_ref[...] = _dot(qbd, kb_ref[...])
    s_new = _dot_nt(qbd, kn_ref[0])
    cnc = cnc_ref[0]
    r = lax.broadcasted_iota(jnp.int32, (n, n), 0)
    c = lax.broadcasted_iota(jnp.int32, (n, n), 1)
    inv_l = []
    for h in range(nh):
        rows = slice(h * n, (h + 1) * n)
        cq = cnc[:, h:h + 1]
        sc = s_ref[rows, :] + (cq - cc_ref[0, h:h + 1, :])
        sn = s_new[rows, :] + (cq - cnr_ref[0, h:h + 1, :])
        sn = jnp.where(c <= r, sn, -jnp.inf)
        m = jnp.maximum(jnp.max(sc, axis=1, keepdims=True), jnp.max(sn, axis=1, keepdims=True))
        pc = jnp.exp(sc - m)
        pn = jnp.exp(sn - m)
        inv_l.append(1.0 / (jnp.sum(pc, axis=1, keepdims=True) + jnp.sum(pn, axis=1, keepdims=True)))
        p_ref[rows, :] = pc.astype(BF16)
        pn_ref[rows, :] = pn.astype(BF16)
    o = _dot_nt(p_ref[...], vb_ref[...]) + _dot(pn_ref[...], vn_ref[0])
    for h in range(nh):
        rows = slice(h * n, (h + 1) * n)
        cols = slice(h * HEAD_DIM, (h + 1) * HEAD_DIM)
        o_ref[0, :, cols] = (o[rows, cols] * inv_l[h]).astype(o_ref.dtype)


def _attn_sample(q, kct, vct, kn, vn, cc, cnr, cnc):
    b, n, a = q.shape
    past = kct.shape[2]
    nh = a // HEAD_DIM
    per_b = lambda *tail: pl.BlockSpec((1,) + tail, lambda bi: (bi,) + (0,) * len(tail))
    est = (4 * _nbytes((a, past), F32) + 2 * _nbytes((a, past), BF16)
           + _nbytes((nh * n, past), F32) + _nbytes((nh * n, past), BF16) + 4 * _nbytes((nh * n, a), F32))
    return pl.pallas_call(
        _attn_sample_kernel,
        grid=(b,),
        in_specs=[per_b(n, a), per_b(a, past), per_b(a, past), per_b(n, a), per_b(n, a),
                  per_b(nh, past), per_b(nh, n), per_b(n, nh)],
        out_specs=per_b(n, a),
        out_shape=jax.ShapeDtypeStruct((b, n, a), BF16),
        scratch_shapes=[pltpu.VMEM((a, past), BF16), pltpu.VMEM((a, past), BF16),
                        pltpu.VMEM((nh * n, past), F32), pltpu.VMEM((nh * n, past), BF16),
                        pltpu.VMEM((nh * n, n), BF16)],
        compiler_params=pltpu.CompilerParams(
            dimension_semantics=("arbitrary",), vmem_limit_bytes=_vmem_limit(est)),
        name="attn_sample",
    )(q, kct, vct, kn, vn, cc, cnr, cnc)


def _merge_kernel(a_ref, h_ref, u_ref, hist_ref, x_ref, wup_ref, wga_ref, wgb_ref, wpool_ref, ps_ref,
                  wout_ref, gpost_ref, gpre_ref, *rest, pos0, zero_first, n_cast):
    ext_ref, m_ref = rest[len(rest) - 2:]
    x1_ref, h2_ref = _convert_slabs(rest[:len(rest) - 2], n_cast)
    nseg, tl, _ = u_ref.shape
    i = pl.program_id(1)
    hist = hist_ref[...]
    if zero_first:
        hist = jnp.where(i == 0, 0.0, hist)
    ext_ref[:, 0:HIST_ROWS, :] = hist
    ext_ref[:, HIST_ROWS:HIST_ROWS + tl, :] = u_ref[...]
    gw = u_ref.shape[2] // POOL_GROUPS
    ogw = wpool_ref.shape[2]
    pos = pos0 + i * tl + lax.broadcasted_iota(jnp.int32, (nseg, tl, gw), 1)
    a = a_ref[...]
    h = h_ref[...]
    for g, w in enumerate(POOL_WINDOWS):
        os_ = slice(g * ogw, (g + 1) * ogw)
        br_a = _dot(a, wup_ref[:, os_])
        ga = _dot(h, wga_ref[:, os_])
        gb = _dot(h, wgb_ref[:, os_])
        cs = slice(g * gw, (g + 1) * gw)
        cur = ext_ref[:, HIST_ROWS:HIST_ROWS + tl, cs]
        tot = cur
        for s in range(1, w):
            tot = tot + ext_ref[:, HIST_ROWS - s:HIST_ROWS - s + tl, cs]
        cnt = jnp.minimum(pos + 1, w).astype(F32)
        pooled = (tot / cnt - cur).reshape(nseg * tl, gw).astype(BF16)
        br_b = _dot(pooled, wpool_ref[g]) * ps_ref[:, os_]
        m_ref[:, os_] = (jax.nn.sigmoid(ga) * br_a + jax.nn.sigmoid(gb) * br_b).astype(BF16)
    x1 = x_ref[...] + _rms(_dot(m_ref[...], wout_ref[...])) * gpost_ref[...]
    x1_ref[...] = x1
    h2_ref[...] = (_rms(x1) * gpre_ref[...]).astype(BF16)


def _merge(a, h, u3, hist, hist_map, x, wup, wga, wgb, wpool, ps, wout, gpost, gpre,
           *, nseg, tl, pos0, zero_first, to_bf16=()):
    s_total, l_total, p = u3.shape
    n, d = x.shape
    tm = nseg * tl
    steps = l_total // tl
    assert nseg == 1 or steps == 1
    row = lambda width: pl.BlockSpec((tm, width), lambda s, i: (s * steps + i, 0))
    n_steps = (s_total // nseg) * steps
    cast_specs, cast_shapes = _slab_specs(to_bf16, n_steps, lambda s, i: (s * steps + i, 0))
    est = (_nbytes(wup.shape, BF16) + 2 * _nbytes(wga.shape, BF16) + _nbytes(wpool.shape, BF16)
           + _nbytes(wout.shape, BF16)
           + 2 * (_nbytes((tm, a.shape[1]), BF16) + 2 * _nbytes((tm, d), BF16) + 2 * _nbytes((tm, d), F32)
                  + _nbytes((tm + HIST_ROWS, p), F32))
           + _nbytes((tm + nseg * HIST_ROWS, p), F32) + _nbytes((tm, d), BF16) + 6 * _nbytes((tm, d), F32)
           + sum(3 * _nbytes(w.shape, F32) // n_steps for w in to_bf16))
    return pl.pallas_call(
        functools.partial(_merge_kernel, pos0=pos0, zero_first=zero_first, n_cast=len(to_bf16)),
        grid=(s_total // nseg, steps),
        in_specs=[row(a.shape[1]), row(d),
                  pl.BlockSpec((nseg, tl, p), lambda s, i: (s, i, 0)),
                  pl.BlockSpec((nseg, HIST_ROWS, p), hist_map),
                  row(d),
                  _resident(wup.shape), _resident(wga.shape), _resident(wgb.shape), _resident(wpool.shape),
                  _resident((1, d)), _resident(wout.shape), _resident((1, d)), _resident((1, d))]
                 + cast_specs,
        out_specs=(row(d), row(d)) + tuple(cast_specs),
        out_shape=(jax.ShapeDtypeStruct((n, d), F32), jax.ShapeDtypeStruct((n, d), BF16)) + cast_shapes,
        scratch_shapes=[pltpu.VMEM((nseg, HIST_ROWS + tl, p), F32), pltpu.VMEM((tm, d), BF16)],
        compiler_params=pltpu.CompilerParams(
            dimension_semantics=("arbitrary", "arbitrary"), vmem_limit_bytes=_vmem_limit(est)),
        name="merge",
    )(a, h, u3, hist, x, wup, wga, wgb, wpool, ps, wout, gpost, gpre, *to_bf16)


def _ffn_kernel(h_ref, x1_ref, w1_ref, w2_ref, g_ref, o_ref):
    f = pl.program_id(1)

    @pl.when(f == 0)
    def _():
        o_ref[...] = jnp.zeros_like(o_ref)

    z = jnp.square(jnp.maximum(_dot(h_ref[...], w1_ref[...]), 0.0)).astype(BF16)
    o_ref[...] += _dot(z, w2_ref[...])

    @pl.when(f == pl.num_programs(1) - 1)
    def _():
        o_ref[...] = x1_ref[...] + _rms(o_ref[...]) * g_ref[...]


def _ffn(h2, x1, w1, w2, g, *, tm, tf):
    n, d = x1.shape
    dff = w1.shape[1]
    est = (2 * _nbytes((tm, d), BF16) + 4 * _nbytes((tm, d), F32)
           + 4 * _nbytes((d, tf), BF16) + 2 * _nbytes((tm, tf), F32) + _nbytes((tm, d), F32))
    return pl.pallas_call(
        _ffn_kernel,
        grid=(n // tm, dff // tf),
        in_specs=[pl.BlockSpec((tm, d), lambda i, f: (i, 0)),
                  pl.BlockSpec((tm, d), lambda i, f: (i, 0)),
                  pl.BlockSpec((d, tf), lambda i, f: (0, f)),
                  pl.BlockSpec((tf, d), lambda i, f: (f, 0)),
                  pl.BlockSpec((1, d), lambda i, f: (0, 0))],
        out_specs=pl.BlockSpec((tm, d), lambda i, f: (i, 0)),
        out_shape=jax.ShapeDtypeStruct((n, d), F32),
        compiler_params=pltpu.CompilerParams(
            dimension_semantics=("arbitrary", "arbitrary"), vmem_limit_bytes=_vmem_limit(est)),
        name="ffn",
    )(h2, x1, w1, w2, g)


def _tile(n, pref):
    t = min(n, pref)
    while n % t:
        t //= 2
    return t


INPROJ_TM = 256
ATTN_TQ = 512
MERGE_TM = 256
FFN_TM = 512
FFN_TF = 1024
CUMSUM_ROWS = 32
REPACK_ROWS = 256


def _cum_logf_rows(logf_bht):
    b, h, t = logf_bht.shape
    rows = logf_bht.reshape(b * h, t)
    pad = (-t) % LANES
    if pad:
        rows = jnp.pad(rows, ((0, 0), (0, pad)))
    return _cumsum_lanes(rows)


def _time_minor(x, nbatch):
    return jnp.transpose(x, (0, 2, 3, 1)).reshape(nbatch, -1, x.shape[1])


def _layer(xp, xs, cache_k, cache_v, cache_logf, state_pool, g_mix_pre, w_in, b_f, w_attn_up, w_pool,
           pool_scale, w_out, g_mix_post, g_ffn_pre, w_ff1, w_ff2, g_ffn_post):
    bp, tp, d = xp.shape
    bs, ts, _ = xs.shape
    past = cache_k.shape[1]
    a, nh, p = ATTN_W, N_HEADS, w_pool.shape[0] * w_pool.shape[1]
    off_f = 3 * a
    off_u = off_f + nh
    off_ga = off_u + p
    off_gb = off_ga + d

    w_qkvu, w_f, w_ga, w_gb = _repack_w_in(jnp.transpose(w_in.astype(F32)), off_f=off_f, off_u=off_u,
                                           off_ga=off_ga, off_gb=off_gb, tr=_tile(w_in.shape[0], REPACK_ROWS))
    b_f2 = jnp.pad(b_f.reshape(1, -1), ((0, 0), (0, LANES - nh)))
    row = lambda v: v.reshape(1, -1)
    g_pre, ps = row(g_mix_pre), row(pool_scale)
    g_post, g_fpre, g_fpost = row(g_mix_post), row(g_ffn_pre), row(g_ffn_post)

    def project(x3, time_minor, to_bf16=()):
        x2 = x3.reshape(-1, d)
        tm = _tile(x3.shape[1] if time_minor else x2.shape[0], INPROJ_TM)
        return x2, _inproj(x2, g_pre, w_qkvu, w_f, b_f2, tm=tm, seq=x3.shape[1], time_minor=time_minor,
                           to_bf16=to_bf16)

    def finish(x2, h, a_out, u3, hist, hist_map, nseg, tl, pos0, zero_first, ffn_w):
        to_bf16 = () if ffn_w else (w_ff1.astype(F32), w_ff2.astype(F32))
        x1, h2, *cast = _merge(a_out, h, u3, hist, hist_map, x2, w_up, w_ga, w_gb, w_pl, ps, w_o, g_post, g_fpre,
                               nseg=nseg, tl=tl, pos0=pos0, zero_first=zero_first, to_bf16=to_bf16)
        w1, w2 = ffn_w or cast
        n = x2.shape[0]
        return _ffn(h2, x1, w1, w2, g_fpost, tm=_tile(n, FFN_TM), tf=_tile(w1.shape[1], FFN_TF)), (w1, w2)

    merge_w = (w_attn_up.astype(F32), w_pool.astype(F32).reshape(p, -1), w_out.astype(F32))
    x2, (h, q, kb, kt, vt, _, u, logft, w_up, w_pl, w_o) = project(xp, True, merge_w)
    w_pl = w_pl.reshape(w_pool.shape)
    c_rows = _cum_logf_rows(logft)
    nblk = nh // HEADS_PER_BLOCK
    three = lambda z: z.reshape(bp, tp, a)
    a_out = _attn_prompt(three(q), three(kb), vt, c_rows.reshape(bp, nblk, HEADS_PER_BLOCK, tp),
                         tq=_tile(tp, ATTN_TQ))
    to_bthd = lambda zt: jnp.transpose(zt.reshape(bp, nh, HEAD_DIM, tp), (0, 3, 1, 2))
    u3 = u.reshape(bp, tp, p)
    tl = _tile(tp, MERGE_TM)
    blocks_per_tile = tl // HIST_ROWS
    hist_map = lambda s, i: (s, jnp.maximum(i * blocks_per_tile - 1, 0), 0)
    yp, ffn_w = finish(x2, h, a_out.reshape(bp * tp, a), u3, u3, hist_map, 1, tl, 0, True, None)
    yp = yp.reshape(bp, tp, d)
    prompt_out = (yp, to_bthd(kt), to_bthd(vt), jnp.transpose(logft, (0, 2, 1)),
                  jnp.concatenate([jnp.zeros((bp, POOL_HIST, p), F32), u3], axis=1)[:, -POOL_HIST:])

    x2, (h, q, kb, k, v, vb, u, logf) = project(xs, False)
    logf3 = logf.reshape(bs, ts, nh)
    f_all = jnp.concatenate([cache_logf.astype(F32), logf3], axis=1)
    c_all = _cum_logf_rows(jnp.transpose(f_all, (0, 2, 1))).reshape(bs, nh, -1)
    c_cache = c_all[:, :, :past]
    c_new = c_all[:, :, past:past + ts]
    three = lambda z: z.reshape(bs, ts, a)
    a_out = _attn_sample(three(q), _time_minor(cache_k.astype(F32), bs), _time_minor(cache_v.astype(F32), bs),
                         three(kb), three(vb), c_cache, c_new, jnp.transpose(c_new, (0, 2, 1)))
    u3 = u.reshape(bs, ts, p)
    hist = jnp.pad(state_pool.astype(F32), ((0, 0), (HIST_ROWS - POOL_HIST, 0), (0, 0)))
    ys, _ = finish(x2, h, a_out.reshape(bs * ts, a), u3, hist, lambda s, i: (s, 0, 0), bs, ts, past, False, ffn_w)
    ys = ys.reshape(bs, ts, d)
    sample_out = (ys, k.reshape(bs, ts, nh, HEAD_DIM), v.reshape(bs, ts, nh, HEAD_DIM), logf3,
                  jnp.concatenate([state_pool.astype(F32), u3], axis=1)[:, -POOL_HIST:])
    return prompt_out, sample_out


def kernel(x_prompt, x_sample, cache_k, cache_v, cache_logf, state_pool, g_mix_pre, w_in, b_f, w_attn_up,
           w_pool, pool_scale, w_out, g_mix_post, g_ffn_pre, w_ff1, w_ff2, g_ffn_post):
    depth = w_in.shape[0]
    xp, xs = x_prompt, x_sample
    per_layer = []
    for l in range(depth):
        po, so = _layer(xp, xs, cache_k[l], cache_v[l], cache_logf[l], state_pool[l], g_mix_pre[l], w_in[l],
                        b_f[l], w_attn_up[l], w_pool[l], pool_scale[l], w_out[l], g_mix_post[l],
                        g_ffn_pre[l], w_ff1[l], w_ff2[l], g_ffn_post[l])
        xp, xs = po[0], so[0]
        per_layer.append(po[1:] + so[1:])
    stacked = [jnp.stack(leaves, 0) for leaves in zip(*per_layer)]
    return (xp, xs, *stacked)
```

```python
import functools
import math

import jax
import jax.numpy as jnp
from jax import lax
from jax.experimental import pallas as pl
from jax.experimental.pallas import tpu as pltpu

N_HEADS = 16
HEAD_DIM = 64
ATTN_W = N_HEADS * HEAD_DIM
POOL_WINDOWS = (2, 4, 8, 16)
POOL_GROUPS = len(POOL_WINDOWS)
POOL_HIST = max(POOL_WINDOWS) - 1
EPS = 1e-6
SCALE = HEAD_DIM ** -0.5

LANES = 128
VMEM_LIMIT_CAP = 60000 * 1024

HEADS_PER_BLOCK = LANES // HEAD_DIM
HIST_ROWS = 16

F32 = jnp.float32
BF16 = jnp.bfloat16


def _vmem_limit(nbytes):
    return int(min(VMEM_LIMIT_CAP, max(32 * 1024 * 1024, nbytes * 5 // 4)))


def _nbytes(shape, dtype):
    return math.prod(shape) * jnp.dtype(dtype).itemsize


def _rms(x):
    return x * lax.rsqrt(jnp.mean(x * x, axis=-1, keepdims=True) + EPS)


def _dot(a, b):
    return jnp.dot(a, b, preferred_element_type=F32)


def _dot_nt(a, b):
    return lax.dot_general(a, b, (((1,), (1,)), ((), ())), preferred_element_type=F32)


def _resident(shape):
    zeros = (0,) * len(shape)
    return pl.BlockSpec(shape, lambda *_: zeros, pipeline_mode=pl.Buffered(1))


def _repack_kernel(wt_ref, qkvu_ref, f_ref, ga_ref, gb_ref, *, off_f, off_u, off_ga, off_gb):
    nh = off_u - off_f
    p = off_ga - off_u
    d = off_gb - off_ga
    piece = lambda lo, hi: wt_ref[lo:hi, :].T.astype(BF16)
    qkvu_ref[:, 0:off_f] = piece(0, off_f)
    qkvu_ref[:, off_f:off_f + p] = piece(off_u, off_ga)
    fwin = wt_ref[off_f:off_f + LANES, :].T
    lane = lax.broadcasted_iota(jnp.int32, fwin.shape, 1)
    f_ref[...] = jnp.where(lane < nh, fwin, 0.0).astype(BF16)
    ga_ref[...] = piece(off_ga, off_gb)
    gb_ref[...] = piece(off_gb, off_gb + d)


def _repack_w_in(wt, *, off_f, off_u, off_ga, off_gb, tr):
    width, rows = wt.shape
    p = off_ga - off_u
    d = off_gb - off_ga
    blk = lambda cols: pl.BlockSpec((tr, cols), lambda i: (i, 0))
    est = 4 * _nbytes((width, tr), F32) + 2 * _nbytes((tr, off_f + p + LANES + 2 * d), BF16)
    return pl.pallas_call(
        functools.partial(_repack_kernel, off_f=off_f, off_u=off_u, off_ga=off_ga, off_gb=off_gb),
        grid=(rows // tr,),
        in_specs=[pl.BlockSpec((width, tr), lambda i: (0, i))],
        out_specs=(blk(off_f + p), blk(LANES), blk(d), blk(d)),
        out_shape=(jax.ShapeDtypeStruct((rows, off_f + p), BF16), jax.ShapeDtypeStruct((rows, LANES), BF16),
                   jax.ShapeDtypeStruct((rows, d), BF16), jax.ShapeDtypeStruct((rows, d), BF16)),
        compiler_params=pltpu.CompilerParams(
            dimension_semantics=("arbitrary",), vmem_limit_bytes=_vmem_limit(est)),
        name="repack_w_in",
    )(wt)


def _convert_slabs(refs, n_cast):
    for src, dst in zip(refs[:n_cast], refs[len(refs) - n_cast:]):
        if len(dst.shape) == 2:
            dst[...] = src[...].astype(dst.dtype)
        else:
            for c in range(dst.shape[0]):
                dst[c] = src[:, c * dst.shape[2]:(c + 1) * dst.shape[2]].astype(dst.dtype)
    return refs[n_cast:len(refs) - n_cast]


def _slab_specs(to_bf16, n_steps, index_map):
    in_specs, out_specs, shapes = [], [], []
    for w, chunk in to_bf16:
        rows, cols = w.shape
        slab = rows // n_steps
        in_specs.append(pl.BlockSpec((slab, cols), index_map))
        if chunk is None:
            out_specs.append(pl.BlockSpec((slab, cols), index_map))
            shapes.append(jax.ShapeDtypeStruct((rows, cols), BF16))
        else:
            out_specs.append(pl.BlockSpec((cols // chunk, slab, chunk), lambda *g: (0, index_map(*g)[0], 0)))
            shapes.append(jax.ShapeDtypeStruct((cols // chunk, rows, chunk), BF16))
    return in_specs, tuple(out_specs), tuple(shapes)


def _inproj_kernel(x_ref, g_ref, w_ref, wf_ref, bf_ref, *rest, time_minor, n_cast):
    h_ref, q_ref, kb_ref, k_ref, v_ref, vb_ref, u_ref, f_ref = _convert_slabs(rest, n_cast)
    h = (_rms(x_ref[...]) * g_ref[...]).astype(BF16)
    h_ref[...] = h
    a = q_ref.shape[1]
    f = _dot(h, wf_ref[...]) + bf_ref[...]
    k = _dot(h, w_ref[:, a:2 * a])
    kb_ref[...] = k.astype(BF16)
    v = _dot(h, w_ref[:, 2 * a:3 * a])
    vb_ref[...] = v.astype(BF16)
    q_ref[...] = (_dot(h, w_ref[:, 0:a]) * SCALE).astype(BF16)
    u_ref[...] = _dot(h, w_ref[:, 3 * a:])
    if time_minor:
        k_ref[0] = k.T
        v_ref[0] = v.T
        f_ref[0] = jax.nn.log_sigmoid(f.T[0:f_ref.shape[1], :])
    else:
        k_ref[...] = k
        v_ref[...] = v
        f_ref[...] = jax.nn.log_sigmoid(f[:, 0:f_ref.shape[1]])


def _inproj(x, g, w, wf, bf, *, tm, seq, time_minor, to_bf16=()):
    n, d = x.shape
    cast_in, cast_out, cast_shapes = _slab_specs(to_bf16, n // tm, lambda i: (i, 0))
    a = ATTN_W
    p = w.shape[1] - 3 * a
    nh = N_HEADS
    row = lambda width: pl.BlockSpec((tm, width), lambda i: (i, 0))
    if time_minor:
        steps = seq // tm
        feat = lambda rows: (jax.ShapeDtypeStruct((n // seq, rows, seq), F32),
                             pl.BlockSpec((1, rows, tm), lambda i: (i // steps, 0, i % steps)))
    else:
        feat = lambda rows: (jax.ShapeDtypeStruct((n, rows), F32), row(rows))
    (k_shape, k_spec), (f_shape, f_spec) = feat(a), feat(nh)
    out_shape = (
        jax.ShapeDtypeStruct((n, d), BF16),
        jax.ShapeDtypeStruct((n, a), BF16),
        jax.ShapeDtypeStruct((n, a), BF16),
        k_shape,
        k_shape,
        jax.ShapeDtypeStruct((n, a), BF16),
        jax.ShapeDtypeStruct((n, p), F32),
        f_shape,
    )
    est = (2 * _nbytes((tm, d), F32) + _nbytes(w.shape, BF16) + _nbytes(wf.shape, BF16)
           + 2 * (_nbytes((tm, d), BF16) + 3 * _nbytes((tm, a), BF16) + 2 * _nbytes((tm, a), F32)
                  + _nbytes((tm, p), F32) + _nbytes((tm, LANES), F32))
           + 4 * _nbytes((tm, a), F32) + sum(3 * _nbytes(c.shape, F32) * tm // n for c, _ in to_bf16))
    return pl.pallas_call(
        functools.partial(_inproj_kernel, time_minor=time_minor, n_cast=len(to_bf16)),
        grid=(n // tm,),
        in_specs=[row(d), _resident((1, d)), _resident(w.shape), _resident(wf.shape), _resident((1, LANES))]
                 + cast_in,
        out_specs=(row(d), row(a), row(a), k_spec, k_spec, row(a), row(p), f_spec) + cast_out,
        out_shape=out_shape + cast_shapes,
        compiler_params=pltpu.CompilerParams(
            dimension_semantics=("arbitrary",), vmem_limit_bytes=_vmem_limit(est)),
        name="inproj",
    )(x, g, w, wf, bf, *[c for c, _ in to_bf16])


def _cumsum_kernel(x_ref, o_ref):
    x = x_ref[...]
    length = x.shape[1]
    lane = lax.broadcasted_iota(jnp.int32, x.shape, 1)
    shift = 1
    while shift < length:
        x = x + jnp.where(lane >= shift, pltpu.roll(x, shift, axis=1), 0.0)
        shift *= 2
    o_ref[...] = x


def _cumsum_lanes(x):
    rows, length = x.shape
    tr = _tile(rows, CUMSUM_ROWS)
    spec = pl.BlockSpec((tr, length), lambda i: (i, 0))
    return pl.pallas_call(
        _cumsum_kernel,
        grid=(rows // tr,),
        in_specs=[spec],
        out_specs=spec,
        out_shape=jax.ShapeDtypeStruct(x.shape, F32),
        compiler_params=pltpu.CompilerParams(dimension_semantics=("arbitrary",)),
        name="cumsum",
    )(x)


VALUE_ROWS = HEAD_DIM + 16


def _attn_kernel(q_ref, k_ref, vt_ref, c_ref, o_ref,
                 vta_ref, ckrep_ref, s_ref, p_ref, alpha_ref, m_ref, acc_ref, *, tq):
    t = k_ref.shape[1]
    nq = t // tq
    for a in range(HEADS_PER_BLOCK):
        vta_ref[a, 0:HEAD_DIM, :] = vt_ref[0, a * HEAD_DIM:(a + 1) * HEAD_DIM, :].astype(BF16)
        vta_ref[a, HEAD_DIM:VALUE_ROWS, :] = jnp.ones((VALUE_ROWS - HEAD_DIM, t), BF16)
        ckrep_ref[a] = jnp.broadcast_to(c_ref[0, 0, a:a + 1, :], (LANES, t)).T

    pairs = [(i, j) for i in range(nq) for j in range(i + 1)]
    rows = lambda j: slice(j * tq, (j + 1) * tq)

    def scores(w):
        i, j = pairs[w]
        q2 = q_ref[0, rows(i), :]
        lane = lax.broadcasted_iota(jnp.int32, q2.shape, 1)
        kt = k_ref[0, rows(j), :]
        for a in range(HEADS_PER_BLOCK):
            qa = jnp.where((lane // HEAD_DIM) == a, q2, jnp.zeros_like(q2))
            ckr = ckrep_ref[a, rows(j), :]
            s_ref[w % 2, a] = _dot_nt(kt, qa) - jnp.concatenate([ckr] * (tq // LANES), axis=1)

    def probs(w):
        i, j = pairs[w]
        for a in range(HEADS_PER_BLOCK):
            s = s_ref[w % 2, a]
            if j == i:
                r = lax.broadcasted_iota(jnp.int32, s.shape, 0)
                c = lax.broadcasted_iota(jnp.int32, s.shape, 1)
                s = jnp.where(r <= c, s, -jnp.inf)
            cqa = c_ref[0, 0, a:a + 1, rows(i)]
            smax = jnp.max(s, axis=0, keepdims=True) + cqa
            if j == 0:
                m_new = smax
            else:
                m_prev = m_ref[a]
                m_new = jnp.maximum(m_prev, smax)
                alpha_ref[w % 2, a] = jnp.exp(m_prev - m_new)
            p_ref[w % 2, a] = jnp.exp(s - (m_new - cqa)).astype(BF16)
            m_ref[a] = m_new

    def values(w):
        i, j = pairs[w]
        for a in range(HEADS_PER_BLOCK):
            pv = _dot(vta_ref[a, :, rows(j)], p_ref[w % 2, a])
            acc_ref[a] = pv if j == 0 else alpha_ref[w % 2, a] * acc_ref[a] + pv
        if j == i:
            ot = jnp.concatenate(
                [acc_ref[a, 0:HEAD_DIM, :] / acc_ref[a, HEAD_DIM:HEAD_DIM + 1, :]
                 for a in range(HEADS_PER_BLOCK)], axis=0)
            o_ref[0, rows(i), :] = ot.T.astype(o_ref.dtype)

    n = len(pairs)
    scores(0)
    for w in range(n):
        if w + 1 < n:
            scores(w + 1)
        probs(w)
        if w >= 1:
            values(w - 1)
    values(n - 1)


def _attn_prompt(q, kb, vt, c_rows, *, tq):
    b, t, a = q.shape
    nblk = a // LANES
    spec = pl.BlockSpec((1, t, LANES), lambda bi, hi: (bi, 0, hi))
    return pl.pallas_call(
        functools.partial(_attn_kernel, tq=tq),
        grid=(b, nblk),
        in_specs=[spec, spec, pl.BlockSpec((1, LANES, t), lambda bi, hi: (bi, hi, 0)),
                  pl.BlockSpec((1, 1, HEADS_PER_BLOCK, t), lambda bi, hi: (bi, hi, 0, 0))],
        out_specs=spec,
        out_shape=jax.ShapeDtypeStruct((b, t, a), BF16),
        scratch_shapes=[pltpu.VMEM((HEADS_PER_BLOCK, VALUE_ROWS, t), BF16),
                        pltpu.VMEM((HEADS_PER_BLOCK, t, LANES), F32),
                        pltpu.VMEM((2, HEADS_PER_BLOCK, tq, tq), F32),
                        pltpu.VMEM((2, HEADS_PER_BLOCK, tq, tq), BF16),
                        pltpu.VMEM((2, HEADS_PER_BLOCK, 1, tq), F32),
                        pltpu.VMEM((HEADS_PER_BLOCK, 1, tq), F32),
                        pltpu.VMEM((HEADS_PER_BLOCK, VALUE_ROWS, tq), F32)],
        compiler_params=pltpu.CompilerParams(dimension_semantics=("arbitrary", "arbitrary")),
        name="attn_prompt",
    )(q, kb, vt, c_rows)


def _attn_sample_kernel(q_ref, kct_ref, vct_ref, kn_ref, vn_ref, cc_ref, cnr_ref, cnc_ref, o_ref,
                        kb_ref, vb_ref, s_ref, p_ref, pn_ref):
    n, a = q_ref.shape[1], q_ref.shape[2]
    nh = a // HEAD_DIM
    q = q_ref[0]
    qt = jnp.concatenate([q] * nh, axis=0)
    row_h = lax.broadcasted_iota(jnp.int32, qt.shape, 0) // n
    col_h = lax.broadcasted_iota(jnp.int32, qt.shape, 1) // HEAD_DIM
    qbd = jnp.where(row_h == col_h, qt, jnp.zeros_like(qt))
    kb_ref[...] = kct_ref[0].astype(BF16)
    vb_ref[...] = vct_ref[0].astype(BF16)
    s_ref[...] = _dot(qbd, kb_ref[...])
    s_new = _dot_nt(qbd, kn_ref[0])
    cnc = cnc_ref[0]
    r = lax.broadcasted_iota(jnp.int32, (n, n), 0)
    c = lax.broadcasted_iota(jnp.int32, (n, n), 1)
    inv_l = []
    for h in range(nh):
        rows = slice(h * n, (h + 1) * n)
        cq = cnc[:, h:h + 1]
        sc = s_ref[rows, :] + (cq - cc_ref[0, h:h + 1, :])
        sn = s_new[rows, :] + (cq - cnr_ref[0, h:h + 1, :])
        sn = jnp.where(c <= r, sn, -jnp.inf)
        m = jnp.maximum(jnp.max(sc, axis=1, keepdims=True), jnp.max(sn, axis=1, keepdims=True))
        pc = jnp.exp(sc - m)
        pn = jnp.exp(sn - m)
        inv_l.append(1.0 / (jnp.sum(pc, axis=1, keepdims=True) + jnp.sum(pn, axis=1, keepdims=True)))
        p_ref[rows, :] = pc.astype(BF16)
        pn_ref[rows, :] = pn.astype(BF16)
    o = _dot_nt(p_ref[...], vb_ref[...]) + _dot(pn_ref[...], vn_ref[0])
    for h in range(nh):
        rows = slice(h * n, (h + 1) * n)
        cols = slice(h * HEAD_DIM, (h + 1) * HEAD_DIM)
        o_ref[0, :, cols] = (o[rows, cols] * inv_l[h]).astype(o_ref.dtype)


def _attn_sample(q, kct, vct, kn, vn, cc, cnr, cnc):
    b, n, a = q.shape
    past = kct.shape[2]
    nh = a // HEAD_DIM
    per_b = lambda *tail: pl.BlockSpec((1,) + tail, lambda bi: (bi,) + (0,) * len(tail))
    est = (4 * _nbytes((a, past), F32) + 2 * _nbytes((a, past), BF16)
           + _nbytes((nh * n, past), F32) + _nbytes((nh * n, past), BF16) + 4 * _nbytes((nh * n, a), F32))
    return pl.pallas_call(
        _attn_sample_kernel,
        grid=(b,),
        in_specs=[per_b(n, a), per_b(a, past), per_b(a, past), per_b(n, a), per_b(n, a),
                  per_b(nh, past), per_b(nh, n), per_b(n, nh)],
        out_specs=per_b(n, a),
        out_shape=jax.ShapeDtypeStruct((b, n, a), BF16),
        scratch_shapes=[pltpu.VMEM((a, past), BF16), pltpu.VMEM((a, past), BF16),
                        pltpu.VMEM((nh * n, past), F32), pltpu.VMEM((nh * n, past), BF16),
                        pltpu.VMEM((nh * n, n), BF16)],
        compiler_params=pltpu.CompilerParams(
            dimension_semantics=("arbitrary",), vmem_limit_bytes=_vmem_limit(est)),
        name="attn_sample",
    )(q, kct, vct, kn, vn, cc, cnr, cnc)


def _merge_kernel(a_ref, h_ref, u_ref, hist_ref, x_ref, wup_ref, wga_ref, wgb_ref, wpool_ref, ps_ref,
                  wout_ref, gpost_ref, gpre_ref, *rest, pos0, zero_first, n_cast):
    ext_ref, m_ref = rest[len(rest) - 2:]
    x1_ref, h2_ref = _convert_slabs(rest[:len(rest) - 2], n_cast)
    nseg, tl, _ = u_ref.shape
    i = pl.program_id(1)
    hist = hist_ref[...]
    if zero_first:
        hist = jnp.where(i == 0, 0.0, hist)
    ext_ref[:, 0:HIST_ROWS, :] = hist
    ext_ref[:, HIST_ROWS:HIST_ROWS + tl, :] = u_ref[...]
    gw = u_ref.shape[2] // POOL_GROUPS
    ogw = wpool_ref.shape[2]
    pos = pos0 + i * tl + lax.broadcasted_iota(jnp.int32, (nseg, tl, gw), 1)
    a = a_ref[...]
    h = h_ref[...]
    for g, w in enumerate(POOL_WINDOWS):
        os_ = slice(g * ogw, (g + 1) * ogw)
        br_a = _dot(a, wup_ref[:, os_])
        ga = _dot(h, wga_ref[:, os_])
        gb = _dot(h, wgb_ref[:, os_])
        cs = slice(g * gw, (g + 1) * gw)
        cur = ext_ref[:, HIST_ROWS:HIST_ROWS + tl, cs]
        tot = cur
        for s in range(1, w):
            tot = tot + ext_ref[:, HIST_ROWS - s:HIST_ROWS - s + tl, cs]
        cnt = jnp.minimum(pos + 1, w).astype(F32)
        pooled = (tot / cnt - cur).reshape(nseg * tl, gw).astype(BF16)
        br_b = _dot(pooled, wpool_ref[g]) * ps_ref[:, os_]
        m_ref[:, os_] = (jax.nn.sigmoid(ga) * br_a + jax.nn.sigmoid(gb) * br_b).astype(BF16)
    x1 = x_ref[...] + _rms(_dot(m_ref[...], wout_ref[...])) * gpost_ref[...]
    x1_ref[...] = x1
    h2_ref[...] = (_rms(x1) * gpre_ref[...]).astype(BF16)


def _merge(a, h, u3, hist, hist_map, x, wup, wga, wgb, wpool, ps, wout, gpost, gpre,
           *, nseg, tl, pos0, zero_first, to_bf16=()):
    s_total, l_total, p = u3.shape
    n, d = x.shape
    tm = nseg * tl
    steps = l_total // tl
    assert nseg == 1 or steps == 1
    row = lambda width: pl.BlockSpec((tm, width), lambda s, i: (s * steps + i, 0))
    n_steps = (s_total // nseg) * steps
    cast_in, cast_out, cast_shapes = _slab_specs(to_bf16, n_steps, lambda s, i: (s * steps + i, 0))
    est = (_nbytes(wup.shape, BF16) + 2 * _nbytes(wga.shape, BF16) + _nbytes(wpool.shape, BF16)
           + _nbytes(wout.shape, BF16)
           + 2 * (_nbytes((tm, a.shape[1]), BF16) + 2 * _nbytes((tm, d), BF16) + 2 * _nbytes((tm, d), F32)
                  + _nbytes((tm + HIST_ROWS, p), F32))
           + _nbytes((tm + nseg * HIST_ROWS, p), F32) + _nbytes((tm, d), BF16) + 6 * _nbytes((tm, d), F32)
           + sum(3 * _nbytes(w.shape, F32) // n_steps for w, _ in to_bf16))
    return pl.pallas_call(
        functools.partial(_merge_kernel, pos0=pos0, zero_first=zero_first, n_cast=len(to_bf16)),
        grid=(s_total // nseg, steps),
        in_specs=[row(a.shape[1]), row(d),
                  pl.BlockSpec((nseg, tl, p), lambda s, i: (s, i, 0)),
                  pl.BlockSpec((nseg, HIST_ROWS, p), hist_map),
                  row(d),
                  _resident(wup.shape), _resident(wga.shape), _resident(wgb.shape), _resident(wpool.shape),
                  _resident((1, d)), _resident(wout.shape), _resident((1, d)), _resident((1, d))]
                 + cast_in,
        out_specs=(row(d), row(d)) + cast_out,
        out_shape=(jax.ShapeDtypeStruct((n, d), F32), jax.ShapeDtypeStruct((n, d), BF16)) + cast_shapes,
        scratch_shapes=[pltpu.VMEM((nseg, HIST_ROWS + tl, p), F32), pltpu.VMEM((tm, d), BF16)],
        compiler_params=pltpu.CompilerParams(
            dimension_semantics=("arbitrary", "arbitrary"), vmem_limit_bytes=_vmem_limit(est)),
        name="merge",
    )(a, h, u3, hist, x, wup, wga, wgb, wpool, ps, wout, gpost, gpre, *[w for w, _ in to_bf16])


def _ffn_kernel(h_ref, x1_ref, w1_ref, w2_ref, g_ref, o_ref):
    f = pl.program_id(1)

    @pl.when(f == 0)
    def _():
        o_ref[...] = jnp.zeros_like(o_ref)

    z = jnp.square(jnp.maximum(_dot(h_ref[...], w1_ref[0]), 0.0)).astype(BF16)
    o_ref[...] += _dot(z, w2_ref[...])

    @pl.when(f == pl.num_programs(1) - 1)
    def _():
        o_ref[...] = x1_ref[...] + _rms(o_ref[...]) * g_ref[...]


def _ffn(h2, x1, w1, w2, g, *, tm):
    n, d = x1.shape
    tf = w1.shape[2]
    dff = w2.shape[0]
    est = (2 * _nbytes((tm, d), BF16) + 4 * _nbytes((tm, d), F32)
           + 4 * _nbytes((d, tf), BF16) + 2 * _nbytes((tm, tf), F32) + _nbytes((tm, d), F32))
    return pl.pallas_call(
        _ffn_kernel,
        grid=(n // tm, dff // tf),
        in_specs=[pl.BlockSpec((tm, d), lambda i, f: (i, 0)),
                  pl.BlockSpec((tm, d), lambda i, f: (i, 0)),
                  pl.BlockSpec((1, d, tf), lambda i, f: (f, 0, 0)),
                  pl.BlockSpec((tf, d), lambda i, f: (f, 0)),
                  pl.BlockSpec((1, d), lambda i, f: (0, 0))],
        out_specs=pl.BlockSpec((tm, d), lambda i, f: (i, 0)),
        out_shape=jax.ShapeDtypeStruct((n, d), F32),
        compiler_params=pltpu.CompilerParams(
            dimension_semantics=("arbitrary", "arbitrary"), vmem_limit_bytes=_vmem_limit(est)),
        name="ffn",
    )(h2, x1, w1, w2, g)


def _tile(n, pref):
    t = min(n, pref)
    while n % t:
        t //= 2
    return t


INPROJ_TM = 256
ATTN_TQ = 512
MERGE_TM = 256
FFN_TM = 1024
FFN_TF = 512
CUMSUM_ROWS = 32
REPACK_ROWS = 256


def _cum_logf_rows(logf_bht):
    b, h, t = logf_bht.shape
    rows = logf_bht.reshape(b * h, t)
    pad = (-t) % LANES
    if pad:
        rows = jnp.pad(rows, ((0, 0), (0, pad)))
    return _cumsum_lanes(rows)


def _time_minor(x, nbatch):
    return jnp.transpose(x, (0, 2, 3, 1)).reshape(nbatch, -1, x.shape[1])


def _layer(xp, xs, cache_k, cache_v, cache_logf, state_pool, g_mix_pre, w_in, b_f, w_attn_up, w_pool,
           pool_scale, w_out, g_mix_post, g_ffn_pre, w_ff1, w_ff2, g_ffn_post):
    bp, tp, d = xp.shape
    bs, ts, _ = xs.shape
    past = cache_k.shape[1]
    a, nh, p = ATTN_W, N_HEADS, w_pool.shape[0] * w_pool.shape[1]
    off_f = 3 * a
    off_u = off_f + nh
    off_ga = off_u + p
    off_gb = off_ga + d

    w_qkvu, w_f, w_ga, w_gb = _repack_w_in(jnp.transpose(w_in.astype(F32)), off_f=off_f, off_u=off_u,
                                           off_ga=off_ga, off_gb=off_gb, tr=_tile(w_in.shape[0], REPACK_ROWS))
    b_f2 = jnp.pad(b_f.reshape(1, -1), ((0, 0), (0, LANES - nh)))
    row = lambda v: v.reshape(1, -1)
    g_pre, ps = row(g_mix_pre), row(pool_scale)
    g_post, g_fpre, g_fpost = row(g_mix_post), row(g_ffn_pre), row(g_ffn_post)

    def project(x3, time_minor, to_bf16=()):
        x2 = x3.reshape(-1, d)
        tm = _tile(x3.shape[1] if time_minor else x2.shape[0], INPROJ_TM)
        return x2, _inproj(x2, g_pre, w_qkvu, w_f, b_f2, tm=tm, seq=x3.shape[1], time_minor=time_minor,
                           to_bf16=to_bf16)

    def finish(x2, h, a_out, u3, hist, hist_map, nseg, tl, pos0, zero_first, ffn_w):
        tf = _tile(w_ff1.shape[1], FFN_TF)
        to_bf16 = () if ffn_w else ((w_ff1.astype(F32), tf), (w_ff2.astype(F32), None))
        x1, h2, *cast = _merge(a_out, h, u3, hist, hist_map, x2, w_up, w_ga, w_gb, w_pl, ps, w_o, g_post, g_fpre,
                               nseg=nseg, tl=tl, pos0=pos0, zero_first=zero_first, to_bf16=to_bf16)
        w1, w2 = ffn_w or cast
        return _ffn(h2, x1, w1, w2, g_fpost, tm=_tile(x2.shape[0], FFN_TM)), (w1, w2)

    merge_w = ((w_attn_up.astype(F32), None), (w_pool.astype(F32).reshape(p, -1), None), (w_out.astype(F32), None))
    x2, (h, q, kb, kt, vt, _, u, logft, w_up, w_pl, w_o) = project(xp, True, merge_w)
    w_pl = w_pl.reshape(w_pool.shape)
    c_rows = _cum_logf_rows(logft)
    nblk = nh // HEADS_PER_BLOCK
    three = lambda z: z.reshape(bp, tp, a)
    a_out = _attn_prompt(three(q), three(kb), vt, c_rows.reshape(bp, nblk, HEADS_PER_BLOCK, tp),
                         tq=_tile(tp, ATTN_TQ))
    to_bthd = lambda zt: jnp.transpose(zt.reshape(bp, nh, HEAD_DIM, tp), (0, 3, 1, 2))
    u3 = u.reshape(bp, tp, p)
    tl = _tile(tp, MERGE_TM)
    blocks_per_tile = tl // HIST_ROWS
    hist_map = lambda s, i: (s, jnp.maximum(i * blocks_per_tile - 1, 0), 0)
    yp, ffn_w = finish(x2, h, a_out.reshape(bp * tp, a), u3, u3, hist_map, 1, tl, 0, True, None)
    yp = yp.reshape(bp, tp, d)
    prompt_out = (yp, to_bthd(kt), to_bthd(vt), jnp.transpose(logft, (0, 2, 1)),
                  jnp.concatenate([jnp.zeros((bp, POOL_HIST, p), F32), u3], axis=1)[:, -POOL_HIST:])

    x2, (h, q, kb, k, v, vb, u, logf) = project(xs, False)
    logf3 = logf.reshape(bs, ts, nh)
    f_all = jnp.concatenate([cache_logf.astype(F32), logf3], axis=1)
    c_all = _cum_logf_rows(jnp.transpose(f_all, (0, 2, 1))).reshape(bs, nh, -1)
    c_cache = c_all[:, :, :past]
    c_new = c_all[:, :, past:past + ts]
    three = lambda z: z.reshape(bs, ts, a)
    a_out = _attn_sample(three(q), _time_minor(cache_k.astype(F32), bs), _time_minor(cache_v.astype(F32), bs),
                         three(kb), three(vb), c_cache, c_new, jnp.transpose(c_new, (0, 2, 1)))
    u3 = u.reshape(bs, ts, p)
    hist = jnp.pad(state_pool.astype(F32), ((0, 0), (HIST_ROWS - POOL_HIST, 0), (0, 0)))
    ys, _ = finish(x2, h, a_out.reshape(bs * ts, a), u3, hist, lambda s, i: (s, 0, 0), bs, ts, past, False, ffn_w)
    ys = ys.reshape(bs, ts, d)
    sample_out = (ys, k.reshape(bs, ts, nh, HEAD_DIM), v.reshape(bs, ts, nh, HEAD_DIM), logf3,
                  jnp.concatenate([state_pool.astype(F32), u3], axis=1)[:, -POOL_HIST:])
    return prompt_out, sample_out


def kernel(x_prompt, x_sample, cache_k, cache_v, cache_logf, state_pool, g_mix_pre, w_in, b_f, w_attn_up,
           w_pool, pool_scale, w_out, g_mix_post, g_ffn_pre, w_ff1, w_ff2, g_ffn_post):
    depth = w_in.shape[0]
    xp, xs = x_prompt, x_sample
    per_layer = []
    for l in range(depth):
        po, so = _layer(xp, xs, cache_k[l], cache_v[l], cache_logf[l], state_pool[l], g_mix_pre[l], w_in[l],
                        b_f[l], w_attn_up[l], w_pool[l], pool_scale[l], w_out[l], g_mix_post[l],
                        g_ffn_pre[l], w_ff1[l], w_ff2[l], g_ffn_post[l])
        xp, xs = po[0], so[0]
        per_layer.append(po[1:] + so[1:])
    stacked = [jnp.stack(leaves, 0) for leaves in zip(*per_layer)]
    return (xp, xs, *stacked)
```

```python
import functools
import math

import jax
import jax.numpy as jnp
from jax import lax
from jax.experimental import pallas as pl
from jax.experimental.pallas import tpu as pltpu

N_HEADS = 16
HEAD_DIM = 64
ATTN_W = N_HEADS * HEAD_DIM
POOL_WINDOWS = (2, 4, 8, 16)
POOL_GROUPS = len(POOL_WINDOWS)
POOL_HIST = max(POOL_WINDOWS) - 1
EPS = 1e-6
SCALE = HEAD_DIM ** -0.5

LANES = 128
VMEM_LIMIT_CAP = 60000 * 1024

HEADS_PER_BLOCK = LANES // HEAD_DIM
HIST_ROWS = 16

F32 = jnp.float32
BF16 = jnp.bfloat16


def _vmem_limit(nbytes):
    return int(min(VMEM_LIMIT_CAP, max(32 * 1024 * 1024, nbytes * 5 // 4)))


def _nbytes(shape, dtype):
    return math.prod(shape) * jnp.dtype(dtype).itemsize


def _rms(x):
    return x * lax.rsqrt(jnp.mean(x * x, axis=-1, keepdims=True) + EPS)


def _dot(a, b):
    return jnp.dot(a, b, preferred_element_type=F32)


def _dot_nt(a, b):
    return lax.dot_general(a, b, (((1,), (1,)), ((), ())), preferred_element_type=F32)


def _resident(shape):
    zeros = (0,) * len(shape)
    return pl.BlockSpec(shape, lambda *_: zeros, pipeline_mode=pl.Buffered(1))


def _repack_kernel(wt_ref, qkvu_ref, f_ref, ga_ref, gb_ref, *, off_f, off_u, off_ga, off_gb):
    nh = off_u - off_f
    p = off_ga - off_u
    d = off_gb - off_ga
    piece = lambda lo, hi: wt_ref[lo:hi, :].T.astype(BF16)
    qkvu_ref[:, 0:off_f] = piece(0, off_f)
    qkvu_ref[:, off_f:off_f + p] = piece(off_u, off_ga)
    fwin = wt_ref[off_f:off_f + LANES, :].T
    lane = lax.broadcasted_iota(jnp.int32, fwin.shape, 1)
    f_ref[...] = jnp.where(lane < nh, fwin, 0.0).astype(BF16)
    ga_ref[...] = piece(off_ga, off_gb)
    gb_ref[...] = piece(off_gb, off_gb + d)


def _repack_w_in(wt, *, off_f, off_u, off_ga, off_gb, tr):
    width, rows = wt.shape
    p = off_ga - off_u
    d = off_gb - off_ga
    blk = lambda cols: pl.BlockSpec((tr, cols), lambda i: (i, 0))
    est = 4 * _nbytes((width, tr), F32) + 2 * _nbytes((tr, off_f + p + LANES + 2 * d), BF16)
    return pl.pallas_call(
        functools.partial(_repack_kernel, off_f=off_f, off_u=off_u, off_ga=off_ga, off_gb=off_gb),
        grid=(rows // tr,),
        in_specs=[pl.BlockSpec((width, tr), lambda i: (0, i))],
        out_specs=(blk(off_f + p), blk(LANES), blk(d), blk(d)),
        out_shape=(jax.ShapeDtypeStruct((rows, off_f + p), BF16), jax.ShapeDtypeStruct((rows, LANES), BF16),
                   jax.ShapeDtypeStruct((rows, d), BF16), jax.ShapeDtypeStruct((rows, d), BF16)),
        compiler_params=pltpu.CompilerParams(
            dimension_semantics=("arbitrary",), vmem_limit_bytes=_vmem_limit(est)),
        name="repack_w_in",
    )(wt)


def _convert_slabs(refs, n_cast):
    for src, dst in zip(refs[:n_cast], refs[len(refs) - n_cast:]):
        if len(dst.shape) == 2:
            dst[...] = src[...].astype(dst.dtype)
        else:
            for c in range(dst.shape[0]):
                dst[c] = src[:, c * dst.shape[2]:(c + 1) * dst.shape[2]].astype(dst.dtype)
    return refs[n_cast:len(refs) - n_cast]


def _slab_specs(to_bf16, n_steps, index_map):
    in_specs, out_specs, shapes = [], [], []
    for w, chunk in to_bf16:
        rows, cols = w.shape
        slab = rows // n_steps
        in_specs.append(pl.BlockSpec((slab, cols), index_map))
        if chunk is None:
            out_specs.append(pl.BlockSpec((slab, cols), index_map))
            shapes.append(jax.ShapeDtypeStruct((rows, cols), BF16))
        else:
            out_specs.append(pl.BlockSpec((cols // chunk, slab, chunk), lambda *g: (0, index_map(*g)[0], 0)))
            shapes.append(jax.ShapeDtypeStruct((cols // chunk, rows, chunk), BF16))
    return in_specs, tuple(out_specs), tuple(shapes)


def _inproj_kernel(x_ref, g_ref, w_ref, wf_ref, bf_ref, *rest, time_minor, n_cast):
    h_ref, q_ref, kb_ref, k_ref, v_ref, vb_ref, u_ref, f_ref = _convert_slabs(rest, n_cast)
    h = (_rms(x_ref[...]) * g_ref[...]).astype(BF16)
    h_ref[...] = h
    a = q_ref.shape[1]
    f = _dot(h, wf_ref[...]) + bf_ref[...]
    k = _dot(h, w_ref[:, a:2 * a])
    kb_ref[...] = k.astype(BF16)
    v = _dot(h, w_ref[:, 2 * a:3 * a])
    vb_ref[...] = v.astype(BF16)
    q_ref[...] = (_dot(h, w_ref[:, 0:a]) * SCALE).astype(BF16)
    u_ref[...] = _dot(h, w_ref[:, 3 * a:])
    if time_minor:
        k_ref[0] = k.T
        v_ref[0] = v.T
        f_ref[0] = jax.nn.log_sigmoid(f.T[0:f_ref.shape[1], :])
    else:
        k_ref[...] = k
        v_ref[...] = v
        f_ref[...] = jax.nn.log_sigmoid(f[:, 0:f_ref.shape[1]])


def _inproj(x, g, w, wf, bf, *, tm, seq, time_minor, to_bf16=()):
    n, d = x.shape
    cast_in, cast_out, cast_shapes = _slab_specs(to_bf16, n // tm, lambda i: (i, 0))
    a = ATTN_W
    p = w.shape[1] - 3 * a
    nh = N_HEADS
    row = lambda width: pl.BlockSpec((tm, width), lambda i: (i, 0))
    if time_minor:
        steps = seq // tm
        feat = lambda rows: (jax.ShapeDtypeStruct((n // seq, rows, seq), F32),
                             pl.BlockSpec((1, rows, tm), lambda i: (i // steps, 0, i % steps)))
    else:
        feat = lambda rows: (jax.ShapeDtypeStruct((n, rows), F32), row(rows))
    (k_shape, k_spec), (f_shape, f_spec) = feat(a), feat(nh)
    out_shape = (
        jax.ShapeDtypeStruct((n, d), BF16),
        jax.ShapeDtypeStruct((n, a), BF16),
        jax.ShapeDtypeStruct((n, a), BF16),
        k_shape,
        k_shape,
        jax.ShapeDtypeStruct((n, a), BF16),
        jax.ShapeDtypeStruct((n, p), F32),
        f_shape,
    )
    est = (2 * _nbytes((tm, d), F32) + _nbytes(w.shape, BF16) + _nbytes(wf.shape, BF16)
           + 2 * (_nbytes((tm, d), BF16) + 3 * _nbytes((tm, a), BF16) + 2 * _nbytes((tm, a), F32)
                  + _nbytes((tm, p), F32) + _nbytes((tm, LANES), F32))
           + 4 * _nbytes((tm, a), F32) + sum(3 * _nbytes(c.shape, F32) * tm // n for c, _ in to_bf16))
    return pl.pallas_call(
        functools.partial(_inproj_kernel, time_minor=time_minor, n_cast=len(to_bf16)),
        grid=(n // tm,),
        in_specs=[row(d), _resident((1, d)), _resident(w.shape), _resident(wf.shape), _resident((1, LANES))]
                 + cast_in,
        out_specs=(row(d), row(a), row(a), k_spec, k_spec, row(a), row(p), f_spec) + cast_out,
        out_shape=out_shape + cast_shapes,
        compiler_params=pltpu.CompilerParams(
            dimension_semantics=("arbitrary",), vmem_limit_bytes=_vmem_limit(est)),
        name="inproj",
    )(x, g, w, wf, bf, *[c for c, _ in to_bf16])


def _cumsum_kernel(x_ref, o_ref):
    x = x_ref[...]
    length = x.shape[1]
    lane = lax.broadcasted_iota(jnp.int32, x.shape, 1)
    shift = 1
    while shift < length:
        x = x + jnp.where(lane >= shift, pltpu.roll(x, shift, axis=1), 0.0)
        shift *= 2
    o_ref[...] = x


def _cumsum_lanes(x):
    rows, length = x.shape
    tr = _tile(rows, CUMSUM_ROWS)
    spec = pl.BlockSpec((tr, length), lambda i: (i, 0))
    return pl.pallas_call(
        _cumsum_kernel,
        grid=(rows // tr,),
        in_specs=[spec],
        out_specs=spec,
        out_shape=jax.ShapeDtypeStruct(x.shape, F32),
        compiler_params=pltpu.CompilerParams(dimension_semantics=("arbitrary",)),
        name="cumsum",
    )(x)


VALUE_ROWS = HEAD_DIM + 16


def _attn_kernel(q_ref, k_ref, vt_ref, c_ref, o_ref,
                 vta_ref, ckrep_ref, s_ref, p_ref, alpha_ref, m_ref, acc_ref, *, tq):
    t = k_ref.shape[1]
    nq = t // tq
    for a in range(HEADS_PER_BLOCK):
        vta_ref[a, 0:HEAD_DIM, :] = vt_ref[0, a * HEAD_DIM:(a + 1) * HEAD_DIM, :].astype(BF16)
        vta_ref[a, HEAD_DIM:VALUE_ROWS, :] = jnp.ones((VALUE_ROWS - HEAD_DIM, t), BF16)
        ckrep_ref[a] = jnp.broadcast_to(c_ref[0, 0, a:a + 1, :], (LANES, t)).T

    pairs = [(i, j) for i in range(nq) for j in range(i + 1)]
    rows = lambda j: slice(j * tq, (j + 1) * tq)

    def scores(w):
        i, j = pairs[w]
        q2 = q_ref[0, rows(i), :]
        lane = lax.broadcasted_iota(jnp.int32, q2.shape, 1)
        kt = k_ref[0, rows(j), :]
        for a in range(HEADS_PER_BLOCK):
            qa = jnp.where((lane // HEAD_DIM) == a, q2, jnp.zeros_like(q2))
            ckr = ckrep_ref[a, rows(j), :]
            s_ref[w % 2, a] = _dot_nt(kt, qa) - jnp.concatenate([ckr] * (tq // LANES), axis=1)

    def probs(w):
        i, j = pairs[w]
        for a in range(HEADS_PER_BLOCK):
            s = s_ref[w % 2, a]
            if j == i:
                r = lax.broadcasted_iota(jnp.int32, s.shape, 0)
                c = lax.broadcasted_iota(jnp.int32, s.shape, 1)
                s = jnp.where(r <= c, s, -jnp.inf)
            cqa = c_ref[0, 0, a:a + 1, rows(i)]
            smax = jnp.max(s, axis=0, keepdims=True) + cqa
            if j == 0:
                m_new = smax
            else:
                m_prev = m_ref[a]
                m_new = jnp.maximum(m_prev, smax)
                alpha_ref[w % 2, a] = jnp.exp(m_prev - m_new)
            p_ref[w % 2, a] = jnp.exp(s - (m_new - cqa)).astype(BF16)
            m_ref[a] = m_new

    def values(w):
        i, j = pairs[w]
        for a in range(HEADS_PER_BLOCK):
            pv = _dot(vta_ref[a, :, rows(j)], p_ref[w % 2, a])
            acc_ref[a] = pv if j == 0 else alpha_ref[w % 2, a] * acc_ref[a] + pv
        if j == i:
            ot = jnp.concatenate(
                [acc_ref[a, 0:HEAD_DIM, :] / acc_ref[a, HEAD_DIM:HEAD_DIM + 1, :]
                 for a in range(HEADS_PER_BLOCK)], axis=0)
            o_ref[0, rows(i), :] = ot.T.astype(o_ref.dtype)

    n = len(pairs)
    scores(0)
    for w in range(n):
        if w + 1 < n:
            scores(w + 1)
        probs(w)
        if w >= 1:
            values(w - 1)
    values(n - 1)


def _attn_prompt(q, kb, vt, c_rows, *, tq):
    b, t, a = q.shape
    nblk = a // LANES
    spec = pl.BlockSpec((1, t, LANES), lambda bi, hi: (bi, 0, hi))
    return pl.pallas_call(
        functools.partial(_attn_kernel, tq=tq),
        grid=(b, nblk),
        in_specs=[spec, spec, pl.BlockSpec((1, LANES, t), lambda bi, hi: (bi, hi, 0)),
                  pl.BlockSpec((1, 1, HEADS_PER_BLOCK, t), lambda bi, hi: (bi, hi, 0, 0))],
        out_specs=spec,
        out_shape=jax.ShapeDtypeStruct((b, t, a), BF16),
        scratch_shapes=[pltpu.VMEM((HEADS_PER_BLOCK, VALUE_ROWS, t), BF16),
                        pltpu.VMEM((HEADS_PER_BLOCK, t, LANES), F32),
                        pltpu.VMEM((2, HEADS_PER_BLOCK, tq, tq), F32),
                        pltpu.VMEM((2, HEADS_PER_BLOCK, tq, tq), BF16),
                        pltpu.VMEM((2, HEADS_PER_BLOCK, 1, tq), F32),
                        pltpu.VMEM((HEADS_PER_BLOCK, 1, tq), F32),
                        pltpu.VMEM((HEADS_PER_BLOCK, VALUE_ROWS, tq), F32)],
        compiler_params=pltpu.CompilerParams(dimension_semantics=("arbitrary", "arbitrary")),
        name="attn_prompt",
    )(q, kb, vt, c_rows)


def _attn_sample_kernel(q_ref, kct_ref, vct_ref, kn_ref, vn_ref, cc_ref, cnr_ref, cnc_ref, o_ref,
                        kb_ref, vb_ref, s_ref, p_ref, pn_ref):
    n, a = q_ref.shape[1], q_ref.shape[2]
    nh = a // HEAD_DIM
    q = q_ref[0]
    qt = jnp.concatenate([q] * nh, axis=0)
    row_h = lax.broadcasted_iota(jnp.int32, qt.shape, 0) // n
    col_h = lax.broadcasted_iota(jnp.int32, qt.shape, 1) // HEAD_DIM
    qbd = jnp.where(row_h == col_h, qt, jnp.zeros_like(qt))
    kb_ref[...] = kct_ref[0].astype(BF16)
    vb_ref[...] = vct_ref[0].astype(BF16)
    s_ref[...] = _dot(qbd, kb_ref[...])
    s_new = _dot_nt(qbd, kn_ref[0])
    cnc = cnc_ref[0]
    r = lax.broadcasted_iota(jnp.int32, (n, n), 0)
    c = lax.broadcasted_iota(jnp.int32, (n, n), 1)
    inv_l = []
    for h in range(nh):
        rows = slice(h * n, (h + 1) * n)
        cq = cnc[:, h:h + 1]
        sc = s_ref[rows, :] + (cq - cc_ref[0, h:h + 1, :])
        sn = s_new[rows, :] + (cq - cnr_ref[0, h:h + 1, :])
        sn = jnp.where(c <= r, sn, -jnp.inf)
        m = jnp.maximum(jnp.max(sc, axis=1, keepdims=True), jnp.max(sn, axis=1, keepdims=True))
        pc = jnp.exp(sc - m)
        pn = jnp.exp(sn - m)
        inv_l.append(1.0 / (jnp.sum(pc, axis=1, keepdims=True) + jnp.sum(pn, axis=1, keepdims=True)))
        p_ref[rows, :] = pc.astype(BF16)
        pn_ref[rows, :] = pn.astype(BF16)
    o = _dot_nt(p_ref[...], vb_ref[...]) + _dot(pn_ref[...], vn_ref[0])
    for h in range(nh):
        rows = slice(h * n, (h + 1) * n)
        cols = slice(h * HEAD_DIM, (h + 1) * HEAD_DIM)
        o_ref[0, :, cols] = (o[rows, cols] * inv_l[h]).astype(o_ref.dtype)


def _attn_sample(q, kct, vct, kn, vn, cc, cnr, cnc):
    b, n, a = q.shape
    past = kct.shape[2]
    nh = a // HEAD_DIM
    per_b = lambda *tail: pl.BlockSpec((1,) + tail, lambda bi: (bi,) + (0,) * len(tail))
    est = (4 * _nbytes((a, past), F32) + 2 * _nbytes((a, past), BF16)
           + _nbytes((nh * n, past), F32) + _nbytes((nh * n, past), BF16) + 4 * _nbytes((nh * n, a), F32))
    return pl.pallas_call(
        _attn_sample_kernel,
        grid=(b,),
        in_specs=[per_b(n, a), per_b(a, past), per_b(a, past), per_b(n, a), per_b(n, a),
                  per_b(nh, past), per_b(nh, n), per_b(n, nh)],
        out_specs=per_b(n, a),
        out_shape=jax.ShapeDtypeStruct((b, n, a), BF16),
        scratch_shapes=[pltpu.VMEM((a, past), BF16), pltpu.VMEM((a, past), BF16),
                        pltpu.VMEM((nh * n, past), F32), pltpu.VMEM((nh * n, past), BF16),
                        pltpu.VMEM((nh * n, n), BF16)],
        compiler_params=pltpu.CompilerParams(
            dimension_semantics=("arbitrary",), vmem_limit_bytes=_vmem_limit(est)),
        name="attn_sample",
    )(q, kct, vct, kn, vn, cc, cnr, cnc)


def _merge_kernel(a_ref, h_ref, u_ref, hist_ref, x_ref, wup_ref, wga_ref, wgb_ref, wpool_ref, ps_ref,
                  wout_ref, gpost_ref, gpre_ref, *rest, pos0, zero_first, n_cast):
    ext_ref, m_ref = rest[len(rest) - 2:]
    x1_ref, h2_ref = _convert_slabs(rest[:len(rest) - 2], n_cast)
    nseg, tl, _ = u_ref.shape
    i = pl.program_id(1)
    hist = hist_ref[...]
    if zero_first:
        hist = jnp.where(i == 0, 0.0, hist)
    ext_ref[:, 0:HIST_ROWS, :] = hist
    ext_ref[:, HIST_ROWS:HIST_ROWS + tl, :] = u_ref[...]
    gw = u_ref.shape[2] // POOL_GROUPS
    ogw = wpool_ref.shape[2]
    pos = pos0 + i * tl + lax.broadcasted_iota(jnp.int32, (nseg, tl, gw), 1)
    a = a_ref[...]
    h = h_ref[...]
    for g, w in enumerate(POOL_WINDOWS):
        os_ = slice(g * ogw, (g + 1) * ogw)
        br_a = _dot(a, wup_ref[:, os_])
        ga = _dot(h, wga_ref[:, os_])
        gb = _dot(h, wgb_ref[:, os_])
        cs = slice(g * gw, (g + 1) * gw)
        cur = ext_ref[:, HIST_ROWS:HIST_ROWS + tl, cs]
        tot = cur
        for s in range(1, w):
            tot = tot + ext_ref[:, HIST_ROWS - s:HIST_ROWS - s + tl, cs]
        cnt = jnp.minimum(pos + 1, w).astype(F32)
        pooled = (tot / cnt - cur).reshape(nseg * tl, gw).astype(BF16)
        br_b = _dot(pooled, wpool_ref[g]) * ps_ref[:, os_]
        m_ref[:, os_] = (jax.nn.sigmoid(ga) * br_a + jax.nn.sigmoid(gb) * br_b).astype(BF16)
    x1 = x_ref[...] + _rms(_dot(m_ref[...], wout_ref[...])) * gpost_ref[...]
    x1_ref[...] = x1
    h2_ref[...] = (_rms(x1) * gpre_ref[...]).astype(BF16)


def _merge(a, h, u3, hist, hist_map, x, wup, wga, wgb, wpool, ps, wout, gpost, gpre,
           *, nseg, tl, pos0, zero_first, to_bf16=()):
    s_total, l_total, p = u3.shape
    n, d = x.shape
    tm = nseg * tl
    steps = l_total // tl
    assert nseg == 1 or steps == 1
    row = lambda width: pl.BlockSpec((tm, width), lambda s, i: (s * steps + i, 0))
    n_steps = (s_total // nseg) * steps
    cast_in, cast_out, cast_shapes = _slab_specs(to_bf16, n_steps, lambda s, i: (s * steps + i, 0))
    est = (_nbytes(wup.shape, BF16) + 2 * _nbytes(wga.shape, BF16) + _nbytes(wpool.shape, BF16)
           + _nbytes(wout.shape, BF16)
           + 2 * (_nbytes((tm, a.shape[1]), BF16) + 2 * _nbytes((tm, d), BF16) + 2 * _nbytes((tm, d), F32)
                  + _nbytes((tm + HIST_ROWS, p), F32))
           + _nbytes((tm + nseg * HIST_ROWS, p), F32) + _nbytes((tm, d), BF16) + 6 * _nbytes((tm, d), F32)
           + sum(3 * _nbytes(w.shape, F32) // n_steps for w, _ in to_bf16))
    return pl.pallas_call(
        functools.partial(_merge_kernel, pos0=pos0, zero_first=zero_first, n_cast=len(to_bf16)),
        grid=(s_total // nseg, steps),
        in_specs=[row(a.shape[1]), row(d),
                  pl.BlockSpec((nseg, tl, p), lambda s, i: (s, i, 0)),
                  pl.BlockSpec((nseg, HIST_ROWS, p), hist_map),
                  row(d),
                  _resident(wup.shape), _resident(wga.shape), _resident(wgb.shape), _resident(wpool.shape),
                  _resident((1, d)), _resident(wout.shape), _resident((1, d)), _resident((1, d))]
                 + cast_in,
        out_specs=(row(d), row(d)) + cast_out,
        out_shape=(jax.ShapeDtypeStruct((n, d), F32), jax.ShapeDtypeStruct((n, d), BF16)) + cast_shapes,
        scratch_shapes=[pltpu.VMEM((nseg, HIST_ROWS + tl, p), F32), pltpu.VMEM((tm, d), BF16)],
        compiler_params=pltpu.CompilerParams(
            dimension_semantics=("arbitrary", "arbitrary"), vmem_limit_bytes=_vmem_limit(est)),
        name="merge",
    )(a, h, u3, hist, x, wup, wga, wgb, wpool, ps, wout, gpost, gpre, *[w for w, _ in to_bf16])


def _ffn_kernel(h_ref, x1_ref, w1_ref, w2_ref, g_ref, o_ref):
    f = pl.program_id(1)

    @pl.when(f == 0)
    def _():
        o_ref[...] = jnp.zeros_like(o_ref)

    z = jnp.square(jnp.maximum(_dot(h_ref[...], w1_ref[0]), 0.0)).astype(BF16)
    o_ref[...] += _dot(z, w2_ref[...])

    @pl.when(f == pl.num_programs(1) - 1)
    def _():
        o_ref[...] = x1_ref[...] + _rms(o_ref[...]) * g_ref[...]


def _ffn(h2, x1, w1, w2, g, *, tm):
    n, d = x1.shape
    tf = w1.shape[2]
    dff = w2.shape[0]
    est = (2 * _nbytes((tm, d), BF16) + 4 * _nbytes((tm, d), F32)
           + 4 * _nbytes((d, tf), BF16) + 2 * _nbytes((tm, tf), F32) + _nbytes((tm, d), F32))
    return pl.pallas_call(
        _ffn_kernel,
        grid=(n // tm, dff // tf),
        in_specs=[pl.BlockSpec((tm, d), lambda i, f: (i, 0)),
                  pl.BlockSpec((tm, d), lambda i, f: (i, 0)),
                  pl.BlockSpec((1, d, tf), lambda i, f: (f, 0, 0)),
                  pl.BlockSpec((tf, d), lambda i, f: (f, 0)),
                  pl.BlockSpec((1, d), lambda i, f: (0, 0))],
        out_specs=pl.BlockSpec((tm, d), lambda i, f: (i, 0)),
        out_shape=jax.ShapeDtypeStruct((n, d), F32),
        compiler_params=pltpu.CompilerParams(
            dimension_semantics=("arbitrary", "arbitrary"), vmem_limit_bytes=_vmem_limit(est)),
        name="ffn",
    )(h2, x1, w1, w2, g)


def _tile(n, pref):
    t = min(n, pref)
    while n % t:
        t //= 2
    return t


INPROJ_TM = 256
ATTN_TQ = 512
MERGE_TM = 256
FFN_TM = 512
FFN_TF = 1024
CUMSUM_ROWS = 32
REPACK_ROWS = 256


def _cum_logf_rows(logf_bht):
    b, h, t = logf_bht.shape
    rows = logf_bht.reshape(b * h, t)
    pad = (-t) % LANES
    if pad:
        rows = jnp.pad(rows, ((0, 0), (0, pad)))
    return _cumsum_lanes(rows)


def _time_minor(x, nbatch):
    return jnp.transpose(x, (0, 2, 3, 1)).reshape(nbatch, -1, x.shape[1])


def _layer(xp, xs, cache_k, cache_v, cache_logf, state_pool, g_mix_pre, w_in, b_f, w_attn_up, w_pool,
           pool_scale, w_out, g_mix_post, g_ffn_pre, w_ff1, w_ff2, g_ffn_post):
    bp, tp, d = xp.shape
    bs, ts, _ = xs.shape
    past = cache_k.shape[1]
    a, nh, p = ATTN_W, N_HEADS, w_pool.shape[0] * w_pool.shape[1]
    off_f = 3 * a
    off_u = off_f + nh
    off_ga = off_u + p
    off_gb = off_ga + d

    w_qkvu, w_f, w_ga, w_gb = _repack_w_in(jnp.transpose(w_in.astype(F32)), off_f=off_f, off_u=off_u,
                                           off_ga=off_ga, off_gb=off_gb, tr=_tile(w_in.shape[0], REPACK_ROWS))
    b_f2 = jnp.pad(b_f.reshape(1, -1), ((0, 0), (0, LANES - nh)))
    row = lambda v: v.reshape(1, -1)
    g_pre, ps = row(g_mix_pre), row(pool_scale)
    g_post, g_fpre, g_fpost = row(g_mix_post), row(g_ffn_pre), row(g_ffn_post)

    def project(x3, time_minor, to_bf16=()):
        x2 = x3.reshape(-1, d)
        tm = _tile(x3.shape[1] if time_minor else x2.shape[0], INPROJ_TM)
        return x2, _inproj(x2, g_pre, w_qkvu, w_f, b_f2, tm=tm, seq=x3.shape[1], time_minor=time_minor,
                           to_bf16=to_bf16)

    def finish(x2, h, a_out, u3, hist, hist_map, nseg, tl, pos0, zero_first, ffn_w):
        tf = _tile(w_ff1.shape[1], FFN_TF)
        to_bf16 = () if ffn_w else ((w_ff1.astype(F32), tf), (w_ff2.astype(F32), None))
        x1, h2, *cast = _merge(a_out, h, u3, hist, hist_map, x2, w_up, w_ga, w_gb, w_pl, ps, w_o, g_post, g_fpre,
                               nseg=nseg, tl=tl, pos0=pos0, zero_first=zero_first, to_bf16=to_bf16)
        w1, w2 = ffn_w or cast
        return _ffn(h2, x1, w1, w2, g_fpost, tm=_tile(x2.shape[0], FFN_TM)), (w1, w2)

    merge_w = ((w_attn_up.astype(F32), None), (w_pool.astype(F32).reshape(p, -1), None), (w_out.astype(F32), None))
    x2, (h, q, kb, kt, vt, _, u, logft, w_up, w_pl, w_o) = project(xp, True, merge_w)
    w_pl = w_pl.reshape(w_pool.shape)
    c_rows = _cum_logf_rows(logft)
    nblk = nh // HEADS_PER_BLOCK
    three = lambda z: z.reshape(bp, tp, a)
    a_out = _attn_prompt(three(q), three(kb), vt, c_rows.reshape(bp, nblk, HEADS_PER_BLOCK, tp),
                         tq=_tile(tp, ATTN_TQ))
    to_bthd = lambda zt: jnp.transpose(zt.reshape(bp, nh, HEAD_DIM, tp), (0, 3, 1, 2))
    u3 = u.reshape(bp, tp, p)
    tl = _tile(tp, MERGE_TM)
    blocks_per_tile = tl // HIST_ROWS
    hist_map = lambda s, i: (s, jnp.maximum(i * blocks_per_tile - 1, 0), 0)
    yp, ffn_w = finish(x2, h, a_out.reshape(bp * tp, a), u3, u3, hist_map, 1, tl, 0, True, None)
    yp = yp.reshape(bp, tp, d)
    prompt_out = (yp, to_bthd(kt), to_bthd(vt), jnp.transpose(logft, (0, 2, 1)),
                  jnp.concatenate([jnp.zeros((bp, POOL_HIST, p), F32), u3], axis=1)[:, -POOL_HIST:])

    x2, (h, q, kb, k, v, vb, u, logf) = project(xs, False)
    logf3 = logf.reshape(bs, ts, nh)
    f_all = jnp.concatenate([cache_logf.astype(F32), logf3], axis=1)
    c_all = _cum_logf_rows(jnp.transpose(f_all, (0, 2, 1))).reshape(bs, nh, -1)
    c_cache = c_all[:, :, :past]
    c_new = c_all[:, :, past:past + ts]
    three = lambda z: z.reshape(bs, ts, a)
    a_out = _attn_sample(three(q), _time_minor(cache_k.astype(F32), bs), _time_minor(cache_v.astype(F32), bs),
                         three(kb), three(vb), c_cache, c_new, jnp.transpose(c_new, (0, 2, 1)))
    u3 = u.reshape(bs, ts, p)
    hist = jnp.pad(state_pool.astype(F32), ((0, 0), (HIST_ROWS - POOL_HIST, 0), (0, 0)))
    ys, _ = finish(x2, h, a_out.reshape(bs * ts, a), u3, hist, lambda s, i: (s, 0, 0), bs, ts, past, False, ffn_w)
    ys = ys.reshape(bs, ts, d)
    sample_out = (ys, k.reshape(bs, ts, nh, HEAD_DIM), v.reshape(bs, ts, nh, HEAD_DIM), logf3,
                  jnp.concatenate([state_pool.astype(F32), u3], axis=1)[:, -POOL_HIST:])
    return prompt_out, sample_out


def kernel(x_prompt, x_sample, cache_k, cache_v, cache_logf, state_pool, g_mix_pre, w_in, b_f, w_attn_up,
           w_pool, pool_scale, w_out, g_mix_post, g_ffn_pre, w_ff1, w_ff2, g_ffn_post):
    depth = w_in.shape[0]
    xp, xs = x_prompt, x_sample
    per_layer = []
    for l in range(depth):
        po, so = _layer(xp, xs, cache_k[l], cache_v[l], cache_logf[l], state_pool[l], g_mix_pre[l], w_in[l],
                        b_f[l], w_attn_up[l], w_pool[l], pool_scale[l], w_out[l], g_mix_post[l],
                        g_ffn_pre[l], w_ff1[l], w_ff2[l], g_ffn_post[l])
        xp, xs = po[0], so[0]
        per_layer.append(po[1:] + so[1:])
    stacked = [jnp.stack(leaves, 0) for leaves in zip(*per_layer)]
    return (xp, xs, *stacked)
```

```python
import functools
import math

import jax
import jax.numpy as jnp
from jax import lax
from jax.experimental import pallas as pl
from jax.experimental.pallas import tpu as pltpu

N_HEADS = 16
HEAD_DIM = 64
ATTN_W = N_HEADS * HEAD_DIM
POOL_WINDOWS = (2, 4, 8, 16)
POOL_GROUPS = len(POOL_WINDOWS)
POOL_HIST = max(POOL_WINDOWS) - 1
EPS = 1e-6
SCALE = HEAD_DIM ** -0.5

LANES = 128
VMEM_LIMIT_CAP = 60000 * 1024

HEADS_PER_BLOCK = LANES // HEAD_DIM
HIST_ROWS = 16

F32 = jnp.float32
BF16 = jnp.bfloat16


def _vmem_limit(nbytes):
    return int(min(VMEM_LIMIT_CAP, max(32 * 1024 * 1024, nbytes * 5 // 4)))


def _nbytes(shape, dtype):
    return math.prod(shape) * jnp.dtype(dtype).itemsize


def _rms(x):
    return x * lax.rsqrt(jnp.mean(x * x, axis=-1, keepdims=True) + EPS)


def _dot(a, b):
    return jnp.dot(a, b, preferred_element_type=F32)


def _dot_nt(a, b):
    return lax.dot_general(a, b, (((1,), (1,)), ((), ())), preferred_element_type=F32)


def _resident(shape):
    zeros = (0,) * len(shape)
    return pl.BlockSpec(shape, lambda *_: zeros, pipeline_mode=pl.Buffered(1))


def _repack_kernel(wt_ref, qkvu_ref, f_ref, ga_ref, gb_ref, *, off_f, off_u, off_ga, off_gb):
    nh = off_u - off_f
    p = off_ga - off_u
    d = off_gb - off_ga
    piece = lambda lo, hi: wt_ref[lo:hi, :].T.astype(BF16)
    qkvu_ref[:, 0:off_f] = piece(0, off_f)
    qkvu_ref[:, off_f:off_f + p] = piece(off_u, off_ga)
    fwin = wt_ref[off_f:off_f + LANES, :].T
    lane = lax.broadcasted_iota(jnp.int32, fwin.shape, 1)
    f_ref[...] = jnp.where(lane < nh, fwin, 0.0).astype(BF16)
    ga_ref[...] = piece(off_ga, off_gb)
    gb_ref[...] = piece(off_gb, off_gb + d)


def _repack_w_in(wt, *, off_f, off_u, off_ga, off_gb, tr):
    width, rows = wt.shape
    p = off_ga - off_u
    d = off_gb - off_ga
    blk = lambda cols: pl.BlockSpec((tr, cols), lambda i: (i, 0))
    est = 4 * _nbytes((width, tr), F32) + 2 * _nbytes((tr, off_f + p + LANES + 2 * d), BF16)
    return pl.pallas_call(
        functools.partial(_repack_kernel, off_f=off_f, off_u=off_u, off_ga=off_ga, off_gb=off_gb),
        grid=(rows // tr,),
        in_specs=[pl.BlockSpec((width, tr), lambda i: (0, i))],
        out_specs=(blk(off_f + p), blk(LANES), blk(d), blk(d)),
        out_shape=(jax.ShapeDtypeStruct((rows, off_f + p), BF16), jax.ShapeDtypeStruct((rows, LANES), BF16),
                   jax.ShapeDtypeStruct((rows, d), BF16), jax.ShapeDtypeStruct((rows, d), BF16)),
        compiler_params=pltpu.CompilerParams(
            dimension_semantics=("arbitrary",), vmem_limit_bytes=_vmem_limit(est)),
        name="repack_w_in",
    )(wt)


def _convert_slabs(refs, n_cast):
    for src, dst in zip(refs[:n_cast], refs[len(refs) - n_cast:]):
        if len(dst.shape) == 2:
            dst[...] = src[...].astype(dst.dtype)
        else:
            for c in range(dst.shape[0]):
                dst[c] = src[:, c * dst.shape[2]:(c + 1) * dst.shape[2]].astype(dst.dtype)
    return refs[n_cast:len(refs) - n_cast]


def _slab_specs(to_bf16, n_steps, index_map):
    in_specs, out_specs, shapes = [], [], []
    for w, chunk in to_bf16:
        rows, cols = w.shape
        slab = rows // n_steps
        in_specs.append(pl.BlockSpec((slab, cols), index_map))
        if chunk is None:
            out_specs.append(pl.BlockSpec((slab, cols), index_map))
            shapes.append(jax.ShapeDtypeStruct((rows, cols), BF16))
        else:
            out_specs.append(pl.BlockSpec((cols // chunk, slab, chunk), lambda *g: (0, index_map(*g)[0], 0)))
            shapes.append(jax.ShapeDtypeStruct((cols // chunk, rows, chunk), BF16))
    return in_specs, tuple(out_specs), tuple(shapes)


def _inproj_kernel(x_ref, g_ref, w_ref, wf_ref, bf_ref, *rest, time_minor, n_cast):
    h_ref, q_ref, kb_ref, k_ref, v_ref, vb_ref, u_ref, f_ref = _convert_slabs(rest, n_cast)
    h = (_rms(x_ref[...]) * g_ref[...]).astype(BF16)
    h_ref[...] = h
    a = q_ref.shape[1]
    f = _dot(h, wf_ref[...]) + bf_ref[...]
    k = _dot(h, w_ref[:, a:2 * a])
    kb_ref[...] = k.astype(BF16)
    v = _dot(h, w_ref[:, 2 * a:3 * a])
    vb_ref[...] = v.astype(BF16)
    q_ref[...] = (_dot(h, w_ref[:, 0:a]) * SCALE).astype(BF16)
    u_ref[...] = _dot(h, w_ref[:, 3 * a:])
    if time_minor:
        k_ref[0] = k.T
        v_ref[0] = v.T
        f_ref[0] = jax.nn.log_sigmoid(f.T[0:f_ref.shape[1], :])
    else:
        k_ref[...] = k
        v_ref[...] = v
        f_ref[...] = jax.nn.log_sigmoid(f[:, 0:f_ref.shape[1]])


def _inproj(x, g, w, wf, bf, *, tm, seq, time_minor, to_bf16=()):
    n, d = x.shape
    cast_in, cast_out, cast_shapes = _slab_specs(to_bf16, n // tm, lambda i: (i, 0))
    a = ATTN_W
    p = w.shape[1] - 3 * a
    nh = N_HEADS
    row = lambda width: pl.BlockSpec((tm, width), lambda i: (i, 0))
    if time_minor:
        steps = seq // tm
        feat = lambda rows: (jax.ShapeDtypeStruct((n // seq, rows, seq), F32),
                             pl.BlockSpec((1, rows, tm), lambda i: (i // steps, 0, i % steps)))
    else:
        feat = lambda rows: (jax.ShapeDtypeStruct((n, rows), F32), row(rows))
    (k_shape, k_spec), (f_shape, f_spec) = feat(a), feat(nh)
    out_shape = (
        jax.ShapeDtypeStruct((n, d), BF16),
        jax.ShapeDtypeStruct((n, a), BF16),
        jax.ShapeDtypeStruct((n, a), BF16),
        k_shape,
        k_shape,
        jax.ShapeDtypeStruct((n, a), BF16),
        jax.ShapeDtypeStruct((n, p), F32),
        f_shape,
    )
    est = (2 * _nbytes((tm, d), F32) + _nbytes(w.shape, BF16) + _nbytes(wf.shape, BF16)
           + 2 * (_nbytes((tm, d), BF16) + 3 * _nbytes((tm, a), BF16) + 2 * _nbytes((tm, a), F32)
                  + _nbytes((tm, p), F32) + _nbytes((tm, LANES), F32))
           + 4 * _nbytes((tm, a), F32) + sum(3 * _nbytes(c.shape, F32) * tm // n for c, _ in to_bf16))
    return pl.pallas_call(
        functools.partial(_inproj_kernel, time_minor=time_minor, n_cast=len(to_bf16)),
        grid=(n // tm,),
        in_specs=[row(d), _resident((1, d)), _resident(w.shape), _resident(wf.shape), _resident((1, LANES))]
                 + cast_in,
        out_specs=(row(d), row(a), row(a), k_spec, k_spec, row(a), row(p), f_spec) + cast_out,
        out_shape=out_shape + cast_shapes,
        compiler_params=pltpu.CompilerParams(
            dimension_semantics=("arbitrary",), vmem_limit_bytes=_vmem_limit(est)),
        name="inproj",
    )(x, g, w, wf, bf, *[c for c, _ in to_bf16])


def _cumsum_kernel(x_ref, o_ref):
    x = x_ref[...]
    length = x.shape[1]
    lane = lax.broadcasted_iota(jnp.int32, x.shape, 1)
    shift = 1
    while shift < length:
        x = x + jnp.where(lane >= shift, pltpu.roll(x, shift, axis=1), 0.0)
        shift *= 2
    o_ref[...] = x


def _cumsum_lanes(x):
    rows, length = x.shape
    tr = _tile(rows, CUMSUM_ROWS)
    spec = pl.BlockSpec((tr, length), lambda i: (i, 0))
    return pl.pallas_call(
        _cumsum_kernel,
        grid=(rows // tr,),
        in_specs=[spec],
        out_specs=spec,
        out_shape=jax.ShapeDtypeStruct(x.shape, F32),
        compiler_params=pltpu.CompilerParams(dimension_semantics=("arbitrary",)),
        name="cumsum",
    )(x)


VALUE_ROWS = HEAD_DIM + 16


def _attn_kernel(q_ref, k_ref, vt_ref, c_ref, o_ref,
                 vta_ref, ckrep_ref, s_ref, p_ref, alpha_ref, m_ref, acc_ref, *, tq):
    t = k_ref.shape[1]
    nq = t // tq
    for a in range(HEADS_PER_BLOCK):
        vta_ref[a, 0:HEAD_DIM, :] = vt_ref[0, a * HEAD_DIM:(a + 1) * HEAD_DIM, :].astype(BF16)
        vta_ref[a, HEAD_DIM:VALUE_ROWS, :] = jnp.ones((VALUE_ROWS - HEAD_DIM, t), BF16)
        ckrep_ref[a] = jnp.broadcast_to(c_ref[0, 0, a:a + 1, :], (LANES, t)).T

    pairs = [(i, j) for i in range(nq) for j in range(i + 1)]
    rows = lambda j: slice(j * tq, (j + 1) * tq)

    def scores(w):
        i, j = pairs[w]
        q2 = q_ref[0, rows(i), :]
        lane = lax.broadcasted_iota(jnp.int32, q2.shape, 1)
        kt = k_ref[0, rows(j), :]
        for a in range(HEADS_PER_BLOCK):
            qa = jnp.where((lane // HEAD_DIM) == a, q2, jnp.zeros_like(q2))
            ckr = ckrep_ref[a, rows(j), :]
            s_ref[w % 2, a] = _dot_nt(kt, qa) - jnp.concatenate([ckr] * (tq // LANES), axis=1)

    def probs(w):
        i, j = pairs[w]
        for a in range(HEADS_PER_BLOCK):
            s = s_ref[w % 2, a]
            if j == i:
                r = lax.broadcasted_iota(jnp.int32, s.shape, 0)
                c = lax.broadcasted_iota(jnp.int32, s.shape, 1)
                s = jnp.where(r <= c, s, -jnp.inf)
            cqa = c_ref[0, 0, a:a + 1, rows(i)]
            smax = jnp.max(s, axis=0, keepdims=True) + cqa
            if j == 0:
                m_new = smax
            else:
                m_prev = m_ref[a]
                m_new = jnp.maximum(m_prev, smax)
                alpha_ref[w % 2, a] = jnp.exp(m_prev - m_new)
            p_ref[w % 2, a] = jnp.exp(s - (m_new - cqa)).astype(BF16)
            m_ref[a] = m_new

    def values(w):
        i, j = pairs[w]
        for a in range(HEADS_PER_BLOCK):
            pv = _dot(vta_ref[a, :, rows(j)], p_ref[w % 2, a])
            acc_ref[a] = pv if j == 0 else alpha_ref[w % 2, a] * acc_ref[a] + pv
        if j == i:
            ot = jnp.concatenate(
                [acc_ref[a, 0:HEAD_DIM, :] / acc_ref[a, HEAD_DIM:HEAD_DIM + 1, :]
                 for a in range(HEADS_PER_BLOCK)], axis=0)
            o_ref[0, rows(i), :] = ot.T.astype(o_ref.dtype)

    n = len(pairs)
    scores(0)
    for w in range(n):
        if w + 1 < n:
            scores(w + 1)
        probs(w)
        if w >= 1:
            values(w - 1)
    values(n - 1)


def _attn_prompt(q, kb, vt, c_rows, *, tq):
    b, t, a = q.shape
    nblk = a // LANES
    spec = pl.BlockSpec((1, t, LANES), lambda bi, hi: (bi, 0, hi))
    return pl.pallas_call(
        functools.partial(_attn_kernel, tq=tq),
        grid=(b, nblk),
        in_specs=[spec, spec, pl.BlockSpec((1, LANES, t), lambda bi, hi: (bi, hi, 0)),
                  pl.BlockSpec((1, 1, HEADS_PER_BLOCK, t), lambda bi, hi: (bi, hi, 0, 0))],
        out_specs=spec,
        out_shape=jax.ShapeDtypeStruct((b, t, a), BF16),
        scratch_shapes=[pltpu.VMEM((HEADS_PER_BLOCK, VALUE_ROWS, t), BF16),
                        pltpu.VMEM((HEADS_PER_BLOCK, t, LANES), F32),
                        pltpu.VMEM((2, HEADS_PER_BLOCK, tq, tq), F32),
                        pltpu.VMEM((2, HEADS_PER_BLOCK, tq, tq), BF16),
                        pltpu.VMEM((2, HEADS_PER_BLOCK, 1, tq), F32),
                        pltpu.VMEM((HEADS_PER_BLOCK, 1, tq), F32),
                        pltpu.VMEM((HEADS_PER_BLOCK, VALUE_ROWS, tq), F32)],
        compiler_params=pltpu.CompilerParams(dimension_semantics=("arbitrary", "arbitrary")),
        name="attn_prompt",
    )(q, kb, vt, c_rows)


def _attn_sample_kernel(q_ref, kct_ref, vct_ref, kn_ref, vn_ref, cc_ref, cnr_ref, cnc_ref, o_ref,
                        kb_ref, vb_ref, s_ref, p_ref, pn_ref):
    n, a = q_ref.shape[1], q_ref.shape[2]
    nh = a // HEAD_DIM
    q = q_ref[0]
    qt = jnp.concatenate([q] * nh, axis=0)
    row_h = lax.broadcasted_iota(jnp.int32, qt.shape, 0) // n
    col_h = lax.broadcasted_iota(jnp.int32, qt.shape, 1) // HEAD_DIM
    qbd = jnp.where(row_h == col_h, qt, jnp.zeros_like(qt))
    kb_ref[...] = kct_ref[0].astype(BF16)
    vb_ref[...] = vct_ref[0].astype(BF16)
    s_ref[...] = _dot(qbd, kb_ref[...])
    s_new = _dot_nt(qbd, kn_ref[0])
    cnc = cnc_ref[0]
    r = lax.broadcasted_iota(jnp.int32, (n, n), 0)
    c = lax.broadcasted_iota(jnp.int32, (n, n), 1)
    inv_l = []
    for h in range(nh):
        rows = slice(h * n, (h + 1) * n)
        cq = cnc[:, h:h + 1]
        sc = s_ref[rows, :] + (cq - cc_ref[0, h:h + 1, :])
        sn = s_new[rows, :] + (cq - cnr_ref[0, h:h + 1, :])
        sn = jnp.where(c <= r, sn, -jnp.inf)
        m = jnp.maximum(jnp.max(sc, axis=1, keepdims=True), jnp.max(sn, axis=1, keepdims=True))
        pc = jnp.exp(sc - m)
        pn = jnp.exp(sn - m)
        inv_l.append(1.0 / (jnp.sum(pc, axis=1, keepdims=True) + jnp.sum(pn, axis=1, keepdims=True)))
        p_ref[rows, :] = pc.astype(BF16)
        pn_ref[rows, :] = pn.astype(BF16)
    o = _dot_nt(p_ref[...], vb_ref[...]) + _dot(pn_ref[...], vn_ref[0])
    for h in range(nh):
        rows = slice(h * n, (h + 1) * n)
        cols = slice(h * HEAD_DIM, (h + 1) * HEAD_DIM)
        o_ref[0, :, cols] = (o[rows, cols] * inv_l[h]).astype(o_ref.dtype)


def _attn_sample(q, kct, vct, kn, vn, cc, cnr, cnc):
    b, n, a = q.shape
    past = kct.shape[2]
    nh = a // HEAD_DIM
    per_b = lambda *tail: pl.BlockSpec((1,) + tail, lambda bi: (bi,) + (0,) * len(tail))
    est = (4 * _nbytes((a, past), F32) + 2 * _nbytes((a, past), BF16)
           + _nbytes((nh * n, past), F32) + _nbytes((nh * n, past), BF16) + 4 * _nbytes((nh * n, a), F32))
    return pl.pallas_call(
        _attn_sample_kernel,
        grid=(b,),
        in_specs=[per_b(n, a), per_b(a, past), per_b(a, past), per_b(n, a), per_b(n, a),
                  per_b(nh, past), per_b(nh, n), per_b(n, nh)],
        out_specs=per_b(n, a),
        out_shape=jax.ShapeDtypeStruct((b, n, a), BF16),
        scratch_shapes=[pltpu.VMEM((a, past), BF16), pltpu.VMEM((a, past), BF16),
                        pltpu.VMEM((nh * n, past), F32), pltpu.VMEM((nh * n, past), BF16),
                        pltpu.VMEM((nh * n, n), BF16)],
        compiler_params=pltpu.CompilerParams(
            dimension_semantics=("arbitrary",), vmem_limit_bytes=_vmem_limit(est)),
        name="attn_sample",
    )(q, kct, vct, kn, vn, cc, cnr, cnc)


def _merge_kernel(a_ref, h_ref, u_ref, hist_ref, x_ref, wup_ref, wga_ref, wgb_ref, wpool_ref, ps_ref,
                  wout_ref, gpost_ref, gpre_ref, *rest, pos0, zero_first, n_cast):
    ext_ref, m_ref = rest[len(rest) - 2:]
    x1_ref, h2_ref = _convert_slabs(rest[:len(rest) - 2], n_cast)
    nseg, tl, _ = u_ref.shape
    i = pl.program_id(1)
    hist = hist_ref[...]
    if zero_first:
        hist = jnp.where(i == 0, 0.0, hist)
    ext_ref[:, 0:HIST_ROWS, :] = hist
    ext_ref[:, HIST_ROWS:HIST_ROWS + tl, :] = u_ref[...]
    gw = u_ref.shape[2] // POOL_GROUPS
    ogw = wpool_ref.shape[2]
    pos = pos0 + i * tl + lax.broadcasted_iota(jnp.int32, (nseg, tl, gw), 1)
    a = a_ref[...]
    h = h_ref[...]
    for g, w in enumerate(POOL_WINDOWS):
        os_ = slice(g * ogw, (g + 1) * ogw)
        br_a = _dot(a, wup_ref[:, os_])
        ga = _dot(h, wga_ref[:, os_])
        gb = _dot(h, wgb_ref[:, os_])
        cs = slice(g * gw, (g + 1) * gw)
        cur = ext_ref[:, HIST_ROWS:HIST_ROWS + tl, cs]
        tot = cur
        for s in range(1, w):
            tot = tot + ext_ref[:, HIST_ROWS - s:HIST_ROWS - s + tl, cs]
        cnt = jnp.minimum(pos + 1, w).astype(F32)
        pooled = (tot / cnt - cur).reshape(nseg * tl, gw).astype(BF16)
        br_b = _dot(pooled, wpool_ref[g]) * ps_ref[:, os_]
        m_ref[:, os_] = (jax.nn.sigmoid(ga) * br_a + jax.nn.sigmoid(gb) * br_b).astype(BF16)
    x1 = x_ref[...] + _rms(_dot(m_ref[...], wout_ref[...])) * gpost_ref[...]
    x1_ref[...] = x1
    h2_ref[...] = (_rms(x1) * gpre_ref[...]).astype(BF16)


def _merge(a, h, u3, hist, hist_map, x, wup, wga, wgb, wpool, ps, wout, gpost, gpre,
           *, nseg, tl, pos0, zero_first, to_bf16=()):
    s_total, l_total, p = u3.shape
    n, d = x.shape
    tm = nseg * tl
    steps = l_total // tl
    assert nseg == 1 or steps == 1
    row = lambda width: pl.BlockSpec((tm, width), lambda s, i: (s * steps + i, 0))
    n_steps = (s_total // nseg) * steps
    cast_in, cast_out, cast_shapes = _slab_specs(to_bf16, n_steps, lambda s, i: (s * steps + i, 0))
    est = (_nbytes(wup.shape, BF16) + 2 * _nbytes(wga.shape, BF16) + _nbytes(wpool.shape, BF16)
           + _nbytes(wout.shape, BF16)
           + 2 * (_nbytes((tm, a.shape[1]), BF16) + 2 * _nbytes((tm, d), BF16) + 2 * _nbytes((tm, d), F32)
                  + _nbytes((tm + HIST_ROWS, p), F32))
           + _nbytes((tm + nseg * HIST_ROWS, p), F32) + _nbytes((tm, d), BF16) + 6 * _nbytes((tm, d), F32)
           + sum(3 * _nbytes(w.shape, F32) // n_steps for w, _ in to_bf16))
    return pl.pallas_call(
        functools.partial(_merge_kernel, pos0=pos0, zero_first=zero_first, n_cast=len(to_bf16)),
        grid=(s_total // nseg, steps),
        in_specs=[row(a.shape[1]), row(d),
                  pl.BlockSpec((nseg, tl, p), lambda s, i: (s, i, 0)),
                  pl.BlockSpec((nseg, HIST_ROWS, p), hist_map),
                  row(d),
                  _resident(wup.shape), _resident(wga.shape), _resident(wgb.shape), _resident(wpool.shape),
                  _resident((1, d)), _resident(wout.shape), _resident((1, d)), _resident((1, d))]
                 + cast_in,
        out_specs=(row(d), row(d)) + cast_out,
        out_shape=(jax.ShapeDtypeStruct((n, d), F32), jax.ShapeDtypeStruct((n, d), BF16)) + cast_shapes,
        scratch_shapes=[pltpu.VMEM((nseg, HIST_ROWS + tl, p), F32), pltpu.VMEM((tm, d), BF16)],
        compiler_params=pltpu.CompilerParams(
            dimension_semantics=("arbitrary", "arbitrary"), vmem_limit_bytes=_vmem_limit(est)),
        name="merge",
    )(a, h, u3, hist, x, wup, wga, wgb, wpool, ps, wout, gpost, gpre, *[w for w, _ in to_bf16])


def _ffn_kernel(h_ref, x1_ref, w1_ref, w2_ref, g_ref, o_ref):
    f = pl.program_id(1)

    @pl.when(f == 0)
    def _():
        o_ref[...] = jnp.zeros_like(o_ref)

    z = jnp.square(jnp.maximum(_dot(h_ref[...], w1_ref[0]), 0.0)).astype(BF16)
    o_ref[...] += _dot(z, w2_ref[...])

    @pl.when(f == pl.num_programs(1) - 1)
    def _():
        o_ref[...] = x1_ref[...] + _rms(o_ref[...]) * g_ref[...]


def _ffn(h2, x1, w1, w2, g, *, tm):
    n, d = x1.shape
    tf = w1.shape[2]
    dff = w2.shape[0]
    est = (2 * _nbytes((tm, d), BF16) + 4 * _nbytes((tm, d), F32)
           + 4 * _nbytes((d, tf), BF16) + 2 * _nbytes((tm, tf), F32) + _nbytes((tm, d), F32))
    return pl.pallas_call(
        _ffn_kernel,
        grid=(n // tm, dff // tf),
        in_specs=[pl.BlockSpec((tm, d), lambda i, f: (i, 0)),
                  pl.BlockSpec((tm, d), lambda i, f: (i, 0)),
                  pl.BlockSpec((1, d, tf), lambda i, f: (f, 0, 0)),
                  pl.BlockSpec((tf, d), lambda i, f: (f, 0)),
                  pl.BlockSpec((1, d), lambda i, f: (0, 0))],
        out_specs=pl.BlockSpec((tm, d), lambda i, f: (i, 0)),
        out_shape=jax.ShapeDtypeStruct((n, d), F32),
        compiler_params=pltpu.CompilerParams(
            dimension_semantics=("arbitrary", "arbitrary"), vmem_limit_bytes=_vmem_limit(est)),
        name="ffn",
    )(h2, x1, w1, w2, g)


def _tile(n, pref):
    t = min(n, pref)
    while n % t:
        t //= 2
    return t


INPROJ_TM = 512
ATTN_TQ = 512
MERGE_TM = 256
FFN_TM = 512
FFN_TF = 1024
CUMSUM_ROWS = 32
REPACK_ROWS = 256


def _cum_logf_rows(logf_bht):
    b, h, t = logf_bht.shape
    rows = logf_bht.reshape(b * h, t)
    pad = (-t) % LANES
    if pad:
        rows = jnp.pad(rows, ((0, 0), (0, pad)))
    return _cumsum_lanes(rows)


def _time_minor(x, nbatch):
    return jnp.transpose(x, (0, 2, 3, 1)).reshape(nbatch, -1, x.shape[1])


def _layer(xp, xs, cache_k, cache_v, cache_logf, state_pool, g_mix_pre, w_in, b_f, w_attn_up, w_pool,
           pool_scale, w_out, g_mix_post, g_ffn_pre, w_ff1, w_ff2, g_ffn_post):
    bp, tp, d = xp.shape
    bs, ts, _ = xs.shape
    past = cache_k.shape[1]
    a, nh, p = ATTN_W, N_HEADS, w_pool.shape[0] * w_pool.shape[1]
    off_f = 3 * a
    off_u = off_f + nh
    off_ga = off_u + p
    off_gb = off_ga + d

    w_qkvu, w_f, w_ga, w_gb = _repack_w_in(jnp.transpose(w_in.astype(F32)), off_f=off_f, off_u=off_u,
                                           off_ga=off_ga, off_gb=off_gb, tr=_tile(w_in.shape[0], REPACK_ROWS))
    b_f2 = jnp.pad(b_f.reshape(1, -1), ((0, 0), (0, LANES - nh)))
    row = lambda v: v.reshape(1, -1)
    g_pre, ps = row(g_mix_pre), row(pool_scale)
    g_post, g_fpre, g_fpost = row(g_mix_post), row(g_ffn_pre), row(g_ffn_post)

    def project(x3, time_minor, to_bf16=()):
        x2 = x3.reshape(-1, d)
        tm = _tile(x3.shape[1] if time_minor else x2.shape[0], INPROJ_TM)
        return x2, _inproj(x2, g_pre, w_qkvu, w_f, b_f2, tm=tm, seq=x3.shape[1], time_minor=time_minor,
                           to_bf16=to_bf16)

    def finish(x2, h, a_out, u3, hist, hist_map, nseg, tl, pos0, zero_first, ffn_w):
        tf = _tile(w_ff1.shape[1], FFN_TF)
        to_bf16 = () if ffn_w else ((w_ff1.astype(F32), tf), (w_ff2.astype(F32), None))
        x1, h2, *cast = _merge(a_out, h, u3, hist, hist_map, x2, w_up, w_ga, w_gb, w_pl, ps, w_o, g_post, g_fpre,
                               nseg=nseg, tl=tl, pos0=pos0, zero_first=zero_first, to_bf16=to_bf16)
        w1, w2 = ffn_w or cast
        return _ffn(h2, x1, w1, w2, g_fpost, tm=_tile(x2.shape[0], FFN_TM)), (w1, w2)

    merge_w = ((w_attn_up.astype(F32), None), (w_pool.astype(F32).reshape(p, -1), None), (w_out.astype(F32), None))
    x2, (h, q, kb, kt, vt, _, u, logft, w_up, w_pl, w_o) = project(xp, True, merge_w)
    w_pl = w_pl.reshape(w_pool.shape)
    c_rows = _cum_logf_rows(logft)
    nblk = nh // HEADS_PER_BLOCK
    three = lambda z: z.reshape(bp, tp, a)
    a_out = _attn_prompt(three(q), three(kb), vt, c_rows.reshape(bp, nblk, HEADS_PER_BLOCK, tp),
                         tq=_tile(tp, ATTN_TQ))
    to_bthd = lambda zt: jnp.transpose(zt.reshape(bp, nh, HEAD_DIM, tp), (0, 3, 1, 2))
    u3 = u.reshape(bp, tp, p)
    tl = _tile(tp, MERGE_TM)
    blocks_per_tile = tl // HIST_ROWS
    hist_map = lambda s, i: (s, jnp.maximum(i * blocks_per_tile - 1, 0), 0)
    yp, ffn_w = finish(x2, h, a_out.reshape(bp * tp, a), u3, u3, hist_map, 1, tl, 0, True, None)
    yp = yp.reshape(bp, tp, d)
    prompt_out = (yp, to_bthd(kt), to_bthd(vt), jnp.transpose(logft, (0, 2, 1)),
                  jnp.concatenate([jnp.zeros((bp, POOL_HIST, p), F32), u3], axis=1)[:, -POOL_HIST:])

    x2, (h, q, kb, k, v, vb, u, logf) = project(xs, False)
    logf3 = logf.reshape(bs, ts, nh)
    f_all = jnp.concatenate([cache_logf.astype(F32), logf3], axis=1)
    c_all = _cum_logf_rows(jnp.transpose(f_all, (0, 2, 1))).reshape(bs, nh, -1)
    c_cache = c_all[:, :, :past]
    c_new = c_all[:, :, past:past + ts]
    three = lambda z: z.reshape(bs, ts, a)
    a_out = _attn_sample(three(q), _time_minor(cache_k.astype(F32), bs), _time_minor(cache_v.astype(F32), bs),
                         three(kb), three(vb), c_cache, c_new, jnp.transpose(c_new, (0, 2, 1)))
    u3 = u.reshape(bs, ts, p)
    hist = jnp.pad(state_pool.astype(F32), ((0, 0), (HIST_ROWS - POOL_HIST, 0), (0, 0)))
    ys, _ = finish(x2, h, a_out.reshape(bs * ts, a), u3, hist, lambda s, i: (s, 0, 0), bs, ts, past, False, ffn_w)
    ys = ys.reshape(bs, ts, d)
    sample_out = (ys, k.reshape(bs, ts, nh, HEAD_DIM), v.reshape(bs, ts, nh, HEAD_DIM), logf3,
                  jnp.concatenate([state_pool.astype(F32), u3], axis=1)[:, -POOL_HIST:])
    return prompt_out, sample_out


def kernel(x_prompt, x_sample, cache_k, cache_v, cache_logf, state_pool, g_mix_pre, w_in, b_f, w_attn_up,
           w_pool, pool_scale, w_out, g_mix_post, g_ffn_pre, w_ff1, w_ff2, g_ffn_post):
    depth = w_in.shape[0]
    xp, xs = x_prompt, x_sample
    per_layer = []
    for l in range(depth):
        po, so = _layer(xp, xs, cache_k[l], cache_v[l], cache_logf[l], state_pool[l], g_mix_pre[l], w_in[l],
                        b_f[l], w_attn_up[l], w_pool[l], pool_scale[l], w_out[l], g_mix_post[l],
                        g_ffn_pre[l], w_ff1[l], w_ff2[l], g_ffn_post[l])
        xp, xs = po[0], so[0]
        per_layer.append(po[1:] + so[1:])
    stacked = [jnp.stack(leaves, 0) for leaves in zip(*per_layer)]
    return (xp, xs, *stacked)
```

```python
import functools
import math

import jax
import jax.numpy as jnp
from jax import lax
from jax.experimental import pallas as pl
from jax.experimental.pallas import tpu as pltpu

N_HEADS = 16
HEAD_DIM = 64
ATTN_W = N_HEADS * HEAD_DIM
POOL_WINDOWS = (2, 4, 8, 16)
POOL_GROUPS = len(POOL_WINDOWS)
POOL_HIST = max(POOL_WINDOWS) - 1
EPS = 1e-6
SCALE = HEAD_DIM ** -0.5

LANES = 128
VMEM_LIMIT_CAP = 60000 * 1024

HEADS_PER_BLOCK = LANES // HEAD_DIM
HIST_ROWS = 16

F32 = jnp.float32
BF16 = jnp.bfloat16


def _vmem_limit(nbytes):
    return int(min(VMEM_LIMIT_CAP, max(32 * 1024 * 1024, nbytes * 5 // 4)))


def _nbytes(shape, dtype):
    return math.prod(shape) * jnp.dtype(dtype).itemsize


def _rms(x):
    return x * lax.rsqrt(jnp.mean(x * x, axis=-1, keepdims=True) + EPS)


def _dot(a, b):
    return jnp.dot(a, b, preferred_element_type=F32)


def _dot_nt(a, b):
    return lax.dot_general(a, b, (((1,), (1,)), ((), ())), preferred_element_type=F32)


def _resident(shape):
    zeros = (0,) * len(shape)
    return pl.BlockSpec(shape, lambda *_: zeros, pipeline_mode=pl.Buffered(1))


def _repack_kernel(wt_ref, qkvu_ref, f_ref, ga_ref, gb_ref, *, off_f, off_u, off_ga, off_gb):
    nh = off_u - off_f
    p = off_ga - off_u
    d = off_gb - off_ga
    piece = lambda lo, hi: wt_ref[lo:hi, :].T.astype(BF16)
    qkvu_ref[:, 0:off_f] = piece(0, off_f)
    qkvu_ref[:, off_f:off_f + p] = piece(off_u, off_ga)
    fwin = wt_ref[off_f:off_f + LANES, :].T
    lane = lax.broadcasted_iota(jnp.int32, fwin.shape, 1)
    f_ref[...] = jnp.where(lane < nh, fwin, 0.0).astype(BF16)
    ga_ref[...] = piece(off_ga, off_gb)
    gb_ref[...] = piece(off_gb, off_gb + d)


def _repack_w_in(wt, *, off_f, off_u, off_ga, off_gb, tr):
    width, rows = wt.shape
    p = off_ga - off_u
    d = off_gb - off_ga
    blk = lambda cols: pl.BlockSpec((tr, cols), lambda i: (i, 0))
    est = 4 * _nbytes((width, tr), F32) + 2 * _nbytes((tr, off_f + p + LANES + 2 * d), BF16)
    return pl.pallas_call(
        functools.partial(_repack_kernel, off_f=off_f, off_u=off_u, off_ga=off_ga, off_gb=off_gb),
        grid=(rows // tr,),
        in_specs=[pl.BlockSpec((width, tr), lambda i: (0, i))],
        out_specs=(blk(off_f + p), blk(LANES), blk(d), blk(d)),
        out_shape=(jax.ShapeDtypeStruct((rows, off_f + p), BF16), jax.ShapeDtypeStruct((rows, LANES), BF16),
                   jax.ShapeDtypeStruct((rows, d), BF16), jax.ShapeDtypeStruct((rows, d), BF16)),
        compiler_params=pltpu.CompilerParams(
            dimension_semantics=("arbitrary",), vmem_limit_bytes=_vmem_limit(est)),
        name="repack_w_in",
    )(wt)


def _convert_slabs(refs, n_cast):
    for src, dst in zip(refs[:n_cast], refs[len(refs) - n_cast:]):
        if len(dst.shape) == 2:
            dst[...] = src[...].astype(dst.dtype)
        else:
            for c in range(dst.shape[0]):
                dst[c] = src[:, c * dst.shape[2]:(c + 1) * dst.shape[2]].astype(dst.dtype)
    return refs[n_cast:len(refs) - n_cast]


def _slab_specs(to_bf16, n_steps, index_map):
    in_specs, out_specs, shapes = [], [], []
    for w, chunk in to_bf16:
        rows, cols = w.shape
        slab = rows // n_steps
        in_specs.append(pl.BlockSpec((slab, cols), index_map))
        if chunk is None:
            out_specs.append(pl.BlockSpec((slab, cols), index_map))
            shapes.append(jax.ShapeDtypeStruct((rows, cols), BF16))
        else:
            out_specs.append(pl.BlockSpec((cols // chunk, slab, chunk), lambda *g: (0, index_map(*g)[0], 0)))
            shapes.append(jax.ShapeDtypeStruct((cols // chunk, rows, chunk), BF16))
    return in_specs, tuple(out_specs), tuple(shapes)


def _inproj_kernel(x_ref, g_ref, w_ref, wf_ref, bf_ref, *rest, time_minor, n_cast):
    h_ref, q_ref, kb_ref, k_ref, v_ref, vb_ref, u_ref, f_ref = _convert_slabs(rest, n_cast)
    h = (_rms(x_ref[...]) * g_ref[...]).astype(BF16)
    h_ref[...] = h
    a = q_ref.shape[1]
    f = _dot(h, wf_ref[...]) + bf_ref[...]
    k = _dot(h, w_ref[:, a:2 * a])
    kb_ref[...] = k.astype(BF16)
    v = _dot(h, w_ref[:, 2 * a:3 * a])
    vb_ref[...] = v.astype(BF16)
    q_ref[...] = (_dot(h, w_ref[:, 0:a]) * SCALE).astype(BF16)
    u_ref[...] = _dot(h, w_ref[:, 3 * a:])
    if time_minor:
        k_ref[0] = k.T
        v_ref[0] = v.T
        f_ref[0] = jax.nn.log_sigmoid(f.T[0:f_ref.shape[1], :])
    else:
        k_ref[...] = k
        v_ref[...] = v
        f_ref[...] = jax.nn.log_sigmoid(f[:, 0:f_ref.shape[1]])


def _inproj(x, g, w, wf, bf, *, tm, seq, time_minor, to_bf16=()):
    n, d = x.shape
    cast_in, cast_out, cast_shapes = _slab_specs(to_bf16, n // tm, lambda i: (i, 0))
    a = ATTN_W
    p = w.shape[1] - 3 * a
    nh = N_HEADS
    row = lambda width: pl.BlockSpec((tm, width), lambda i: (i, 0))
    if time_minor:
        steps = seq // tm
        feat = lambda rows: (jax.ShapeDtypeStruct((n // seq, rows, seq), F32),
                             pl.BlockSpec((1, rows, tm), lambda i: (i // steps, 0, i % steps)))
    else:
        feat = lambda rows: (jax.ShapeDtypeStruct((n, rows), F32), row(rows))
    (k_shape, k_spec), (f_shape, f_spec) = feat(a), feat(nh)
    out_shape = (
        jax.ShapeDtypeStruct((n, d), BF16),
        jax.ShapeDtypeStruct((n, a), BF16),
        jax.ShapeDtypeStruct((n, a), BF16),
        k_shape,
        k_shape,
        jax.ShapeDtypeStruct((n, a), BF16),
        jax.ShapeDtypeStruct((n, p), F32),
        f_shape,
    )
    est = (2 * _nbytes((tm, d), F32) + _nbytes(w.shape, BF16) + _nbytes(wf.shape, BF16)
           + 2 * (_nbytes((tm, d), BF16) + 3 * _nbytes((tm, a), BF16) + 2 * _nbytes((tm, a), F32)
                  + _nbytes((tm, p), F32) + _nbytes((tm, LANES), F32))
           + 4 * _nbytes((tm, a), F32) + sum(3 * _nbytes(c.shape, F32) * tm // n for c, _ in to_bf16))
    return pl.pallas_call(
        functools.partial(_inproj_kernel, time_minor=time_minor, n_cast=len(to_bf16)),
        grid=(n // tm,),
        in_specs=[row(d), _resident((1, d)), _resident(w.shape), _resident(wf.shape), _resident((1, LANES))]
                 + cast_in,
        out_specs=(row(d), row(a), row(a), k_spec, k_spec, row(a), row(p), f_spec) + cast_out,
        out_shape=out_shape + cast_shapes,
        compiler_params=pltpu.CompilerParams(
            dimension_semantics=("arbitrary",), vmem_limit_bytes=_vmem_limit(est)),
        name="inproj",
    )(x, g, w, wf, bf, *[c for c, _ in to_bf16])


def _cumsum_kernel(x_ref, o_ref):
    x = x_ref[...]
    length = x.shape[1]
    lane = lax.broadcasted_iota(jnp.int32, x.shape, 1)
    shift = 1
    while shift < length:
        x = x + jnp.where(lane >= shift, pltpu.roll(x, shift, axis=1), 0.0)
        shift *= 2
    o_ref[...] = x


def _cumsum_lanes(x):
    rows, length = x.shape
    tr = _tile(rows, CUMSUM_ROWS)
    spec = pl.BlockSpec((tr, length), lambda i: (i, 0))
    return pl.pallas_call(
        _cumsum_kernel,
        grid=(rows // tr,),
        in_specs=[spec],
        out_specs=spec,
        out_shape=jax.ShapeDtypeStruct(x.shape, F32),
        compiler_params=pltpu.CompilerParams(dimension_semantics=("arbitrary",)),
        name="cumsum",
    )(x)


VALUE_ROWS = HEAD_DIM + 16


def _attn_kernel(q_ref, k_ref, vt_ref, c_ref, o_ref,
                 vta_ref, ckrep_ref, s_ref, p_ref, alpha_ref, m_ref, acc_ref, *, tq):
    t = k_ref.shape[1]
    nq = t // tq
    for a in range(HEADS_PER_BLOCK):
        vta_ref[a, 0:HEAD_DIM, :] = vt_ref[0, a * HEAD_DIM:(a + 1) * HEAD_DIM, :].astype(BF16)
        vta_ref[a, HEAD_DIM:VALUE_ROWS, :] = jnp.ones((VALUE_ROWS - HEAD_DIM, t), BF16)
        ckrep_ref[a] = jnp.broadcast_to(c_ref[0, 0, a:a + 1, :], (LANES, t)).T

    pairs = [(i, j) for i in range(nq) for j in range(i + 1)]
    rows = lambda j: slice(j * tq, (j + 1) * tq)
    hq = tq // 2

    def scores(w):
        i, j = pairs[w]
        q2 = q_ref[0, rows(i), :]
        lane = lax.broadcasted_iota(jnp.int32, q2.shape, 1)
        kt = k_ref[0, rows(j), :]
        for a in range(HEADS_PER_BLOCK):
            qa = jnp.where((lane // HEAD_DIM) == a, q2, jnp.zeros_like(q2))
            ckr = ckrep_ref[a, rows(j), :]
            if j < i:
                s_ref[w % 2, a] = _dot_nt(kt, qa) - jnp.concatenate([ckr] * (tq // LANES), axis=1)
            else:
                s_ref[w % 2, a, 0:hq, 0:hq] = (_dot_nt(kt[0:hq], qa[0:hq])
                                               - jnp.concatenate([ckr[0:hq]] * (hq // LANES), axis=1))
                s_ref[w % 2, a, :, hq:tq] = _dot_nt(kt, qa[hq:tq]) - jnp.concatenate([ckr] * (hq // LANES), axis=1)

    def causal(s, first_query):
        r = lax.broadcasted_iota(jnp.int32, s.shape, 0)
        c = lax.broadcasted_iota(jnp.int32, s.shape, 1)
        return jnp.where(r <= c + first_query, s, -jnp.inf)

    def probs(w):
        i, j = pairs[w]
        for a in range(HEADS_PER_BLOCK):
            cqa = c_ref[0, 0, a:a + 1, rows(i)]
            if j < i:
                s = s_ref[w % 2, a]
                smax = jnp.max(s, axis=0, keepdims=True)
            else:
                s_lo = causal(s_ref[w % 2, a, 0:hq, 0:hq], 0)
                s_hi = causal(s_ref[w % 2, a, :, hq:tq], hq)
                smax = jnp.concatenate([jnp.max(s_lo, axis=0, keepdims=True),
                                        jnp.max(s_hi, axis=0, keepdims=True)], axis=1)
            smax = smax + cqa
            if j == 0:
                m_new = smax
            else:
                m_prev = m_ref[a]
                m_new = jnp.maximum(m_prev, smax)
                alpha_ref[w % 2, a] = jnp.exp(m_prev - m_new)
            shift = m_new - cqa
            if j < i:
                p_ref[w % 2, a] = jnp.exp(s - shift).astype(BF16)
            else:
                p_ref[w % 2, a, 0:hq, 0:hq] = jnp.exp(s_lo - shift[:, 0:hq]).astype(BF16)
                p_ref[w % 2, a, :, hq:tq] = jnp.exp(s_hi - shift[:, hq:tq]).astype(BF16)
            m_ref[a] = m_new

    def values(w):
        i, j = pairs[w]
        for a in range(HEADS_PER_BLOCK):
            if j < i:
                pv = _dot(vta_ref[a, :, rows(j)], p_ref[w % 2, a])
            else:
                pv = jnp.concatenate(
                    [_dot(vta_ref[a, :, j * tq:j * tq + hq], p_ref[w % 2, a, 0:hq, 0:hq]),
                     _dot(vta_ref[a, :, rows(j)], p_ref[w % 2, a, :, hq:tq])], axis=1)
            acc_ref[a] = pv if j == 0 else alpha_ref[w % 2, a] * acc_ref[a] + pv
        if j == i:
            ot = jnp.concatenate(
                [acc_ref[a, 0:HEAD_DIM, :] / acc_ref[a, HEAD_DIM:HEAD_DIM + 1, :]
                 for a in range(HEADS_PER_BLOCK)], axis=0)
            o_ref[0, rows(i), :] = ot.T.astype(o_ref.dtype)

    n = len(pairs)
    scores(0)
    for w in range(n):
        if w + 1 < n:
            scores(w + 1)
        probs(w)
        if w >= 1:
            values(w - 1)
    values(n - 1)


def _attn_prompt(q, kb, vt, c_rows, *, tq):
    b, t, a = q.shape
    nblk = a // LANES
    spec = pl.BlockSpec((1, t, LANES), lambda bi, hi: (bi, 0, hi))
    return pl.pallas_call(
        functools.partial(_attn_kernel, tq=tq),
        grid=(b, nblk),
        in_specs=[spec, spec, pl.BlockSpec((1, LANES, t), lambda bi, hi: (bi, hi, 0)),
                  pl.BlockSpec((1, 1, HEADS_PER_BLOCK, t), lambda bi, hi: (bi, hi, 0, 0))],
        out_specs=spec,
        out_shape=jax.ShapeDtypeStruct((b, t, a), BF16),
        scratch_shapes=[pltpu.VMEM((HEADS_PER_BLOCK, VALUE_ROWS, t), BF16),
                        pltpu.VMEM((HEADS_PER_BLOCK, t, LANES), F32),
                        pltpu.VMEM((2, HEADS_PER_BLOCK, tq, tq), F32),
                        pltpu.VMEM((2, HEADS_PER_BLOCK, tq, tq), BF16),
                        pltpu.VMEM((2, HEADS_PER_BLOCK, 1, tq), F32),
                        pltpu.VMEM((HEADS_PER_BLOCK, 1, tq), F32),
                        pltpu.VMEM((HEADS_PER_BLOCK, VALUE_ROWS, tq), F32)],
        compiler_params=pltpu.CompilerParams(dimension_semantics=("arbitrary", "arbitrary")),
        name="attn_prompt",
    )(q, kb, vt, c_rows)


def _attn_sample_kernel(q_ref, kct_ref, vct_ref, kn_ref, vn_ref, cc_ref, cnr_ref, cnc_ref, o_ref,
                        kb_ref, vb_ref, s_ref, p_ref, pn_ref):
    n, a = q_ref.shape[1], q_ref.shape[2]
    nh = a // HEAD_DIM
    q = q_ref[0]
    qt = jnp.concatenate([q] * nh, axis=0)
    row_h = lax.broadcasted_iota(jnp.int32, qt.shape, 0) // n
    col_h = lax.broadcasted_iota(jnp.int32, qt.shape, 1) // HEAD_DIM
    qbd = jnp.where(row_h == col_h, qt, jnp.zeros_like(qt))
    kb_ref[...] = kct_ref[0].astype(BF16)
    vb_ref[...] = vct_ref[0].astype(BF16)
    s_ref[...] = _dot(qbd, kb_ref[...])
    s_new = _dot_nt(qbd, kn_ref[0])
    cnc = cnc_ref[0]
    r = lax.broadcasted_iota(jnp.int32, (n, n), 0)
    c = lax.broadcasted_iota(jnp.int32, (n, n), 1)
    inv_l = []
    for h in range(nh):
        rows = slice(h * n, (h + 1) * n)
        cq = cnc[:, h:h + 1]
        sc = s_ref[rows, :] + (cq - cc_ref[0, h:h + 1, :])
        sn = s_new[rows, :] + (cq - cnr_ref[0, h:h + 1, :])
        sn = jnp.where(c <= r, sn, -jnp.inf)
        m = jnp.maximum(jnp.max(sc, axis=1, keepdims=True), jnp.max(sn, axis=1, keepdims=True))
        pc = jnp.exp(sc - m)
        pn = jnp.exp(sn - m)
        inv_l.append(1.0 / (jnp.sum(pc, axis=1, keepdims=True) + jnp.sum(pn, axis=1, keepdims=True)))
        p_ref[rows, :] = pc.astype(BF16)
        pn_ref[rows, :] = pn.astype(BF16)
    o = _dot_nt(p_ref[...], vb_ref[...]) + _dot(pn_ref[...], vn_ref[0])
    for h in range(nh):
        rows = slice(h * n, (h + 1) * n)
        cols = slice(h * HEAD_DIM, (h + 1) * HEAD_DIM)
        o_ref[0, :, cols] = (o[rows, cols] * inv_l[h]).astype(o_ref.dtype)


def _attn_sample(q, kct, vct, kn, vn, cc, cnr, cnc):
    b, n, a = q.shape
    past = kct.shape[2]
    nh = a // HEAD_DIM
    per_b = lambda *tail: pl.BlockSpec((1,) + tail, lambda bi: (bi,) + (0,) * len(tail))
    est = (4 * _nbytes((a, past), F32) + 2 * _nbytes((a, past), BF16)
           + _nbytes((nh * n, past), F32) + _nbytes((nh * n, past), BF16) + 4 * _nbytes((nh * n, a), F32))
    return pl.pallas_call(
        _attn_sample_kernel,
        grid=(b,),
        in_specs=[per_b(n, a), per_b(a, past), per_b(a, past), per_b(n, a), per_b(n, a),
                  per_b(nh, past), per_b(nh, n), per_b(n, nh)],
        out_specs=per_b(n, a),
        out_shape=jax.ShapeDtypeStruct((b, n, a), BF16),
        scratch_shapes=[pltpu.VMEM((a, past), BF16), pltpu.VMEM((a, past), BF16),
                        pltpu.VMEM((nh * n, past), F32), pltpu.VMEM((nh * n, past), BF16),
                        pltpu.VMEM((nh * n, n), BF16)],
        compiler_params=pltpu.CompilerParams(
            dimension_semantics=("arbitrary",), vmem_limit_bytes=_vmem_limit(est)),
        name="attn_sample",
    )(q, kct, vct, kn, vn, cc, cnr, cnc)


def _merge_kernel(a_ref, h_ref, u_ref, hist_ref, x_ref, wup_ref, wga_ref, wgb_ref, wpool_ref, ps_ref,
                  wout_ref, gpost_ref, gpre_ref, *rest, pos0, zero_first, n_cast):
    ext_ref, m_ref = rest[len(rest) - 2:]
    x1_ref, h2_ref = _convert_slabs(rest[:len(rest) - 2], n_cast)
    nseg, tl, _ = u_ref.shape
    i = pl.program_id(1)
    hist = hist_ref[...]
    if zero_first:
        hist = jnp.where(i == 0, 0.0, hist)
    ext_ref[:, 0:HIST_ROWS, :] = hist
    ext_ref[:, HIST_ROWS:HIST_ROWS + tl, :] = u_ref[...]
    gw = u_ref.shape[2] // POOL_GROUPS
    ogw = wpool_ref.shape[2]
    pos = pos0 + i * tl + lax.broadcasted_iota(jnp.int32, (nseg, tl, gw), 1)
    a = a_ref[...]
    h = h_ref[...]
    for g, w in enumerate(POOL_WINDOWS):
        os_ = slice(g * ogw, (g + 1) * ogw)
        br_a = _dot(a, wup_ref[:, os_])
        ga = _dot(h, wga_ref[:, os_])
        gb = _dot(h, wgb_ref[:, os_])
        cs = slice(g * gw, (g + 1) * gw)
        cur = ext_ref[:, HIST_ROWS:HIST_ROWS + tl, cs]
        tot = cur
        for s in range(1, w):
            tot = tot + ext_ref[:, HIST_ROWS - s:HIST_ROWS - s + tl, cs]
        cnt = jnp.minimum(pos + 1, w).astype(F32)
        pooled = (tot / cnt - cur).reshape(nseg * tl, gw).astype(BF16)
        br_b = _dot(pooled, wpool_ref[g]) * ps_ref[:, os_]
        m_ref[:, os_] = (jax.nn.sigmoid(ga) * br_a + jax.nn.sigmoid(gb) * br_b).astype(BF16)
    x1 = x_ref[...] + _rms(_dot(m_ref[...], wout_ref[...])) * gpost_ref[...]
    x1_ref[...] = x1
    h2_ref[...] = (_rms(x1) * gpre_ref[...]).astype(BF16)


def _merge(a, h, u3, hist, hist_map, x, wup, wga, wgb, wpool, ps, wout, gpost, gpre,
           *, nseg, tl, pos0, zero_first, to_bf16=()):
    s_total, l_total, p = u3.shape
    n, d = x.shape
    tm = nseg * tl
    steps = l_total // tl
    assert nseg == 1 or steps == 1
    row = lambda width: pl.BlockSpec((tm, width), lambda s, i: (s * steps + i, 0))
    n_steps = (s_total // nseg) * steps
    cast_in, cast_out, cast_shapes = _slab_specs(to_bf16, n_steps, lambda s, i: (s * steps + i, 0))
    est = (_nbytes(wup.shape, BF16) + 2 * _nbytes(wga.shape, BF16) + _nbytes(wpool.shape, BF16)
           + _nbytes(wout.shape, BF16)
           + 2 * (_nbytes((tm, a.shape[1]), BF16) + 2 * _nbytes((tm, d), BF16) + 2 * _nbytes((tm, d), F32)
                  + _nbytes((tm + HIST_ROWS, p), F32))
           + _nbytes((tm + nseg * HIST_ROWS, p), F32) + _nbytes((tm, d), BF16) + 6 * _nbytes((tm, d), F32)
           + sum(3 * _nbytes(w.shape, F32) // n_steps for w, _ in to_bf16))
    return pl.pallas_call(
        functools.partial(_merge_kernel, pos0=pos0, zero_first=zero_first, n_cast=len(to_bf16)),
        grid=(s_total // nseg, steps),
        in_specs=[row(a.shape[1]), row(d),
                  pl.BlockSpec((nseg, tl, p), lambda s, i: (s, i, 0)),
                  pl.BlockSpec((nseg, HIST_ROWS, p), hist_map),
                  row(d),
                  _resident(wup.shape), _resident(wga.shape), _resident(wgb.shape), _resident(wpool.shape),
                  _resident((1, d)), _resident(wout.shape), _resident((1, d)), _resident((1, d))]
                 + cast_in,
        out_specs=(row(d), row(d)) + cast_out,
        out_shape=(jax.ShapeDtypeStruct((n, d), F32), jax.ShapeDtypeStruct((n, d), BF16)) + cast_shapes,
        scratch_shapes=[pltpu.VMEM((nseg, HIST_ROWS + tl, p), F32), pltpu.VMEM((tm, d), BF16)],
        compiler_params=pltpu.CompilerParams(
            dimension_semantics=("arbitrary", "arbitrary"), vmem_limit_bytes=_vmem_limit(est)),
        name="merge",
    )(a, h, u3, hist, x, wup, wga, wgb, wpool, ps, wout, gpost, gpre, *[w for w, _ in to_bf16])


def _ffn_kernel(h_ref, x1_ref, w1_ref, w2_ref, g_ref, o_ref):
    f = pl.program_id(1)

    @pl.when(f == 0)
    def _():
        o_ref[...] = jnp.zeros_like(o_ref)

    z = jnp.square(jnp.maximum(_dot(h_ref[...], w1_ref[0]), 0.0)).astype(BF16)
    o_ref[...] += _dot(z, w2_ref[...])

    @pl.when(f == pl.num_programs(1) - 1)
    def _():
        o_ref[...] = x1_ref[...] + _rms(o_ref[...]) * g_ref[...]


def _ffn(h2, x1, w1, w2, g, *, tm):
    n, d = x1.shape
    tf = w1.shape[2]
    dff = w2.shape[0]
    est = (2 * _nbytes((tm, d), BF16) + 4 * _nbytes((tm, d), F32)
           + 4 * _nbytes((d, tf), BF16) + 2 * _nbytes((tm, tf), F32) + _nbytes((tm, d), F32))
    return pl.pallas_call(
        _ffn_kernel,
        grid=(n // tm, dff // tf),
        in_specs=[pl.BlockSpec((tm, d), lambda i, f: (i, 0)),
                  pl.BlockSpec((tm, d), lambda i, f: (i, 0)),
                  pl.BlockSpec((1, d, tf), lambda i, f: (f, 0, 0)),
                  pl.BlockSpec((tf, d), lambda i, f: (f, 0)),
                  pl.BlockSpec((1, d), lambda i, f: (0, 0))],
        out_specs=pl.BlockSpec((tm, d), lambda i, f: (i, 0)),
        out_shape=jax.ShapeDtypeStruct((n, d), F32),
        compiler_params=pltpu.CompilerParams(
            dimension_semantics=("arbitrary", "arbitrary"), vmem_limit_bytes=_vmem_limit(est)),
        name="ffn",
    )(h2, x1, w1, w2, g)


def _tile(n, pref):
    t = min(n, pref)
    while n % t:
        t //= 2
    return t


INPROJ_TM = 512
ATTN_TQ = 512
MERGE_TM = 256
FFN_TM = 512
FFN_TF = 1024
CUMSUM_ROWS = 32
REPACK_ROWS = 256


def _cum_logf_rows(logf_bht):
    b, h, t = logf_bht.shape
    rows = logf_bht.reshape(b * h, t)
    pad = (-t) % LANES
    if pad:
        rows = jnp.pad(rows, ((0, 0), (0, pad)))
    return _cumsum_lanes(rows)


def _time_minor(x, nbatch):
    return jnp.transpose(x, (0, 2, 3, 1)).reshape(nbatch, -1, x.shape[1])


def _layer(xp, xs, cache_k, cache_v, cache_logf, state_pool, g_mix_pre, w_in, b_f, w_attn_up, w_pool,
           pool_scale, w_out, g_mix_post, g_ffn_pre, w_ff1, w_ff2, g_ffn_post):
    bp, tp, d = xp.shape
    bs, ts, _ = xs.shape
    past = cache_k.shape[1]
    a, nh, p = ATTN_W, N_HEADS, w_pool.shape[0] * w_pool.shape[1]
    off_f = 3 * a
    off_u = off_f + nh
    off_ga = off_u + p
    off_gb = off_ga + d

    w_qkvu, w_f, w_ga, w_gb = _repack_w_in(jnp.transpose(w_in.astype(F32)), off_f=off_f, off_u=off_u,
                                           off_ga=off_ga, off_gb=off_gb, tr=_tile(w_in.shape[0], REPACK_ROWS))
    b_f2 = jnp.pad(b_f.reshape(1, -1), ((0, 0), (0, LANES - nh)))
    row = lambda v: v.reshape(1, -1)
    g_pre, ps = row(g_mix_pre), row(pool_scale)
    g_post, g_fpre, g_fpost = row(g_mix_post), row(g_ffn_pre), row(g_ffn_post)

    def project(x3, time_minor, to_bf16=()):
        x2 = x3.reshape(-1, d)
        tm = _tile(x3.shape[1] if time_minor else x2.shape[0], INPROJ_TM)
        return x2, _inproj(x2, g_pre, w_qkvu, w_f, b_f2, tm=tm, seq=x3.shape[1], time_minor=time_minor,
                           to_bf16=to_bf16)

    def finish(x2, h, a_out, u3, hist, hist_map, nseg, tl, pos0, zero_first, ffn_w):
        tf = _tile(w_ff1.shape[1], FFN_TF)
        to_bf16 = () if ffn_w else ((w_ff1.astype(F32), tf), (w_ff2.astype(F32), None))
        x1, h2, *cast = _merge(a_out, h, u3, hist, hist_map, x2, w_up, w_ga, w_gb, w_pl, ps, w_o, g_post, g_fpre,
                               nseg=nseg, tl=tl, pos0=pos0, zero_first=zero_first, to_bf16=to_bf16)
        w1, w2 = ffn_w or cast
        return _ffn(h2, x1, w1, w2, g_fpost, tm=_tile(x2.shape[0], FFN_TM)), (w1, w2)

    merge_w = ((w_attn_up.astype(F32), None), (w_pool.astype(F32).reshape(p, -1), None), (w_out.astype(F32), None))
    x2, (h, q, kb, kt, vt, _, u, logft, w_up, w_pl, w_o) = project(xp, True, merge_w)
    w_pl = w_pl.reshape(w_pool.shape)
    c_rows = _cum_logf_rows(logft)
    nblk = nh // HEADS_PER_BLOCK
    three = lambda z: z.reshape(bp, tp, a)
    a_out = _attn_prompt(three(q), three(kb), vt, c_rows.reshape(bp, nblk, HEADS_PER_BLOCK, tp),
                         tq=_tile(tp, ATTN_TQ))
    to_bthd = lambda zt: jnp.transpose(zt.reshape(bp, nh, HEAD_DIM, tp), (0, 3, 1, 2))
    u3 = u.reshape(bp, tp, p)
    tl = _tile(tp, MERGE_TM)
    blocks_per_tile = tl // HIST_ROWS
    hist_map = lambda s, i: (s, jnp.maximum(i * blocks_per_tile - 1, 0), 0)
    yp, ffn_w = finish(x2, h, a_out.reshape(bp * tp, a), u3, u3, hist_map, 1, tl, 0, True, None)
    yp = yp.reshape(bp, tp, d)
    prompt_out = (yp, to_bthd(kt), to_bthd(vt), jnp.transpose(logft, (0, 2, 1)),
                  jnp.concatenate([jnp.zeros((bp, POOL_HIST, p), F32), u3], axis=1)[:, -POOL_HIST:])

    x2, (h, q, kb, k, v, vb, u, logf) = project(xs, False)
    logf3 = logf.reshape(bs, ts, nh)
    f_all = jnp.concatenate([cache_logf.astype(F32), logf3], axis=1)
    c_all = _cum_logf_rows(jnp.transpose(f_all, (0, 2, 1))).reshape(bs, nh, -1)
    c_cache = c_all[:, :, :past]
    c_new = c_all[:, :, past:past + ts]
    three = lambda z: z.reshape(bs, ts, a)
    a_out = _attn_sample(three(q), _time_minor(cache_k.astype(F32), bs), _time_minor(cache_v.astype(F32), bs),
                         three(kb), three(vb), c_cache, c_new, jnp.transpose(c_new, (0, 2, 1)))
    u3 = u.reshape(bs, ts, p)
    hist = jnp.pad(state_pool.astype(F32), ((0, 0), (HIST_ROWS - POOL_HIST, 0), (0, 0)))
    ys, _ = finish(x2, h, a_out.reshape(bs * ts, a), u3, hist, lambda s, i: (s, 0, 0), bs, ts, past, False, ffn_w)
    ys = ys.reshape(bs, ts, d)
    sample_out = (ys, k.reshape(bs, ts, nh, HEAD_DIM), v.reshape(bs, ts, nh, HEAD_DIM), logf3,
                  jnp.concatenate([state_pool.astype(F32), u3], axis=1)[:, -POOL_HIST:])
    return prompt_out, sample_out


def kernel(x_prompt, x_sample, cache_k, cache_v, cache_logf, state_pool, g_mix_pre, w_in, b_f, w_attn_up,
           w_pool, pool_scale, w_out, g_mix_post, g_ffn_pre, w_ff1, w_ff2, g_ffn_post):
    depth = w_in.shape[0]
    xp, xs = x_prompt, x_sample
    per_layer = []
    for l in range(depth):
        po, so = _layer(xp, xs, cache_k[l], cache_v[l], cache_logf[l], state_pool[l], g_mix_pre[l], w_in[l],
                        b_f[l], w_attn_up[l], w_pool[l], pool_scale[l], w_out[l], g_mix_post[l],
                        g_ffn_pre[l], w_ff1[l], w_ff2[l], g_ffn_post[l])
        xp, xs = po[0], so[0]
        per_layer.append(po[1:] + so[1:])
    stacked = [jnp.stack(leaves, 0) for leaves in zip(*per_layer)]
    return (xp, xs, *stacked)
```

```python
import functools
import math

import jax
import jax.numpy as jnp
from jax import lax
from jax.experimental import pallas as pl
from jax.experimental.pallas import tpu as pltpu

N_HEADS = 16
HEAD_DIM = 64
ATTN_W = N_HEADS * HEAD_DIM
POOL_WINDOWS = (2, 4, 8, 16)
POOL_GROUPS = len(POOL_WINDOWS)
POOL_HIST = max(POOL_WINDOWS) - 1
EPS = 1e-6
SCALE = HEAD_DIM ** -0.5

LANES = 128
VMEM_LIMIT_CAP = 60000 * 1024

HEADS_PER_BLOCK = LANES // HEAD_DIM
HIST_ROWS = 16

F32 = jnp.float32
BF16 = jnp.bfloat16


def _vmem_limit(nbytes):
    return int(min(VMEM_LIMIT_CAP, max(32 * 1024 * 1024, nbytes * 5 // 4)))


def _nbytes(shape, dtype):
    return math.prod(shape) * jnp.dtype(dtype).itemsize


def _rms(x):
    return x * lax.rsqrt(jnp.mean(x * x, axis=-1, keepdims=True) + EPS)


def _dot(a, b):
    return jnp.dot(a, b, preferred_element_type=F32)


def _dot_nt(a, b):
    return lax.dot_general(a, b, (((1,), (1,)), ((), ())), preferred_element_type=F32)


def _resident(shape):
    zeros = (0,) * len(shape)
    return pl.BlockSpec(shape, lambda *_: zeros, pipeline_mode=pl.Buffered(1))


def _repack_kernel(wt_ref, qkvu_ref, f_ref, ga_ref, gb_ref, *, off_f, off_u, off_ga, off_gb):
    nh = off_u - off_f
    p = off_ga - off_u
    d = off_gb - off_ga
    piece = lambda lo, hi: wt_ref[lo:hi, :].T.astype(BF16)
    qkvu_ref[:, 0:off_f] = piece(0, off_f)
    qkvu_ref[:, off_f:off_f + p] = piece(off_u, off_ga)
    fwin = wt_ref[off_f:off_f + LANES, :].T
    lane = lax.broadcasted_iota(jnp.int32, fwin.shape, 1)
    f_ref[...] = jnp.where(lane < nh, fwin, 0.0).astype(BF16)
    ga_ref[...] = piece(off_ga, off_gb)
    gb_ref[...] = piece(off_gb, off_gb + d)


def _repack_w_in(wt, *, off_f, off_u, off_ga, off_gb, tr):
    width, rows = wt.shape
    p = off_ga - off_u
    d = off_gb - off_ga
    blk = lambda cols: pl.BlockSpec((tr, cols), lambda i: (i, 0))
    est = 4 * _nbytes((width, tr), F32) + 2 * _nbytes((tr, off_f + p + LANES + 2 * d), BF16)
    return pl.pallas_call(
        functools.partial(_repack_kernel, off_f=off_f, off_u=off_u, off_ga=off_ga, off_gb=off_gb),
        grid=(rows // tr,),
        in_specs=[pl.BlockSpec((width, tr), lambda i: (0, i))],
        out_specs=(blk(off_f + p), blk(LANES), blk(d), blk(d)),
        out_shape=(jax.ShapeDtypeStruct((rows, off_f + p), BF16), jax.ShapeDtypeStruct((rows, LANES), BF16),
                   jax.ShapeDtypeStruct((rows, d), BF16), jax.ShapeDtypeStruct((rows, d), BF16)),
        compiler_params=pltpu.CompilerParams(
            dimension_semantics=("arbitrary",), vmem_limit_bytes=_vmem_limit(est)),
        name="repack_w_in",
    )(wt)


def _convert_slabs(refs, n_cast):
    for src, dst in zip(refs[:n_cast], refs[len(refs) - n_cast:]):
        if len(dst.shape) == 2:
            dst[...] = src[...].astype(dst.dtype)
        else:
            for c in range(dst.shape[0]):
                dst[c] = src[:, c * dst.shape[2]:(c + 1) * dst.shape[2]].astype(dst.dtype)
    return refs[n_cast:len(refs) - n_cast]


def _slab_specs(to_bf16, n_steps, index_map):
    in_specs, out_specs, shapes = [], [], []
    for w, chunk in to_bf16:
        rows, cols = w.shape
        slab = rows // n_steps
        in_specs.append(pl.BlockSpec((slab, cols), index_map))
        if chunk is None:
            out_specs.append(pl.BlockSpec((slab, cols), index_map))
            shapes.append(jax.ShapeDtypeStruct((rows, cols), BF16))
        else:
            out_specs.append(pl.BlockSpec((cols // chunk, slab, chunk), lambda *g: (0, index_map(*g)[0], 0)))
            shapes.append(jax.ShapeDtypeStruct((cols // chunk, rows, chunk), BF16))
    return in_specs, tuple(out_specs), tuple(shapes)


def _inproj_kernel(x_ref, g_ref, w_ref, wf_ref, bf_ref, *rest, time_minor, n_cast):
    h_ref, q_ref, kb_ref, k_ref, v_ref, vb_ref, u_ref, f_ref = _convert_slabs(rest, n_cast)
    h = (_rms(x_ref[...]) * g_ref[...]).astype(BF16)
    h_ref[...] = h
    a = q_ref.shape[1]
    f = _dot(h, wf_ref[...]) + bf_ref[...]
    k = _dot(h, w_ref[:, a:2 * a])
    kb_ref[...] = k.astype(BF16)
    v = _dot(h, w_ref[:, 2 * a:3 * a])
    vb_ref[...] = v.astype(BF16)
    q_ref[...] = (_dot(h, w_ref[:, 0:a]) * SCALE).astype(BF16)
    u_ref[...] = _dot(h, w_ref[:, 3 * a:])
    if time_minor:
        k_ref[0] = k.T
        v_ref[0] = v.T
        f_ref[0] = jax.nn.log_sigmoid(f.T[0:f_ref.shape[1], :])
    else:
        k_ref[...] = k
        v_ref[...] = v
        f_ref[...] = jax.nn.log_sigmoid(f[:, 0:f_ref.shape[1]])


def _inproj(x, g, w, wf, bf, *, tm, seq, time_minor, to_bf16=()):
    n, d = x.shape
    cast_in, cast_out, cast_shapes = _slab_specs(to_bf16, n // tm, lambda i: (i, 0))
    a = ATTN_W
    p = w.shape[1] - 3 * a
    nh = N_HEADS
    row = lambda width: pl.BlockSpec((tm, width), lambda i: (i, 0))
    if time_minor:
        steps = seq // tm
        feat = lambda rows: (jax.ShapeDtypeStruct((n // seq, rows, seq), F32),
                             pl.BlockSpec((1, rows, tm), lambda i: (i // steps, 0, i % steps)))
    else:
        feat = lambda rows: (jax.ShapeDtypeStruct((n, rows), F32), row(rows))
    (k_shape, k_spec), (f_shape, f_spec) = feat(a), feat(nh)
    out_shape = (
        jax.ShapeDtypeStruct((n, d), BF16),
        jax.ShapeDtypeStruct((n, a), BF16),
        jax.ShapeDtypeStruct((n, a), BF16),
        k_shape,
        k_shape,
        jax.ShapeDtypeStruct((n, a), BF16),
        jax.ShapeDtypeStruct((n, p), F32),
        f_shape,
    )
    est = (2 * _nbytes((tm, d), F32) + _nbytes(w.shape, BF16) + _nbytes(wf.shape, BF16)
           + 2 * (_nbytes((tm, d), BF16) + 3 * _nbytes((tm, a), BF16) + 2 * _nbytes((tm, a), F32)
                  + _nbytes((tm, p), F32) + _nbytes((tm, LANES), F32))
           + 4 * _nbytes((tm, a), F32) + sum(3 * _nbytes(c.shape, F32) * tm // n for c, _ in to_bf16))
    return pl.pallas_call(
        functools.partial(_inproj_kernel, time_minor=time_minor, n_cast=len(to_bf16)),
        grid=(n // tm,),
        in_specs=[row(d), _resident((1, d)), _resident(w.shape), _resident(wf.shape), _resident((1, LANES))]
                 + cast_in,
        out_specs=(row(d), row(a), row(a), k_spec, k_spec, row(a), row(p), f_spec) + cast_out,
        out_shape=out_shape + cast_shapes,
        compiler_params=pltpu.CompilerParams(
            dimension_semantics=("arbitrary",), vmem_limit_bytes=_vmem_limit(est)),
        name="inproj",
    )(x, g, w, wf, bf, *[c for c, _ in to_bf16])


def _cumsum_kernel(x_ref, o_ref):
    x = x_ref[...]
    length = x.shape[1]
    lane = lax.broadcasted_iota(jnp.int32, x.shape, 1)
    shift = 1
    while shift < length:
        x = x + jnp.where(lane >= shift, pltpu.roll(x, shift, axis=1), 0.0)
        shift *= 2
    o_ref[...] = x


def _cumsum_lanes(x):
    rows, length = x.shape
    tr = _tile(rows, CUMSUM_ROWS)
    spec = pl.BlockSpec((tr, length), lambda i: (i, 0))
    return pl.pallas_call(
        _cumsum_kernel,
        grid=(rows // tr,),
        in_specs=[spec],
        out_specs=spec,
        out_shape=jax.ShapeDtypeStruct(x.shape, F32),
        compiler_params=pltpu.CompilerParams(dimension_semantics=("arbitrary",)),
        name="cumsum",
    )(x)


VALUE_ROWS = HEAD_DIM + 16


def _attn_kernel(q_ref, k_ref, vt_ref, c_ref, o_ref,
                 vta_ref, ckrep_ref, s_ref, p_ref, alpha_ref, m_ref, acc_ref, *, tq):
    t = k_ref.shape[1]
    nq = t // tq
    for a in range(HEADS_PER_BLOCK):
        vta_ref[a, 0:HEAD_DIM, :] = vt_ref[0, a * HEAD_DIM:(a + 1) * HEAD_DIM, :].astype(BF16)
        vta_ref[a, HEAD_DIM:VALUE_ROWS, :] = jnp.ones((VALUE_ROWS - HEAD_DIM, t), BF16)
        ckrep_ref[a] = jnp.broadcast_to(c_ref[0, 0, a:a + 1, :], (LANES, t)).T

    pairs = [(i, j) for i in range(nq) for j in range(i + 1)]
    rows = lambda j: slice(j * tq, (j + 1) * tq)
    hq = tq // 2

    def scores(w):
        i, j = pairs[w]
        q2 = q_ref[0, rows(i), :]
        lane = lax.broadcasted_iota(jnp.int32, q2.shape, 1)
        kt = k_ref[0, rows(j), :]
        for a in range(HEADS_PER_BLOCK):
            qa = jnp.where((lane // HEAD_DIM) == a, q2, jnp.zeros_like(q2))
            ckr = ckrep_ref[a, rows(j), :]
            if j < i:
                s_ref[w % 2, a] = _dot_nt(kt, qa) - jnp.concatenate([ckr] * (tq // LANES), axis=1)
            else:
                s_ref[w % 2, a, 0:hq, 0:hq] = (_dot_nt(kt[0:hq], qa[0:hq])
                                               - jnp.concatenate([ckr[0:hq]] * (hq // LANES), axis=1))
                s_ref[w % 2, a, :, hq:tq] = _dot_nt(kt, qa[hq:tq]) - jnp.concatenate([ckr] * (hq // LANES), axis=1)

    def causal(s, first_query):
        r = lax.broadcasted_iota(jnp.int32, s.shape, 0)
        c = lax.broadcasted_iota(jnp.int32, s.shape, 1)
        return jnp.where(r <= c + first_query, s, -jnp.inf)

    def probs(w):
        i, j = pairs[w]
        for a in range(HEADS_PER_BLOCK):
            cqa = c_ref[0, 0, a:a + 1, rows(i)]
            if j < i:
                s = s_ref[w % 2, a]
                smax = jnp.max(s, axis=0, keepdims=True)
            else:
                s_lo = causal(s_ref[w % 2, a, 0:hq, 0:hq], 0)
                s_hi = causal(s_ref[w % 2, a, :, hq:tq], hq)
                smax = jnp.concatenate([jnp.max(s_lo, axis=0, keepdims=True),
                                        jnp.max(s_hi, axis=0, keepdims=True)], axis=1)
            smax = smax + cqa
            if j == 0:
                m_new = smax
            else:
                m_prev = m_ref[a]
                m_new = jnp.maximum(m_prev, smax)
                alpha_ref[w % 2, a] = jnp.exp(m_prev - m_new)
            shift = m_new - cqa
            if j < i:
                p_ref[w % 2, a] = jnp.exp(s - shift).astype(BF16)
            else:
                p_ref[w % 2, a, 0:hq, 0:hq] = jnp.exp(s_lo - shift[:, 0:hq]).astype(BF16)
                p_ref[w % 2, a, :, hq:tq] = jnp.exp(s_hi - shift[:, hq:tq]).astype(BF16)
            m_ref[a] = m_new

    def values(w):
        i, j = pairs[w]
        for a in range(HEADS_PER_BLOCK):
            if j < i:
                pv = _dot(vta_ref[a, :, rows(j)], p_ref[w % 2, a])
            else:
                pv = jnp.concatenate(
                    [_dot(vta_ref[a, :, j * tq:j * tq + hq], p_ref[w % 2, a, 0:hq, 0:hq]),
                     _dot(vta_ref[a, :, rows(j)], p_ref[w % 2, a, :, hq:tq])], axis=1)
            acc_ref[a] = pv if j == 0 else alpha_ref[w % 2, a] * acc_ref[a] + pv
        if j == i:
            ot = jnp.concatenate(
                [acc_ref[a, 0:HEAD_DIM, :] / acc_ref[a, HEAD_DIM:HEAD_DIM + 1, :]
                 for a in range(HEADS_PER_BLOCK)], axis=0)
            o_ref[0, rows(i), :] = ot.T.astype(o_ref.dtype)

    n = len(pairs)
    scores(0)
    for w in range(n):
        if w + 1 < n:
            scores(w + 1)
        probs(w)
        if w >= 1:
            values(w - 1)
    values(n - 1)


def _attn_prompt(q, kb, vt, c_rows, *, tq):
    b, t, a = q.shape
    nblk = a // LANES
    spec = pl.BlockSpec((1, t, LANES), lambda bi, hi: (bi, 0, hi))
    return pl.pallas_call(
        functools.partial(_attn_kernel, tq=tq),
        grid=(b, nblk),
        in_specs=[spec, spec, pl.BlockSpec((1, LANES, t), lambda bi, hi: (bi, hi, 0)),
                  pl.BlockSpec((1, 1, HEADS_PER_BLOCK, t), lambda bi, hi: (bi, hi, 0, 0))],
        out_specs=spec,
        out_shape=jax.ShapeDtypeStruct((b, t, a), BF16),
        scratch_shapes=[pltpu.VMEM((HEADS_PER_BLOCK, VALUE_ROWS, t), BF16),
                        pltpu.VMEM((HEADS_PER_BLOCK, t, LANES), F32),
                        pltpu.VMEM((2, HEADS_PER_BLOCK, tq, tq), F32),
                        pltpu.VMEM((2, HEADS_PER_BLOCK, tq, tq), BF16),
                        pltpu.VMEM((2, HEADS_PER_BLOCK, 1, tq), F32),
                        pltpu.VMEM((HEADS_PER_BLOCK, 1, tq), F32),
                        pltpu.VMEM((HEADS_PER_BLOCK, VALUE_ROWS, tq), F32)],
        compiler_params=pltpu.CompilerParams(dimension_semantics=("arbitrary", "arbitrary")),
        name="attn_prompt",
    )(q, kb, vt, c_rows)


def _attn_sample_kernel(q_ref, kct_ref, vct_ref, kn_ref, vn_ref, cc_ref, cnr_ref, cnc_ref, o_ref,
                        kb_ref, vb_ref, s_ref, p_ref, pn_ref):
    n, a = q_ref.shape[1], q_ref.shape[2]
    nh = a // HEAD_DIM
    q = q_ref[0]
    qt = jnp.concatenate([q] * nh, axis=0)
    row_h = lax.broadcasted_iota(jnp.int32, qt.shape, 0) // n
    col_h = lax.broadcasted_iota(jnp.int32, qt.shape, 1) // HEAD_DIM
    qbd = jnp.where(row_h == col_h, qt, jnp.zeros_like(qt))
    kb_ref[...] = kct_ref[0].astype(BF16)
    vb_ref[...] = vct_ref[0].astype(BF16)
    s_ref[...] = _dot(qbd, kb_ref[...])
    s_new = _dot_nt(qbd, kn_ref[0])
    cnc = cnc_ref[0]
    r = lax.broadcasted_iota(jnp.int32, (n, n), 0)
    c = lax.broadcasted_iota(jnp.int32, (n, n), 1)
    inv_l = []
    for h in range(nh):
        rows = slice(h * n, (h + 1) * n)
        cq = cnc[:, h:h + 1]
        sc = s_ref[rows, :] + (cq - cc_ref[0, h:h + 1, :])
        sn = s_new[rows, :] + (cq - cnr_ref[0, h:h + 1, :])
        sn = jnp.where(c <= r, sn, -jnp.inf)
        m = jnp.maximum(jnp.max(sc, axis=1, keepdims=True), jnp.max(sn, axis=1, keepdims=True))
        pc = jnp.exp(sc - m)
        pn = jnp.exp(sn - m)
        inv_l.append(1.0 / (jnp.sum(pc, axis=1, keepdims=True) + jnp.sum(pn, axis=1, keepdims=True)))
        p_ref[rows, :] = pc.astype(BF16)
        pn_ref[rows, :] = pn.astype(BF16)
    o = _dot_nt(p_ref[...], vb_ref[...]) + _dot(pn_ref[...], vn_ref[0])
    for h in range(nh):
        rows = slice(h * n, (h + 1) * n)
        cols = slice(h * HEAD_DIM, (h + 1) * HEAD_DIM)
        o_ref[0, :, cols] = (o[rows, cols] * inv_l[h]).astype(o_ref.dtype)


def _attn_sample(q, kct, vct, kn, vn, cc, cnr, cnc):
    b, n, a = q.shape
    past = kct.shape[2]
    nh = a // HEAD_DIM
    per_b = lambda *tail: pl.BlockSpec((1,) + tail, lambda bi: (bi,) + (0,) * len(tail))
    est = (4 * _nbytes((a, past), F32) + 2 * _nbytes((a, past), BF16)
           + _nbytes((nh * n, past), F32) + _nbytes((nh * n, past), BF16) + 4 * _nbytes((nh * n, a), F32))
    return pl.pallas_call(
        _attn_sample_kernel,
        grid=(b,),
        in_specs=[per_b(n, a), per_b(a, past), per_b(a, past), per_b(n, a), per_b(n, a),
                  per_b(nh, past), per_b(nh, n), per_b(n, nh)],
        out_specs=per_b(n, a),
        out_shape=jax.ShapeDtypeStruct((b, n, a), BF16),
        scratch_shapes=[pltpu.VMEM((a, past), BF16), pltpu.VMEM((a, past), BF16),
                        pltpu.VMEM((nh * n, past), F32), pltpu.VMEM((nh * n, past), BF16),
                        pltpu.VMEM((nh * n, n), BF16)],
        compiler_params=pltpu.CompilerParams(
            dimension_semantics=("arbitrary",), vmem_limit_bytes=_vmem_limit(est)),
        name="attn_sample",
    )(q, kct, vct, kn, vn, cc, cnr, cnc)


def _merge_kernel(a_ref, h_ref, u_ref, hist_ref, x_ref, wup_ref, wga_ref, wgb_ref, wpool_ref, ps_ref,
                  wout_ref, gpost_ref, gpre_ref, *rest, pos0, zero_first, n_cast):
    ext_ref, m_ref = rest[len(rest) - 2:]
    x1_ref, h2_ref = _convert_slabs(rest[:len(rest) - 2], n_cast)
    nseg, tl, _ = u_ref.shape
    i = pl.program_id(1)
    hist = hist_ref[...]
    if zero_first:
        hist = jnp.where(i == 0, 0.0, hist)
    ext_ref[:, 0:HIST_ROWS, :] = hist
    ext_ref[:, HIST_ROWS:HIST_ROWS + tl, :] = u_ref[...]
    gw = u_ref.shape[2] // POOL_GROUPS
    ogw = wpool_ref.shape[2]
    pos = pos0 + i * tl + lax.broadcasted_iota(jnp.int32, (nseg, tl, gw), 1)
    a = a_ref[...]
    h = h_ref[...]
    for g, w in enumerate(POOL_WINDOWS):
        os_ = slice(g * ogw, (g + 1) * ogw)
        br_a = _dot(a, wup_ref[:, os_])
        ga = _dot(h, wga_ref[:, os_])
        gb = _dot(h, wgb_ref[:, os_])
        cs = slice(g * gw, (g + 1) * gw)
        cur = ext_ref[:, HIST_ROWS:HIST_ROWS + tl, cs]
        tot = ext_ref[:, HIST_ROWS - (w - 1):HIST_ROWS + tl, cs]
        s = 1
        while s < w:
            tot = tot[:, s:, :] + tot[:, :tot.shape[1] - s, :]
            s *= 2
        cnt = jnp.minimum(pos + 1, w).astype(F32)
        pooled = (tot / cnt - cur).reshape(nseg * tl, gw).astype(BF16)
        br_b = _dot(pooled, wpool_ref[g]) * ps_ref[:, os_]
        m_ref[:, os_] = (jax.nn.sigmoid(ga) * br_a + jax.nn.sigmoid(gb) * br_b).astype(BF16)
    x1 = x_ref[...] + _rms(_dot(m_ref[...], wout_ref[...])) * gpost_ref[...]
    x1_ref[...] = x1
    h2_ref[...] = (_rms(x1) * gpre_ref[...]).astype(BF16)


def _merge(a, h, u3, hist, hist_map, x, wup, wga, wgb, wpool, ps, wout, gpost, gpre,
           *, nseg, tl, pos0, zero_first, to_bf16=()):
    s_total, l_total, p = u3.shape
    n, d = x.shape
    tm = nseg * tl
    steps = l_total // tl
    assert nseg == 1 or steps == 1
    row = lambda width: pl.BlockSpec((tm, width), lambda s, i: (s * steps + i, 0))
    n_steps = (s_total // nseg) * steps
    cast_in, cast_out, cast_shapes = _slab_specs(to_bf16, n_steps, lambda s, i: (s * steps + i, 0))
    est = (_nbytes(wup.shape, BF16) + 2 * _nbytes(wga.shape, BF16) + _nbytes(wpool.shape, BF16)
           + _nbytes(wout.shape, BF16)
           + 2 * (_nbytes((tm, a.shape[1]), BF16) + 2 * _nbytes((tm, d), BF16) + 2 * _nbytes((tm, d), F32)
                  + _nbytes((tm + HIST_ROWS, p), F32))
           + _nbytes((tm + nseg * HIST_ROWS, p), F32) + _nbytes((tm, d), BF16) + 6 * _nbytes((tm, d), F32)
           + sum(3 * _nbytes(w.shape, F32) // n_steps for w, _ in to_bf16))
    return pl.pallas_call(
        functools.partial(_merge_kernel, pos0=pos0, zero_first=zero_first, n_cast=len(to_bf16)),
        grid=(s_total // nseg, steps),
        in_specs=[row(a.shape[1]), row(d),
                  pl.BlockSpec((nseg, tl, p), lambda s, i: (s, i, 0)),
                  pl.BlockSpec((nseg, HIST_ROWS, p), hist_map),
                  row(d),
                  _resident(wup.shape), _resident(wga.shape), _resident(wgb.shape), _resident(wpool.shape),
                  _resident((1, d)), _resident(wout.shape), _resident((1, d)), _resident((1, d))]
                 + cast_in,
        out_specs=(row(d), row(d)) + cast_out,
        out_shape=(jax.ShapeDtypeStruct((n, d), F32), jax.ShapeDtypeStruct((n, d), BF16)) + cast_shapes,
        scratch_shapes=[pltpu.VMEM((nseg, HIST_ROWS + tl, p), F32), pltpu.VMEM((tm, d), BF16)],
        compiler_params=pltpu.CompilerParams(
            dimension_semantics=("arbitrary", "arbitrary"), vmem_limit_bytes=_vmem_limit(est)),
        name="merge",
    )(a, h, u3, hist, x, wup, wga, wgb, wpool, ps, wout, gpost, gpre, *[w for w, _ in to_bf16])


def _ffn_kernel(h_ref, x1_ref, w1_ref, w2_ref, g_ref, o_ref):
    f = pl.program_id(1)

    @pl.when(f == 0)
    def _():
        o_ref[...] = jnp.zeros_like(o_ref)

    z = jnp.square(jnp.maximum(_dot(h_ref[...], w1_ref[0]), 0.0)).astype(BF16)
    o_ref[...] += _dot(z, w2_ref[...])

    @pl.when(f == pl.num_programs(1) - 1)
    def _():
        o_ref[...] = x1_ref[...] + _rms(o_ref[...]) * g_ref[...]


def _ffn(h2, x1, w1, w2, g, *, tm):
    n, d = x1.shape
    tf = w1.shape[2]
    dff = w2.shape[0]
    est = (2 * _nbytes((tm, d), BF16) + 4 * _nbytes((tm, d), F32)
           + 4 * _nbytes((d, tf), BF16) + 2 * _nbytes((tm, tf), F32) + _nbytes((tm, d), F32))
    return pl.pallas_call(
        _ffn_kernel,
        grid=(n // tm, dff // tf),
        in_specs=[pl.BlockSpec((tm, d), lambda i, f: (i, 0)),
                  pl.BlockSpec((tm, d), lambda i, f: (i, 0)),
                  pl.BlockSpec((1, d, tf), lambda i, f: (f, 0, 0)),
                  pl.BlockSpec((tf, d), lambda i, f: (f, 0)),
                  pl.BlockSpec((1, d), lambda i, f: (0, 0))],
        out_specs=pl.BlockSpec((tm, d), lambda i, f: (i, 0)),
        out_shape=jax.ShapeDtypeStruct((n, d), F32),
        compiler_params=pltpu.CompilerParams(
            dimension_semantics=("arbitrary", "arbitrary"), vmem_limit_bytes=_vmem_limit(est)),
        name="ffn",
    )(h2, x1, w1, w2, g)


def _tile(n, pref):
    t = min(n, pref)
    while n % t:
        t //= 2
    return t


INPROJ_TM = 512
ATTN_TQ = 512
MERGE_TM = 256
FFN_TM = 512
FFN_TF = 1024
CUMSUM_ROWS = 32
REPACK_ROWS = 256


def _cum_logf_rows(logf_bht):
    b, h, t = logf_bht.shape
    rows = logf_bht.reshape(b * h, t)
    pad = (-t) % LANES
    if pad:
        rows = jnp.pad(rows, ((0, 0), (0, pad)))
    return _cumsum_lanes(rows)


def _time_minor(x, nbatch):
    return jnp.transpose(x, (0, 2, 3, 1)).reshape(nbatch, -1, x.shape[1])


def _layer(xp, xs, cache_k, cache_v, cache_logf, state_pool, g_mix_pre, w_in, b_f, w_attn_up, w_pool,
           pool_scale, w_out, g_mix_post, g_ffn_pre, w_ff1, w_ff2, g_ffn_post):
    bp, tp, d = xp.shape
    bs, ts, _ = xs.shape
    past = cache_k.shape[1]
    a, nh, p = ATTN_W, N_HEADS, w_pool.shape[0] * w_pool.shape[1]
    off_f = 3 * a
    off_u = off_f + nh
    off_ga = off_u + p
    off_gb = off_ga + d

    w_qkvu, w_f, w_ga, w_gb = _repack_w_in(jnp.transpose(w_in.astype(F32)), off_f=off_f, off_u=off_u,
                                           off_ga=off_ga, off_gb=off_gb, tr=_tile(w_in.shape[0], REPACK_ROWS))
    b_f2 = jnp.pad(b_f.reshape(1, -1), ((0, 0), (0, LANES - nh)))
    row = lambda v: v.reshape(1, -1)
    g_pre, ps = row(g_mix_pre), row(pool_scale)
    g_post, g_fpre, g_fpost = row(g_mix_post), row(g_ffn_pre), row(g_ffn_post)

    def project(x3, time_minor, to_bf16=()):
        x2 = x3.reshape(-1, d)
        tm = _tile(x3.shape[1] if time_minor else x2.shape[0], INPROJ_TM)
        return x2, _inproj(x2, g_pre, w_qkvu, w_f, b_f2, tm=tm, seq=x3.shape[1], time_minor=time_minor,
                           to_bf16=to_bf16)

    def finish(x2, h, a_out, u3, hist, hist_map, nseg, tl, pos0, zero_first, ffn_w):
        tf = _tile(w_ff1.shape[1], FFN_TF)
        to_bf16 = () if ffn_w else ((w_ff1.astype(F32), tf), (w_ff2.astype(F32), None))
        x1, h2, *cast = _merge(a_out, h, u3, hist, hist_map, x2, w_up, w_ga, w_gb, w_pl, ps, w_o, g_post, g_fpre,
                               nseg=nseg, tl=tl, pos0=pos0, zero_first=zero_first, to_bf16=to_bf16)
        w1, w2 = ffn_w or cast
        return _ffn(h2, x1, w1, w2, g_fpost, tm=_tile(x2.shape[0], FFN_TM)), (w1, w2)

    merge_w = ((w_attn_up.astype(F32), None), (w_pool.astype(F32).reshape(p, -1), None), (w_out.astype(F32), None))
    x2, (h, q, kb, kt, vt, _, u, logft, w_up, w_pl, w_o) = project(xp, True, merge_w)
    w_pl = w_pl.reshape(w_pool.shape)
    c_rows = _cum_logf_rows(logft)
    nblk = nh // HEADS_PER_BLOCK
    three = lambda z: z.reshape(bp, tp, a)
    a_out = _attn_prompt(three(q), three(kb), vt, c_rows.reshape(bp, nblk, HEADS_PER_BLOCK, tp),
                         tq=_tile(tp, ATTN_TQ))
    to_bthd = lambda zt: jnp.transpose(zt.reshape(bp, nh, HEAD_DIM, tp), (0, 3, 1, 2))
    u3 = u.reshape(bp, tp, p)
    tl = _tile(tp, MERGE_TM)
    blocks_per_tile = tl // HIST_ROWS
    hist_map = lambda s, i: (s, jnp.maximum(i * blocks_per_tile - 1, 0), 0)
    yp, ffn_w = finish(x2, h, a_out.reshape(bp * tp, a), u3, u3, hist_map, 1, tl, 0, True, None)
    yp = yp.reshape(bp, tp, d)
    prompt_out = (yp, to_bthd(kt), to_bthd(vt), jnp.transpose(logft, (0, 2, 1)),
                  jnp.concatenate([jnp.zeros((bp, POOL_HIST, p), F32), u3], axis=1)[:, -POOL_HIST:])

    x2, (h, q, kb, k, v, vb, u, logf) = project(xs, False)
    logf3 = logf.reshape(bs, ts, nh)
    f_all = jnp.concatenate([cache_logf.astype(F32), logf3], axis=1)
    c_all = _cum_logf_rows(jnp.transpose(f_all, (0, 2, 1))).reshape(bs, nh, -1)
    c_cache = c_all[:, :, :past]
    c_new = c_all[:, :, past:past + ts]
    three = lambda z: z.reshape(bs, ts, a)
    a_out = _attn_sample(three(q), _time_minor(cache_k.astype(F32), bs), _time_minor(cache_v.astype(F32), bs),
                         three(kb), three(vb), c_cache, c_new, jnp.transpose(c_new, (0, 2, 1)))
    u3 = u.reshape(bs, ts, p)
    hist = jnp.pad(state_pool.astype(F32), ((0, 0), (HIST_ROWS - POOL_HIST, 0), (0, 0)))
    ys, _ = finish(x2, h, a_out.reshape(bs * ts, a), u3, hist, lambda s, i: (s, 0, 0), bs, ts, past, False, ffn_w)
    ys = ys.reshape(bs, ts, d)
    sample_out = (ys, k.reshape(bs, ts, nh, HEAD_DIM), v.reshape(bs, ts, nh, HEAD_DIM), logf3,
                  jnp.concatenate([state_pool.astype(F32), u3], axis=1)[:, -POOL_HIST:])
    return prompt_out, sample_out


def kernel(x_prompt, x_sample, cache_k, cache_v, cache_logf, state_pool, g_mix_pre, w_in, b_f, w_attn_up,
           w_pool, pool_scale, w_out, g_mix_post, g_ffn_pre, w_ff1, w_ff2, g_ffn_post):
    depth = w_in.shape[0]
    xp, xs = x_prompt, x_sample
    per_layer = []
    for l in range(depth):
        po, so = _layer(xp, xs, cache_k[l], cache_v[l], cache_logf[l], state_pool[l], g_mix_pre[l], w_in[l],
                        b_f[l], w_attn_up[l], w_pool[l], pool_scale[l], w_out[l], g_mix_post[l],
                        g_ffn_pre[l], w_ff1[l], w_ff2[l], g_ffn_post[l])
        xp, xs = po[0], so[0]
        per_layer.append(po[1:] + so[1:])
    stacked = [jnp.stack(leaves, 0) for leaves in zip(*per_layer)]
    return (xp, xs, *stacked)
```
